```python
import math, functools
import jax, jax.numpy as jnp
from jax import lax
import numpy as np

D_MODEL = 1024
BATCH = 2
SEQ = 16384
DEPTH = 1
DEC_BATCH = 16
DEC_SEQ = 64
PAST_LEN = 1024

CHUNK = 64
LEFT_CHUNKS = 8
HEAD_DIM = 128
N_HEADS = D_MODEL // HEAD_DIM
D_ATT = N_HEADS * HEAD_DIM
MAX_REL = 256
NEG_INF = -1e30
D_RNN = D_MODEL
LRU_BLOCKS = 8
LRU_BLOCK_W = D_RNN // LRU_BLOCKS
CONV_W = 4
LRU_C = 8.0
N_EXPERTS = 32
TOP_K = 4
D_FF = D_MODEL
SWIGLU_ALPHA = 1.702
SWIGLU_LIMIT = 7.0
MOE_BLOCK = 128
RMS_EPS = 1e-6

kernel_name = "hawk_chunkband_moe_stream_step"


def _rmsnorm(x, g):
    xf = x.astype(jnp.float32)
    y = xf * lax.rsqrt(jnp.mean(xf * xf, axis=-1, keepdims=True) + RMS_EPS)
    return (y * g.astype(jnp.float32)).astype(x.dtype)


def _causal_conv(x, state, w, b):
    T = x.shape[1]
    xp = jnp.concatenate([state.astype(x.dtype), x], axis=1)
    y = b + sum(xp[:, j:j + T] * w[j] for j in range(CONV_W))
    return y, xp[:, -(CONV_W - 1):]


def _lin_comb(left, right):
    a1, b1 = left
    a2, b2 = right
    return a1 * a2, a2 * b1 + b2


def _rg_lru(x, pos, h0, wa, ba, wx, bx, lam):
    B, T, C = x.shape
    xf = x.astype(jnp.float32)
    xb = xf.reshape(B, T, LRU_BLOCKS, LRU_BLOCK_W)
    r = jax.nn.sigmoid(jnp.einsum('btgi,gij->btgj', xb, wa.astype(jnp.float32)).reshape(B, T, C) + ba.astype(jnp.float32))
    i = jax.nn.sigmoid(jnp.einsum('btgi,gij->btgj', xb, wx.astype(jnp.float32)).reshape(B, T, C) + bx.astype(jnp.float32))
    log_a = -LRU_C * r * jax.nn.softplus(-lam.astype(jnp.float32))
    a = jnp.exp(log_a)
    mult = jnp.sqrt(-jnp.expm1(2.0 * log_a))
    mult = jnp.where((pos == 0)[None, :, None], 1.0, mult)
    bterm = mult * (i * xf)
    bterm = bterm.at[:, 0].add(a[:, 0] * h0.astype(jnp.float32))
    _, h = lax.associative_scan(_lin_comb, (a, bterm), axis=1)
    return h.astype(x.dtype), h[:, -1].astype(x.dtype)


def _band_attend(q, k, v, q_pos, k_pos, rel_bias):
    s = jnp.einsum('bqhd,bkhd->bhqk', q, k).astype(jnp.float32) * (HEAD_DIM ** -0.5)
    rel = jnp.clip(q_pos[:, None] - k_pos[None, :], -MAX_REL, MAX_REL) + MAX_REL
    s = s + rel_bias.astype(jnp.float32)[:, rel][None]
    qc = q_pos // CHUNK
    kc = k_pos // CHUNK
    ok = (k_pos[None, :] >= 0) & (kc[None, :] <= qc[:, None]) & (kc[None, :] >= qc[:, None] - LEFT_CHUNKS)
    s = jnp.where(ok[None, None], s, NEG_INF)
    p = jax.nn.softmax(s, axis=-1).astype(v.dtype)
    return jnp.einsum('bhqk,bkhd->bqhd', p, v)


def _prompt_band_attention(q, k, v, rel_bias):
    B, S, H, Dh = q.shape
    n_chunks = S // CHUNK
    band = LEFT_CHUNKS * CHUNK
    span = band + CHUNK
    pad = ((0, 0), (band, 0), (0, 0), (0, 0))
    kp = jnp.pad(k, pad)
    vp = jnp.pad(v, pad)
    qc = jnp.moveaxis(q.reshape(B, n_chunks, CHUNK, H, Dh), 1, 0)

    def one_chunk(args):
        c, q_blk = args
        start = c * CHUNK
        kb = lax.dynamic_slice_in_dim(kp, start, span, axis=1)
        vb = lax.dynamic_slice_in_dim(vp, start, span, axis=1)
        q_pos = start + jnp.arange(CHUNK, dtype=jnp.int32)
        k_pos = start - band + jnp.arange(span, dtype=jnp.int32)
        return _band_attend(q_blk, kb, vb, q_pos, k_pos, rel_bias)

    out = lax.map(one_chunk, (jnp.arange(n_chunks, dtype=jnp.int32), qc))
    return jnp.moveaxis(out, 0, 1).reshape(B, S, H, Dh)


def _sample_band_attention(q, k, v, k_cache, v_cache, q_pos, cache_pos, rel_bias):
    k_all = jnp.concatenate([k_cache.astype(k.dtype), k], axis=1)
    v_all = jnp.concatenate([v_cache.astype(v.dtype), v], axis=1)
    k_pos = jnp.concatenate([cache_pos, q_pos])
    return _band_attend(q, k_all, v_all, q_pos, k_pos, rel_bias)


def _swiglu_clamped(h):
    g = jnp.minimum(h[..., ::2], SWIGLU_LIMIT)
    lin = jnp.clip(h[..., 1::2], -SWIGLU_LIMIT, SWIGLU_LIMIT)
    return g * jax.nn.sigmoid(SWIGLU_ALPHA * g) * (lin + 1.0)


def _moe(x, router_w, router_b, w1, b1, w2, b2):
    T, D = x.shape
    N = T * TOP_K
    logits = (x @ router_w).astype(jnp.float32) + router_b.astype(jnp.float32)
    top_v, top_e = lax.top_k(logits, TOP_K)
    gate = jax.nn.softmax(top_v, axis=-1).astype(x.dtype)
    flat_e = top_e.reshape(N)
    flat_t = jnp.repeat(jnp.arange(T, dtype=jnp.int32), TOP_K)
    flat_g = gate.reshape(N)
    order = jnp.argsort(flat_e)
    se = flat_e[order]
    counts = jnp.bincount(flat_e, length=N_EXPERTS)
    pcounts = (counts + MOE_BLOCK - 1) // MOE_BLOCK * MOE_BLOCK
    start = jnp.cumsum(counts) - counts
    pend = jnp.cumsum(pcounts)
    pstart = pend - pcounts
    dest = pstart[se] + jnp.arange(N, dtype=jnp.int32) - start[se]
    n_blocks = -(-N // MOE_BLOCK) + N_EXPERTS
    n_slots = n_blocks * MOE_BLOCK
    slot_tok = jnp.full((n_slots,), T, jnp.int32).at[dest].set(flat_t[order])
    slot_g = jnp.zeros((n_slots,), x.dtype).at[dest].set(flat_g[order])
    block_e = jnp.minimum(jnp.searchsorted(pend // MOE_BLOCK, jnp.arange(n_blocks), side='right'), N_EXPERTS - 1)
    xs = jnp.concatenate([x, jnp.zeros((1, D), x.dtype)], axis=0)[slot_tok].reshape(n_blocks, MOE_BLOCK, D)

    def expert_block(args):
        xb, e = args
        h = xb @ w1[e] + b1[e]
        return _swiglu_clamped(h) @ w2[e] + b2[e]

    ys = lax.map(expert_block, (xs, block_e)).reshape(n_slots, D)
    out = jax.ops.segment_sum(ys * slot_g[:, None], slot_tok, num_segments=T + 1)
    return out[:T]


def _layer(x, pos, conv_state, h0, attend, lw):
    (g_mix, w_in, conv_w, conv_b, wa, ba, wx, bx, lam, w_out,
     g_ffn, r_w, r_b, w1, b1, w2, b2) = lw
    B, T, _ = x.shape
    u = _rmsnorm(x, g_mix)
    z = u @ w_in
    cuts = [D_RNN, D_RNN + D_ATT, D_RNN + 2 * D_ATT, D_RNN + 3 * D_ATT, D_RNN + 3 * D_ATT + D_MODEL]
    x_rnn, q, k, v, gate_rnn, gate_att = jnp.split(z, cuts, axis=-1)
    xc, conv_new = _causal_conv(x_rnn, conv_state, conv_w, conv_b)
    h_seq, h_last = _rg_lru(xc, pos, h0, wa, ba, wx, bx, lam)
    q = q.reshape(B, T, N_HEADS, HEAD_DIM)
    k = k.reshape(B, T, N_HEADS, HEAD_DIM)
    v = v.reshape(B, T, N_HEADS, HEAD_DIM)
    att = attend(q, k, v).reshape(B, T, D_ATT)
    mixed = jax.nn.sigmoid(gate_rnn) * h_seq + jax.nn.sigmoid(gate_att) * att
    x = x + mixed @ w_out
    x = x + _moe(_rmsnorm(x, g_ffn).reshape(B * T, D_MODEL), r_w, r_b, w1, b1, w2, b2).reshape(B, T, D_MODEL)
    return x, conv_new, h_last, k, v


def setup_inputs(seed: int = 0) -> dict:
    key = jax.random.key(seed)
    ks = jax.random.split(key, 26)
    f32 = jnp.float32
    n_cache = min(LEFT_CHUNKS * CHUNK, PAST_LEN)
    d_in = D_RNN + 3 * D_ATT + 2 * D_MODEL

    def nrm(k, shape, scale):
        return scale * jax.random.normal(k, shape, f32)

    a0 = jax.random.uniform(ks[10], (DEPTH, D_RNN), f32, 0.9, 0.999)
    s = a0 ** (1.0 / LRU_C)
    lam = jnp.log(s) - jnp.log1p(-s)
    return {
        "x_prompt": nrm(ks[0], (BATCH, SEQ, D_MODEL), 1.0),
        "x_sample": nrm(ks[1], (DEC_BATCH, DEC_SEQ, D_MODEL), 1.0),
        "cache_k": nrm(ks[2], (DEPTH, DEC_BATCH, n_cache, N_HEADS, HEAD_DIM), 1.0),
        "cache_v": nrm(ks[3], (DEPTH, DEC_BATCH, n_cache, N_HEADS, HEAD_DIM), 1.0),
        "state_conv": nrm(ks[4], (DEPTH, DEC_BATCH, CONV_W - 1, D_RNN), 1.0),
        "state_lru": nrm(ks[5], (DEPTH, DEC_BATCH, D_RNN), 0.5),
        "norm_mix": 1.0 + nrm(ks[6], (DEPTH, D_MODEL), 0.02),
        "w_in": nrm(ks[7], (DEPTH, D_MODEL, d_in), D_MODEL ** -0.5),
        "conv_w": nrm(ks[8], (DEPTH, CONV_W, D_RNN), CONV_W ** -0.5),
        "conv_b": nrm(ks[9], (DEPTH, D_RNN), 0.01),
        "lru_wa": nrm(ks[11], (DEPTH, LRU_BLOCKS, LRU_BLOCK_W, LRU_BLOCK_W), LRU_BLOCK_W ** -0.5),
        "lru_ba": nrm(ks[12], (DEPTH, D_RNN), 0.01),
        "lru_wx": nrm(ks[13], (DEPTH, LRU_BLOCKS, LRU_BLOCK_W, LRU_BLOCK_W), LRU_BLOCK_W ** -0.5),
        "lru_bx": nrm(ks[14], (DEPTH, D_RNN), 0.01),
        "lru_lambda": lam,
        "rel_bias": nrm(ks[15], (DEPTH, N_HEADS, 2 * MAX_REL + 1), 0.1),
        "w_out": nrm(ks[16], (DEPTH, D_MODEL, D_MODEL), D_MODEL ** -0.5),
        "norm_ffn": 1.0 + nrm(ks[17], (DEPTH, D_MODEL), 0.02),
        "router_w": nrm(ks[18], (DEPTH, D_MODEL, N_EXPERTS), D_MODEL ** -0.5),
        "router_b": nrm(ks[19], (DEPTH, N_EXPERTS), 0.01),
        "w1": nrm(ks[20], (DEPTH, N_EXPERTS, D_MODEL, 2 * D_FF), D_MODEL ** -0.5),
        "b1": nrm(ks[21], (DEPTH, N_EXPERTS, 2 * D_FF), 0.01),
        "w2": nrm(ks[22], (DEPTH, N_EXPERTS, D_FF, D_MODEL), D_FF ** -0.5),
        "b2": nrm(ks[23], (DEPTH, N_EXPERTS, D_MODEL), 0.01),
        "norm_out": 1.0 + nrm(ks[24], (D_MODEL,), 0.02),
    }


def reference(x_prompt, x_sample, cache_k, cache_v, state_conv, state_lru,
              norm_mix, w_in, conv_w, conv_b, lru_wa, lru_ba, lru_wx, lru_bx, lru_lambda,
              rel_bias, w_out, norm_ffn, router_w, router_b, w1, b1, w2, b2, norm_out):
    Bp, Sp, _ = x_prompt.shape
    Bs, Ss, _ = x_sample.shape
    pos_p = jnp.arange(Sp, dtype=jnp.int32)
    pos_s = PAST_LEN + jnp.arange(Ss, dtype=jnp.int32)
    n_cache = cache_k.shape[2]
    cache_pos = PAST_LEN - n_cache + jnp.arange(n_cache, dtype=jnp.int32)
    n_keep = min(LEFT_CHUNKS * CHUNK, Sp)

    hp, hs = x_prompt, x_sample
    kp_l, vp_l, cp_l, lp_l = [], [], [], []
    ks_l, vs_l, cs_l, ls_l = [], [], [], []
    for l in range(DEPTH):
        lw = (norm_mix[l], w_in[l], conv_w[l], conv_b[l], lru_wa[l], lru_ba[l], lru_wx[l], lru_bx[l],
              lru_lambda[l], w_out[l], norm_ffn[l], router_w[l], router_b[l], w1[l], b1[l], w2[l], b2[l])
        attend_p = functools.partial(_prompt_band_attention, rel_bias=rel_bias[l])
        conv0 = jnp.zeros((Bp, CONV_W - 1, D_RNN), hp.dtype)
        lru0 = jnp.zeros((Bp, D_RNN), hp.dtype)
        hp, c_p, h_p, k_p, v_p = _layer(hp, pos_p, conv0, lru0, attend_p, lw)
        kp_l.append(k_p[:, -n_keep:])
        vp_l.append(v_p[:, -n_keep:])
        cp_l.append(c_p)
        lp_l.append(h_p)

        attend_s = functools.partial(_sample_band_attention, k_cache=cache_k[l], v_cache=cache_v[l],
                                     q_pos=pos_s, cache_pos=cache_pos, rel_bias=rel_bias[l])
        hs, c_s, h_s, k_s, v_s = _layer(hs, pos_s, state_conv[l], state_lru[l], attend_s, lw)
        ks_l.append(k_s)
        vs_l.append(v_s)
        cs_l.append(c_s)
        ls_l.append(h_s)

    y_prompt = _rmsnorm(hp, norm_out)
    y_sample = _rmsnorm(hs, norm_out)
    return (y_prompt, y_sample,
            jnp.stack(kp_l), jnp.stack(vp_l), jnp.stack(cp_l), jnp.stack(lp_l),
            jnp.stack(ks_l), jnp.stack(vs_l), jnp.stack(cs_l), jnp.stack(ls_l))
```

```python
import functools

import jax
import jax.numpy as jnp
from jax import lax
from jax.experimental import pallas as pl
from jax.experimental.pallas import tpu as pltpu

F32 = jnp.float32
BF16 = jnp.bfloat16
I32 = jnp.int32

CHUNK = 64
LEFT_CHUNKS = 8
HEAD_DIM = 128
MAX_REL = 256
NEG_INF = -1e30
LRU_BLOCKS = 8
CONV_W = 4
LRU_C = 8.0
TOP_K = 4
SWIGLU_ALPHA = 1.702
SWIGLU_LIMIT = 7.0
RMS_EPS = 1e-6
PAST_LEN = 1024

LANES = 128
SUBLANES = 8
VMEM_LIMIT = 56 * 1024 * 1024
MOE_BLOCK = 128
ROUTE_TILE = 512
ROW_TILE = 512
ATT_GROUP = 4


def _cparams(sem):
    return pltpu.CompilerParams(dimension_semantics=sem, vmem_limit_bytes=VMEM_LIMIT)


def _sigmoid(x):
    return 1.0 / (1.0 + jnp.exp(-x))


def _rms(x, g):
    ms = jnp.mean(x * x, axis=-1, keepdims=True)
    return x * lax.rsqrt(ms + RMS_EPS) * g


def _inproj_kernel(x_ref, g_ref, w_ref, xr_ref, q_ref, k_ref, v_ref, gr_ref, ga_ref,
                   kt_ref, vt_ref, *, d):
    u = _rms(x_ref[...], g_ref[...]).astype(BF16)

    def piece(j):
        return jnp.dot(u, w_ref[:, j * d:(j + 1) * d], preferred_element_type=F32)

    xr_ref[...] = piece(0)
    q_ref[...] = piece(1).astype(BF16)
    kf = piece(2)
    k_ref[...] = kf.astype(BF16)
    kt_ref[...] = kf
    vf = piece(3)
    v_ref[...] = vf.astype(BF16)
    vt_ref[...] = vf
    gr_ref[...] = piece(4).astype(BF16)
    ga_ref[...] = piece(5).astype(BF16)


def _inproj(x2d, g, w_bf, *, rows_per_batch, n_keep, tm):
    n, d = x2d.shape
    nt = n // tm
    if n_keep == rows_per_batch:
        tail_rows = n
        tail_map = lambda i: (i, 0)
    else:
        ntb = rows_per_batch // tm
        nk = n_keep // tm
        tail_rows = (n // rows_per_batch) * n_keep
        tail_map = lambda i: ((i // ntb) * nk + jnp.maximum(i % ntb - (ntb - nk), 0), 0)
    row = pl.BlockSpec((tm, d), lambda i: (i, 0))
    outs = pl.pallas_call(
        functools.partial(_inproj_kernel, d=d),
        grid=(nt,),
        in_specs=[row,
                  pl.BlockSpec((1, d), lambda i: (0, 0)),
                  pl.BlockSpec((d, 6 * d), lambda i: (0, 0), pipeline_mode=pl.Buffered(1))],
        out_specs=[row, row, row, row, row, row,
                   pl.BlockSpec((tm, d), tail_map), pl.BlockSpec((tm, d), tail_map)],
        out_shape=[jax.ShapeDtypeStruct((n, d), F32),
                   jax.ShapeDtypeStruct((n, d), BF16),
                   jax.ShapeDtypeStruct((n, d), BF16),
                   jax.ShapeDtypeStruct((n, d), BF16),
                   jax.ShapeDtypeStruct((n, d), BF16),
                   jax.ShapeDtypeStruct((n, d), BF16),
                   jax.ShapeDtypeStruct((tail_rows, d), F32),
                   jax.ShapeDtypeStruct((tail_rows, d), F32)],
        compiler_params=_cparams(("arbitrary",)),
        name="inproj",
    )(x2d, g, w_bf)
    return outs


def _lru_kernel(xr_ref, gr_ref, cs_ref, h0_ref, cw_ref, cb_ref, wg_ref, ba_ref, bx_ref, lam_ref,
                mr_ref, cnew_ref, hlast_ref, xp_s, a_s, b_s, hs_s, h_s, *, ts, starts_at_zero):
    t = pl.program_id(1)
    head = SUBLANES
    hist = CONV_W - 1

    @pl.when(t == 0)
    def _():
        xp_s[0:head, :] = jnp.zeros((head, xp_s.shape[1]), F32)
        xp_s[head - hist:head, :] = cs_ref[0]
        h_s[...] = h0_ref[0]

    xp_s[head:head + ts, :] = xr_ref[...]
    cw = cw_ref[...]
    xc = cb_ref[...] + xp_s[head - hist:head - hist + ts, :] * cw[0:1, :]
    for j in range(1, CONV_W):
        xc = xc + xp_s[head - hist + j:head - hist + j + ts, :] * cw[j:j + 1, :]
    tail = xp_s[ts + head - hist:ts + head, :]
    xp_s[head - hist:head, :] = tail
    cnew_ref[0] = tail

    lam = lam_ref[...]
    z = -lam
    softplus = jnp.maximum(z, 0.0) + jnp.log1p(jnp.exp(-jnp.abs(z)))
    c = -LRU_C * softplus
    bw = xc.shape[1] // LRU_BLOCKS
    if starts_at_zero:
        first = (lax.broadcasted_iota(I32, (ts, bw), 0) + t * ts) == 0
    for g in range(LRU_BLOCKS):
        sl = slice(g * bw, (g + 1) * bw)
        xg = xc[:, sl]
        pre = jnp.dot(xg.astype(BF16), wg_ref[g], preferred_element_type=F32)
        r = _sigmoid(pre[:, :bw] + ba_ref[:, sl])
        ig = _sigmoid(pre[:, bw:] + bx_ref[:, sl])
        log_a = c[:, sl] * r
        a = jnp.exp(log_a)
        mult = jnp.sqrt(1.0 - a * a)
        if starts_at_zero:
            mult = jnp.where(first, 1.0, mult)
        a_s[:, sl] = a
        b_s[:, sl] = mult * (ig * xg)

    def scan_body(i, h):
        base = pl.multiple_of(i * SUBLANES, SUBLANES)
        a8 = a_s[pl.ds(base, SUBLANES), :]
        b8 = b_s[pl.ds(base, SUBLANES), :]
        rows = []
        for j in range(SUBLANES):
            h = a8[j:j + 1, :] * h + b8[j:j + 1, :]
            rows.append(h)
        hs_s[pl.ds(base, SUBLANES), :] = jnp.concatenate(rows, axis=0)
        return h

    h = lax.fori_loop(0, ts // SUBLANES, scan_body, h_s[...])
    h_s[...] = h
    hlast_ref[0] = h
    mr_ref[...] = (_sigmoid(gr_ref[...].astype(F32)) * hs_s[...]).astype(BF16)


def _lru(xr, gr, conv_state, h0, conv_w, conv_b, wg, ba, bx, lam, *, batch, seq, ts, starts_at_zero):
    n, d = xr.shape
    ntb = seq // ts
    row = pl.BlockSpec((ts, d), lambda b, t: (b * ntb + t, 0))
    vec = pl.BlockSpec((1, d), lambda b, t: (0, 0))
    return pl.pallas_call(
        functools.partial(_lru_kernel, ts=ts, starts_at_zero=starts_at_zero),
        grid=(batch, ntb),
        in_specs=[row, row,
                  pl.BlockSpec((1, CONV_W - 1, d), lambda b, t: (b, 0, 0)),
                  pl.BlockSpec((1, 1, d), lambda b, t: (b, 0, 0)),
                  pl.BlockSpec((CONV_W, d), lambda b, t: (0, 0)),
                  vec,
                  pl.BlockSpec(wg.shape, lambda b, t: (0, 0, 0)),
                  vec, vec, vec],
        out_specs=[row,
                   pl.BlockSpec((1, CONV_W - 1, d), lambda b, t: (b, 0, 0)),
                   pl.BlockSpec((1, 1, d), lambda b, t: (b, 0, 0))],
        out_shape=[jax.ShapeDtypeStruct((n, d), BF16),
                   jax.ShapeDtypeStruct((batch, CONV_W - 1, d), F32),
                   jax.ShapeDtypeStruct((batch, 1, d), F32)],
        scratch_shapes=[pltpu.VMEM((ts + SUBLANES, d), F32),
                        pltpu.VMEM((ts, d), F32),
                        pltpu.VMEM((ts, d), F32),
                        pltpu.VMEM((ts, d), F32),
                        pltpu.VMEM((1, d), F32)],
        compiler_params=_cparams(("arbitrary", "arbitrary")),
        name="lru",
    )(xr, gr, conv_state, h0, conv_w, conv_b, wg, ba, bx, lam)


def _bias_kernel(f_ref, o_ref, *, nq, nkeys, width):
    f = f_ref[0]
    x = jnp.broadcast_to(f, (nq, width))
    rolled = pltpu.roll(x, width - (nq - 1), 1, stride=1, stride_axis=0)
    t = rolled[:, :nkeys]
    qc = lax.broadcasted_iota(I32, (nq, nkeys), 0) // CHUNK
    kc = lax.broadcasted_iota(I32, (nq, nkeys), 1) // CHUNK
    ok = jnp.logical_and(kc >= qc, kc <= qc + LEFT_CHUNKS)
    o_ref[0] = jnp.where(ok, t, NEG_INF)


def _bias_table(rel_bias, nq, nkeys):
    nh = rel_bias.shape[0]
    band = LEFT_CHUNKS * CHUNK
    width = -(-(nkeys + nq) // LANES) * LANES
    left = band + (nq - 1) - MAX_REL
    right = max(width - left - (2 * MAX_REL + 1), 0)
    flipped = rel_bias[:, ::-1]
    f = jnp.pad(flipped, ((0, 0), (left, right)), mode="edge")[:, :width].reshape(nh, 1, width)
    return pl.pallas_call(
        functools.partial(_bias_kernel, nq=nq, nkeys=nkeys, width=width),
        grid=(nh,),
        in_specs=[pl.BlockSpec((1, 1, width), lambda h: (h, 0, 0))],
        out_specs=pl.BlockSpec((1, nq, nkeys), lambda h: (h, 0, 0)),
        out_shape=jax.ShapeDtypeStruct((nh, nq, nkeys), F32),
        compiler_params=_cparams(("arbitrary",)),
        name="bias",
    )(f)


def _attn_kernel(*refs, part_rows, n_heads, mask_parts):
    np_ = len(part_rows)
    q_ref = refs[0]
    k_refs = refs[1:1 + np_]
    v_refs = refs[1 + np_:1 + 2 * np_]
    bias_ref, ga_ref, mr_ref, o_ref = refs[1 + 2 * np_:]
    i = pl.program_id(1)
    scale = HEAD_DIM ** -0.5
    for h in range(n_heads):
        hs = slice(h * HEAD_DIM, (h + 1) * HEAD_DIM)
        qh = (q_ref[:, hs].astype(F32) * scale).astype(BF16)
        s_parts = []
        off = 0
        for p in range(np_):
            kp = k_refs[p][:, hs].astype(BF16)
            s = lax.dot_general(qh, kp, (((1,), (1,)), ((), ())), preferred_element_type=F32)
            s = s + bias_ref[h, :, off:off + part_rows[p]]
            if mask_parts and p < np_ - 1:
                s = jnp.where(i < (np_ - 1 - p), NEG_INF, s)
            s_parts.append(s)
            off += part_rows[p]
        m = s_parts[0].max(axis=-1, keepdims=True)
        for s in s_parts[1:]:
            m = jnp.maximum(m, s.max(axis=-1, keepdims=True))
        l = None
        o = None
        for p in range(np_):
            e = jnp.exp(s_parts[p] - m)
            lp = e.sum(axis=-1, keepdims=True)
            op = jnp.dot(e.astype(BF16), v_refs[p][:, hs].astype(BF16), preferred_element_type=F32)
            l = lp if l is None else l + lp
            o = op if o is None else o + op
        att = o / l
        mixed = mr_ref[:, hs].astype(F32) + _sigmoid(ga_ref[:, hs].astype(F32)) * att
        o_ref[:, hs] = mixed.astype(BF16)


def _attn_prompt(q, k, v, bias, ga, mr, *, batch, seq):
    n, d = q.shape
    gq = ATT_GROUP * CHUNK
    nparts = LEFT_CHUNKS // ATT_GROUP + 1
    ng = seq // gq
    cur = pl.BlockSpec((gq, d), lambda b, i: (b * ng + i, 0))

    def back(p):
        return pl.BlockSpec((gq, d), lambda b, i: (b * ng + jnp.maximum(i - p, 0), 0))

    kv_specs = [back(nparts - 1 - p) for p in range(nparts)]
    n_heads = d // HEAD_DIM
    return pl.pallas_call(
        functools.partial(_attn_kernel, part_rows=(gq,) * nparts, n_heads=n_heads, mask_parts=True),
        grid=(batch, ng),
        in_specs=[cur] + kv_specs + kv_specs
                 + [pl.BlockSpec(bias.shape, lambda b, i: (0, 0, 0), pipeline_mode=pl.Buffered(1)), cur, cur],
        out_specs=cur,
        out_shape=jax.ShapeDtypeStruct((n, d), BF16),
        compiler_params=_cparams(("arbitrary", "arbitrary")),
        name="attn_prompt",
    )(q, *([k] * nparts), *([v] * nparts), bias, ga, mr)


def _attn_sample(q, k, v, ck, cv, bias, ga, mr, *, batch, seq):
    n, d = q.shape
    nc = ck.shape[0] // batch
    cur = pl.BlockSpec((seq, d), lambda b, i: (b, 0))
    cache = pl.BlockSpec((nc, d), lambda b, i: (b, 0))
    n_heads = d // HEAD_DIM
    return pl.pallas_call(
        functools.partial(_attn_kernel, part_rows=(nc, seq), n_heads=n_heads, mask_parts=False),
        grid=(batch, 1),
        in_specs=[cur, cache, cur, cache, cur,
                  pl.BlockSpec(bias.shape, lambda b, i: (0, 0, 0), pipeline_mode=pl.Buffered(1)), cur, cur],
        out_specs=cur,
        out_shape=jax.ShapeDtypeStruct((n, d), BF16),
        compiler_params=_cparams(("arbitrary", "arbitrary")),
        name="attn_sample",
    )(q, ck, k, cv, v, bias, ga, mr)


def _store_token_tiles(ref, x):
    rows = x.shape[0]
    for c in range(x.shape[1] // LANES):
        ref[pl.ds(c, rows, stride=SUBLANES), :] = x[:, c * LANES:(c + 1) * LANES]


def _load_token_tiles(ref, rows):
    return jnp.concatenate([ref[pl.ds(c, rows, stride=SUBLANES), :] for c in range(SUBLANES)], axis=1)


def _split_bf16(x):
    hi = x.astype(BF16)
    lo = (x - hi.astype(F32)).astype(BF16)
    return hi, lo


def _outproj_kernel(mp_ref, xp_ref, ms_ref, xs_ref, w_ref, g_ref, rwt_ref, rb_ref,
                    x2_ref, hn_ref, lg_ref, *, ntp):
    def run(mixed_ref, x_ref):
        x2 = x_ref[...] + jnp.dot(mixed_ref[...], w_ref[...], preferred_element_type=F32)
        x2_ref[...] = x2
        hn = _rms(x2, g_ref[...])
        _store_token_tiles(hn_ref, hn)
        h_hi, h_lo = _split_bf16(hn)
        w_hi, w_lo = _split_bf16(rwt_ref[...])
        nt = (((1,), (1,)), ((), ()))
        lg = lax.dot_general(w_hi, h_hi, nt, preferred_element_type=F32)
        lg = lg + lax.dot_general(w_hi, h_lo, nt, preferred_element_type=F32)
        lg = lg + lax.dot_general(w_lo, h_hi, nt, preferred_element_type=F32)
        lg_ref[...] = lg + rb_ref[...]

    i = pl.program_id(0)

    @pl.when(i < ntp)
    def _():
        run(mp_ref, xp_ref)

    @pl.when(i >= ntp)
    def _():
        run(ms_ref, xs_ref)


def _outproj(mixed_p, xp2d, mixed_s, xs2d, w_bf, g, rwt, rb, *, tm):
    tp, d = xp2d.shape
    tsm = xs2d.shape[0]
    ne = rwt.shape[0]
    ntp = tp // tm
    nts = tsm // tm
    total_rows = tp + tsm
    prow = pl.BlockSpec((tm, d), lambda i: (jnp.minimum(i, ntp - 1), 0))
    srow = pl.BlockSpec((tm, d), lambda i: (jnp.maximum(i - ntp, 0), 0))
    full = lambda shape: pl.BlockSpec(shape, lambda i: (0,) * len(shape))
    return pl.pallas_call(
        functools.partial(_outproj_kernel, ntp=ntp),
        grid=(ntp + nts,),
        in_specs=[prow, prow, srow, srow, full((d, d)), full((1, d)), full((ne, d)), full((ne, 1))],
        out_specs=[pl.BlockSpec((tm, d), lambda i: (i, 0)),
                   pl.BlockSpec((tm * SUBLANES, LANES), lambda i: (i, 0)),
                   pl.BlockSpec((ne, tm), lambda i: (0, i))],
        out_shape=[jax.ShapeDtypeStruct((total_rows, d), F32),
                   jax.ShapeDtypeStruct((total_rows * SUBLANES, LANES), F32),
                   jax.ShapeDtypeStruct((ne, total_rows), F32)],
        compiler_params=_cparams(("arbitrary",)),
        name="outproj",
    )(mixed_p, xp2d, mixed_s, xs2d, w_bf, g, rwt, rb)


def _route_kernel(lg_ref, e_ref, r_ref, g_ref, cnt_ref, carry_s, *, tr):
    @pl.when(pl.program_id(0) == 0)
    def _():
        carry_s[...] = jnp.zeros(carry_s.shape, F32)

    work = lg_ref[...]
    ne = work.shape[0]
    eid = lax.broadcasted_iota(I32, (ne, tr), 0).astype(F32)
    sels, vals, idxs = [], [], []
    for _ in range(TOP_K):
        m = work.max(axis=0, keepdims=True)
        idx = jnp.where(work == m, eid, float(ne)).min(axis=0, keepdims=True)
        sel = eid == idx
        sels.append(sel)
        vals.append(m)
        idxs.append(idx)
        work = jnp.where(sel, -jnp.inf, work)
    ex = [jnp.exp(v - vals[0]) for v in vals]
    den = ex[0] + ex[1] + ex[2] + ex[3]
    chosen = jnp.zeros((ne, tr), F32)
    for sel in sels:
        chosen = chosen + sel.astype(F32)
    rr = lax.broadcasted_iota(I32, (tr, tr), 0)
    cc = lax.broadcasted_iota(I32, (tr, tr), 1)
    upper = jnp.where(rr < cc, 1.0, 0.0).astype(BF16)
    before = jnp.dot(chosen.astype(BF16), upper, preferred_element_type=F32)
    carry = carry_s[:, 0:1]
    rank_all = before + carry
    for k in range(TOP_K):
        e_ref[k:k + 1, :] = idxs[k].astype(I32)
        g_ref[k:k + 1, :] = ex[k] / den
        r_ref[k:k + 1, :] = jnp.where(sels[k], rank_all, 0.0).sum(axis=0, keepdims=True).astype(I32)
    new_carry = carry + chosen.sum(axis=1, keepdims=True)
    carry_s[...] = jnp.broadcast_to(new_carry, carry_s.shape)
    cnt_ref[...] = jnp.broadcast_to(new_carry, cnt_ref.shape).astype(I32)


def _route(lgt, *, tr):
    ne, t = lgt.shape
    blk = lambda rows: pl.BlockSpec((rows, tr), lambda i: (0, i))
    return pl.pallas_call(
        functools.partial(_route_kernel, tr=tr),
        grid=(t // tr,),
        in_specs=[blk(ne)],
        out_specs=[blk(TOP_K), blk(TOP_K), blk(TOP_K), pl.BlockSpec((ne, LANES), lambda i: (0, 0))],
        out_shape=[jax.ShapeDtypeStruct((TOP_K, t), I32),
                   jax.ShapeDtypeStruct((TOP_K, t), I32),
                   jax.ShapeDtypeStruct((TOP_K, t), F32),
                   jax.ShapeDtypeStruct((ne, LANES), I32)],
        scratch_shapes=[pltpu.VMEM((ne, LANES), F32)],
        compiler_params=_cparams(("arbitrary",)),
        name="route",
    )(lgt)


def _plan_kernel(cnt_ref, meta_ref, be_ref, *, ne, nblocks, blk):
    acc = jnp.int32(0)
    for e in range(ne):
        c = cnt_ref[e, 0]
        pc = ((c + (blk - 1)) // blk) * blk
        meta_ref[e] = acc
        b0 = acc // blk
        acc = acc + pc
        meta_ref[ne + e] = acc
        b1 = acc // blk

        def fill(b, carry, e=e):
            be_ref[b] = jnp.int32(e)
            return carry

        lax.fori_loop(b0, b1, fill, 0)
    used = acc // blk
    meta_ref[2 * ne] = used

    def fill_rest(b, carry):
        be_ref[b] = jnp.int32(ne - 1)
        return carry

    lax.fori_loop(used, nblocks, fill_rest, 0)
    for j in range(2 * ne + 1, meta_ref.shape[0]):
        meta_ref[j] = jnp.int32(0)


def _plan(counts, *, nblocks, blk):
    ne = counts.shape[0]
    smem = pl.BlockSpec(memory_space=pltpu.SMEM)
    return pl.pallas_call(
        functools.partial(_plan_kernel, ne=ne, nblocks=nblocks, blk=blk),
        in_specs=[smem],
        out_specs=[smem, smem],
        out_shape=[jax.ShapeDtypeStruct((LANES,), I32), jax.ShapeDtypeStruct((nblocks,), I32)],
        name="plan",
    )(counts)


def _dispatch_kernel(meta_ref, e_ref, r_ref, hn_ref, xs_ref, zero_s, sem, *, td, ne, blk, nblocks):
    i = pl.program_id(0)

    @pl.when(i == 0)
    def _():
        zero_s[...] = jnp.zeros(zero_s.shape, F32)

        def zcopy(e):
            row0 = pl.multiple_of(jnp.maximum(meta_ref[ne + e] - blk, 0) * SUBLANES, SUBLANES)
            return pltpu.make_async_copy(zero_s, xs_ref.at[pl.ds(row0, blk * SUBLANES), :], sem)

        for e in range(ne):
            zcopy(e).start()
        for e in range(ne):
            zcopy(e).wait()

        def ztail(b, carry):
            row0 = pl.multiple_of(b * (blk * SUBLANES), blk * SUBLANES)
            cp = pltpu.make_async_copy(zero_s, xs_ref.at[pl.ds(row0, blk * SUBLANES), :], sem)
            cp.start()
            cp.wait()
            return carry

        lax.fori_loop(meta_ref[2 * ne], nblocks, ztail, 0)

    def tile_copy(r, slot):
        src = hn_ref.at[pl.ds(pl.multiple_of(r * SUBLANES, SUBLANES), SUBLANES), :]
        dst = xs_ref.at[pl.ds(pl.multiple_of(slot * SUBLANES, SUBLANES), SUBLANES), :]
        return pltpu.make_async_copy(src, dst, sem)

    def issue(r, carry):
        for k in range(TOP_K):
            tile_copy(r, meta_ref[e_ref[k, r]] + r_ref[k, r]).start()
        return carry

    lax.fori_loop(0, td, issue, 0)

    def drain(r, carry):
        for k in range(TOP_K):
            tile_copy(r, 0).wait()
        return carry

    lax.fori_loop(0, td, drain, 0)


def _dispatch(meta, e_all, r_all, hn_all, *, n_slots, td, blk, ne):
    t = hn_all.shape[0] // SUBLANES
    grid_spec = pltpu.PrefetchScalarGridSpec(
        num_scalar_prefetch=1,
        grid=(t // td,),
        in_specs=[pl.BlockSpec((TOP_K, td), lambda i, m: (0, i), memory_space=pltpu.SMEM),
                  pl.BlockSpec((TOP_K, td), lambda i, m: (0, i), memory_space=pltpu.SMEM),
                  pl.BlockSpec((td * SUBLANES, LANES), lambda i, m: (i, 0))],
        out_specs=pl.BlockSpec(memory_space=pl.ANY),
        scratch_shapes=[pltpu.VMEM((blk * SUBLANES, LANES), F32), pltpu.SemaphoreType.DMA(())],
    )
    return pl.pallas_call(
        functools.partial(_dispatch_kernel, td=td, ne=ne, blk=blk, nblocks=n_slots // blk),
        grid_spec=grid_spec,
        out_shape=jax.ShapeDtypeStruct((n_slots * SUBLANES, LANES), F32),
        compiler_params=_cparams(("arbitrary",)),
        name="dispatch",
    )(meta, e_all, r_all, hn_all)


def _experts_kernel(be_ref, meta_ref, x_ref, w1g_ref, w1l_ref, b1g_ref, b1l_ref, w2_ref, b2_ref,
                    y_ref, *, ne, blk):
    b = pl.program_id(0)

    @pl.when(b < meta_ref[2 * ne])
    def _():
        xb = _load_token_tiles(x_ref, blk).astype(BF16)
        hg = jnp.dot(xb, w1g_ref[0], preferred_element_type=F32) + b1g_ref[0]
        hl = jnp.dot(xb, w1l_ref[0], preferred_element_type=F32) + b1l_ref[0]
        g = jnp.minimum(hg, SWIGLU_LIMIT)
        lin = jnp.clip(hl, -SWIGLU_LIMIT, SWIGLU_LIMIT)
        act = g * _sigmoid(SWIGLU_ALPHA * g) * (lin + 1.0)
        y = jnp.dot(act.astype(BF16), w2_ref[0], preferred_element_type=F32) + b2_ref[0]
        _store_token_tiles(y_ref, y)

    @pl.when(b >= meta_ref[2 * ne])
    def _():
        y_ref[...] = jnp.zeros(y_ref.shape, F32)


def _experts(be, meta, xs, w1g, w1l, b1g, b1l, w2, b2, *, blk):
    ne, d, dff = w1g.shape
    n_slots = xs.shape[0] // SUBLANES
    nblocks = n_slots // blk

    def xmap(b, be_r, meta_r):
        return (jnp.minimum(b, meta_r[2 * ne] - 1), 0)

    def wmap(b, be_r, meta_r):
        return (be_r[b], 0, 0)

    grid_spec = pltpu.PrefetchScalarGridSpec(
        num_scalar_prefetch=2,
        grid=(nblocks,),
        in_specs=[pl.BlockSpec((blk * SUBLANES, LANES), xmap),
                  pl.BlockSpec((1, d, dff), wmap),
                  pl.BlockSpec((1, d, dff), wmap),
                  pl.BlockSpec((1, 1, dff), wmap),
                  pl.BlockSpec((1, 1, dff), wmap),
                  pl.BlockSpec((1, dff, d), wmap),
                  pl.BlockSpec((1, 1, d), wmap)],
        out_specs=pl.BlockSpec((blk * SUBLANES, LANES), lambda b, be_r, meta_r: (b, 0)),
    )
    return pl.pallas_call(
        functools.partial(_experts_kernel, ne=ne, blk=blk),
        grid_spec=grid_spec,
        out_shape=jax.ShapeDtypeStruct((n_slots * SUBLANES, LANES), F32),
        compiler_params=_cparams(("arbitrary",)),
        name="experts",
    )(be, meta, xs, w1g, w1l, b1g, b1l, w2, b2)


def _combine_kernel(meta_ref, e_ref, r_ref, x2_ref, gt_ref, gn_ref, ys_ref, y_ref, buf, sem, *, tc):
    def tile_copy(k, r, slot):
        src = ys_ref.at[pl.ds(pl.multiple_of(slot * SUBLANES, SUBLANES), SUBLANES), :]
        dst = buf.at[k, pl.ds(pl.multiple_of(r * SUBLANES, SUBLANES), SUBLANES), :]
        return pltpu.make_async_copy(src, dst, sem)

    def issue(r, carry):
        for k in range(TOP_K):
            tile_copy(k, r, meta_ref[e_ref[k, r]] + r_ref[k, r]).start()
        return carry

    lax.fori_loop(0, tc, issue, 0)

    def drain(r, carry):
        for k in range(TOP_K):
            tile_copy(k, r, 0).wait()
        return carry

    lax.fori_loop(0, tc, drain, 0)
    acc = x2_ref[...]
    for k in range(TOP_K):
        acc = acc + gt_ref[:, k:k + 1] * _load_token_tiles(buf.at[k], tc)
    y_ref[...] = _rms(acc, gn_ref[...])


def _combine(meta, e_all, r_all, x2_all, gt_all, gn, ys, *, row_off, rows, tc):
    _, d = x2_all.shape
    boff = row_off // tc
    grid_spec = pltpu.PrefetchScalarGridSpec(
        num_scalar_prefetch=1,
        grid=(rows // tc,),
        in_specs=[pl.BlockSpec((TOP_K, tc), lambda i, m: (0, i + boff), memory_space=pltpu.SMEM),
                  pl.BlockSpec((TOP_K, tc), lambda i, m: (0, i + boff), memory_space=pltpu.SMEM),
                  pl.BlockSpec((tc, d), lambda i, m: (i + boff, 0)),
                  pl.BlockSpec((tc, TOP_K), lambda i, m: (i + boff, 0)),
                  pl.BlockSpec((1, d), lambda i, m: (0, 0)),
                  pl.BlockSpec(memory_space=pl.ANY)],
        out_specs=pl.BlockSpec((tc, d), lambda i, m: (i, 0)),
        scratch_shapes=[pltpu.VMEM((TOP_K, tc * SUBLANES, LANES), F32), pltpu.SemaphoreType.DMA(())],
    )
    return pl.pallas_call(
        functools.partial(_combine_kernel, tc=tc),
        grid_spec=grid_spec,
        out_shape=jax.ShapeDtypeStruct((rows, d), F32),
        compiler_params=_cparams(("arbitrary",)),
        name="combine",
    )(meta, e_all, r_all, x2_all, gt_all, gn, ys)


def _pick_tile(n, pref):
    t = min(pref, n)
    while n % t:
        t //= 2
    return t


def kernel(x_prompt, x_sample, cache_k, cache_v, state_conv, state_lru, norm_mix, w_in, conv_w, conv_b,
           lru_wa, lru_ba, lru_wx, lru_bx, lru_lambda, rel_bias, w_out, norm_ffn, router_w, router_b,
           w1, b1, w2, b2, norm_out):
    depth = w_in.shape[0]
    assert depth == 1, "single-layer step"
    bp, sp, d = x_prompt.shape
    bs, ss, _ = x_sample.shape
    n_heads = d // HEAD_DIM
    band = LEFT_CHUNKS * CHUNK
    n_keep = min(band, sp)
    n_cache = cache_k.shape[2]
    ne = router_w.shape[-1]
    dff = w2.shape[2]
    l = 0
    assert PAST_LEN % CHUNK == 0 and n_cache == band and ss == CHUNK
    assert d == SUBLANES * LANES, "token-tile layout holds one token per (8,128) tile"

    w_in_bf = w_in[l].astype(BF16)
    w_out_bf = w_out[l].astype(BF16)
    g_mix = norm_mix[l].reshape(1, d)
    g_ffn = norm_ffn[l].reshape(1, d)
    g_out = norm_out.reshape(1, d)
    wg = jnp.concatenate([lru_wa[l], lru_wx[l]], axis=-1).astype(BF16)
    ba = lru_ba[l].reshape(1, d)
    bx = lru_bx[l].reshape(1, d)
    lam = lru_lambda[l].reshape(1, d)
    cw = conv_w[l]
    cb = conv_b[l].reshape(1, d)
    rwt = router_w[l].T
    rb = router_b[l].reshape(ne, 1)
    w1g = w1[l][:, :, 0::2].astype(BF16)
    w1l = w1[l][:, :, 1::2].astype(BF16)
    b1g = b1[l][:, 0::2].reshape(ne, 1, dff)
    b1l = b1[l][:, 1::2].reshape(ne, 1, dff)
    w2b = w2[l].astype(BF16)
    b2r = b2[l].reshape(ne, 1, d)

    tp = bp * sp
    tsm = bs * ss
    t_all = tp + tsm

    def mixer(x3d, conv_state, h0, attend, *, keep, starts_at_zero):
        b, s, _ = x3d.shape
        x2d = x3d.reshape(b * s, d)
        tm = _pick_tile(b * s, ROW_TILE)
        xr, q, k, v, gr, ga, kt, vt = _inproj(x2d, g_mix, w_in_bf, rows_per_batch=s, n_keep=keep, tm=tm)
        ts = _pick_tile(s, ROW_TILE)
        mr, cnew, hlast = _lru(xr, gr, conv_state, h0.reshape(b, 1, d), cw, cb, wg, ba, bx, lam,
                               batch=b, seq=s, ts=ts, starts_at_zero=starts_at_zero)
        mixed = attend(q, k, v, ga, mr)
        return x2d, mixed, kt, vt, cnew, hlast.reshape(b, d)

    gq = ATT_GROUP * CHUNK
    bias_p = _bias_table(rel_bias[l], gq, band + gq)
    bias_s = _bias_table(rel_bias[l], ss, n_cache + ss)

    attend_p = lambda q, k, v, ga, mr: _attn_prompt(q, k, v, bias_p, ga, mr, batch=bp, seq=sp)
    ck = cache_k[l].reshape(bs * n_cache, d)
    cv = cache_v[l].reshape(bs * n_cache, d)
    attend_s = lambda q, k, v, ga, mr: _attn_sample(q, k, v, ck, cv, bias_s, ga, mr, batch=bs, seq=ss)

    conv0 = jnp.zeros((bp, CONV_W - 1, d), F32)
    lru0 = jnp.zeros((bp, d), F32)
    xp2d, mixed_p, ktp, vtp, cnew_p, hlast_p = mixer(x_prompt, conv0, lru0, attend_p,
                                                     keep=n_keep, starts_at_zero=True)
    xs2d, mixed_s, kts, vts, cnew_s, hlast_s = mixer(x_sample, state_conv[l], state_lru[l], attend_s,
                                                     keep=ss, starts_at_zero=False)

    tmo = _pick_tile(tsm, ROW_TILE)
    x2_all, hn_all, lgt_all = _outproj(mixed_p, xp2d, mixed_s, xs2d, w_out_bf, g_ffn, rwt, rb, tm=tmo)

    tr = _pick_tile(t_all, ROUTE_TILE)
    e_all, r_all, gates, counts = _route(lgt_all, tr=tr)
    blk = MOE_BLOCK
    nblocks = -(-(t_all * TOP_K) // blk) + ne
    meta, be = _plan(counts, nblocks=nblocks, blk=blk)
    td = _pick_tile(t_all, 256)
    xs = _dispatch(meta, e_all, r_all, hn_all, n_slots=nblocks * blk, td=td, blk=blk, ne=ne)
    ys = _experts(be, meta, xs, w1g, w1l, b1g, b1l, w2b, b2r, blk=blk)
    gt_all = gates.T
    tc = _pick_tile(tsm, 128)
    y_p = _combine(meta, e_all, r_all, x2_all, gt_all, g_out, ys, row_off=0, rows=tp, tc=tc)
    y_s = _combine(meta, e_all, r_all, x2_all, gt_all, g_out, ys, row_off=tp, rows=tsm, tc=tc)

    return (y_p.reshape(bp, sp, d), y_s.reshape(bs, ss, d),
            ktp.reshape(1, bp, n_keep, n_heads, HEAD_DIM), vtp.reshape(1, bp, n_keep, n_heads, HEAD_DIM),
            cnew_p[None], hlast_p[None],
            kts.reshape(1, bs, ss, n_heads, HEAD_DIM), vts.reshape(1, bs, ss, n_heads, HEAD_DIM),
            cnew_s[None], hlast_s[None])
```

```python
import functools

import jax
import jax.numpy as jnp
from jax import lax
from jax.experimental import pallas as pl
from jax.experimental.pallas import tpu as pltpu

F32 = jnp.float32
BF16 = jnp.bfloat16
I32 = jnp.int32

CHUNK = 64
LEFT_CHUNKS = 8
HEAD_DIM = 128
MAX_REL = 256
NEG_INF = -1e30
LRU_BLOCKS = 8
CONV_W = 4
LRU_C = 8.0
TOP_K = 4
SWIGLU_ALPHA = 1.702
SWIGLU_LIMIT = 7.0
RMS_EPS = 1e-6
PAST_LEN = 1024

LANES = 128
SUBLANES = 8
MXU_COLS = 256
VMEM_LIMIT = 56 * 1024 * 1024
MOE_BLOCK = 256
ROUTE_TILE = 512
ROW_TILE = 512
ATT_GROUP = 4


def _cparams(sem):
    return pltpu.CompilerParams(dimension_semantics=sem, vmem_limit_bytes=VMEM_LIMIT)


def _sigmoid(x):
    return 1.0 / (1.0 + jnp.exp(-x))


def _rms(x, g):
    ms = jnp.mean(x * x, axis=-1, keepdims=True)
    return x * lax.rsqrt(ms + RMS_EPS) * g


def _inproj_kernel(x_ref, g_ref, w_ref, xr_ref, q_ref, k_ref, v_ref, gr_ref, ga_ref,
                   kt_ref, vt_ref, *, d):
    u = _rms(x_ref[...], g_ref[...]).astype(BF16)

    def piece(j):
        return jnp.dot(u, w_ref[:, j * d:(j + 1) * d], preferred_element_type=F32)

    xr_ref[...] = piece(0)
    q_ref[...] = piece(1).astype(BF16)
    kf = piece(2)
    k_ref[...] = kf.astype(BF16)
    kt_ref[...] = kf
    vf = piece(3)
    v_ref[...] = vf.astype(BF16)
    vt_ref[...] = vf
    gr_ref[...] = piece(4).astype(BF16)
    ga_ref[...] = piece(5).astype(BF16)


def _inproj(x2d, g, w_bf, *, rows_per_batch, n_keep, tm):
    n, d = x2d.shape
    nt = n // tm
    if n_keep == rows_per_batch:
        tail_rows = n
        tail_map = lambda i: (i, 0)
    else:
        ntb = rows_per_batch // tm
        nk = n_keep // tm
        tail_rows = (n // rows_per_batch) * n_keep
        tail_map = lambda i: ((i // ntb) * nk + jnp.maximum(i % ntb - (ntb - nk), 0), 0)
    row = pl.BlockSpec((tm, d), lambda i: (i, 0))
    outs = pl.pallas_call(
        functools.partial(_inproj_kernel, d=d),
        grid=(nt,),
        in_specs=[row,
                  pl.BlockSpec((1, d), lambda i: (0, 0)),
                  pl.BlockSpec((d, 6 * d), lambda i: (0, 0), pipeline_mode=pl.Buffered(1))],
        out_specs=[row, row, row, row, row, row,
                   pl.BlockSpec((tm, d), tail_map), pl.BlockSpec((tm, d), tail_map)],
        out_shape=[jax.ShapeDtypeStruct((n, d), F32),
                   jax.ShapeDtypeStruct((n, d), BF16),
                   jax.ShapeDtypeStruct((n, d), BF16),
                   jax.ShapeDtypeStruct((n, d), BF16),
                   jax.ShapeDtypeStruct((n, d), BF16),
                   jax.ShapeDtypeStruct((n, d), BF16),
                   jax.ShapeDtypeStruct((tail_rows, d), F32),
                   jax.ShapeDtypeStruct((tail_rows, d), F32)],
        compiler_params=_cparams(("arbitrary",)),
        name="inproj",
    )(x2d, g, w_bf)
    return outs


def _lru_kernel(xr_ref, gr_ref, cs_ref, h0_ref, cw_ref, cb_ref, wg_ref, ba_ref, bx_ref, lam_ref,
                mr_ref, cnew_ref, hlast_ref, xp_s, a_s, b_s, hs_s, h_s, *, ts, starts_at_zero):
    t = pl.program_id(1)
    head = SUBLANES
    hist = CONV_W - 1

    @pl.when(t == 0)
    def _():
        xp_s[0:head, :] = jnp.zeros((head, xp_s.shape[1]), F32)
        xp_s[head - hist:head, :] = cs_ref[0]
        h_s[...] = h0_ref[0]

    xp_s[head:head + ts, :] = xr_ref[...]
    cw = cw_ref[...]
    xc = cb_ref[...] + xp_s[head - hist:head - hist + ts, :] * cw[0:1, :]
    for j in range(1, CONV_W):
        xc = xc + xp_s[head - hist + j:head - hist + j + ts, :] * cw[j:j + 1, :]
    tail = xp_s[ts + head - hist:ts + head, :]
    xp_s[head - hist:head, :] = tail
    cnew_ref[0] = tail

    lam = lam_ref[...]
    z = -lam
    softplus = jnp.maximum(z, 0.0) + jnp.log1p(jnp.exp(-jnp.abs(z)))
    c = -LRU_C * softplus
    bw = xc.shape[1] // LRU_BLOCKS
    if starts_at_zero:
        first = (lax.broadcasted_iota(I32, (ts, bw), 0) + t * ts) == 0
    for g in range(LRU_BLOCKS):
        sl = slice(g * bw, (g + 1) * bw)
        xg = xc[:, sl]
        pre = jnp.dot(xg.astype(BF16), wg_ref[g], preferred_element_type=F32)
        r = _sigmoid(pre[:, :bw] + ba_ref[:, sl])
        ig = _sigmoid(pre[:, bw:] + bx_ref[:, sl])
        log_a = c[:, sl] * r
        a = jnp.exp(log_a)
        mult = jnp.sqrt(1.0 - a * a)
        if starts_at_zero:
            mult = jnp.where(first, 1.0, mult)
        a_s[:, sl] = a
        b_s[:, sl] = mult * (ig * xg)

    def scan_body(i, h):
        base = pl.multiple_of(i * SUBLANES, SUBLANES)
        a8 = a_s[pl.ds(base, SUBLANES), :]
        b8 = b_s[pl.ds(base, SUBLANES), :]
        rows = []
        for j in range(SUBLANES):
            h = a8[j:j + 1, :] * h + b8[j:j + 1, :]
            rows.append(h)
        hs_s[pl.ds(base, SUBLANES), :] = jnp.concatenate(rows, axis=0)
        return h

    h = lax.fori_loop(0, ts // SUBLANES, scan_body, h_s[...])
    h_s[...] = h
    hlast_ref[0] = h
    mr_ref[...] = (_sigmoid(gr_ref[...].astype(F32)) * hs_s[...]).astype(BF16)


def _lru(xr, gr, conv_state, h0, conv_w, conv_b, wg, ba, bx, lam, *, batch, seq, ts, starts_at_zero):
    n, d = xr.shape
    ntb = seq // ts
    row = pl.BlockSpec((ts, d), lambda b, t: (b * ntb + t, 0))
    vec = pl.BlockSpec((1, d), lambda b, t: (0, 0))
    return pl.pallas_call(
        functools.partial(_lru_kernel, ts=ts, starts_at_zero=starts_at_zero),
        grid=(batch, ntb),
        in_specs=[row, row,
                  pl.BlockSpec((1, CONV_W - 1, d), lambda b, t: (b, 0, 0)),
                  pl.BlockSpec((1, 1, d), lambda b, t: (b, 0, 0)),
                  pl.BlockSpec((CONV_W, d), lambda b, t: (0, 0)),
                  vec,
                  pl.BlockSpec(wg.shape, lambda b, t: (0, 0, 0)),
                  vec, vec, vec],
        out_specs=[row,
                   pl.BlockSpec((1, CONV_W - 1, d), lambda b, t: (b, 0, 0)),
                   pl.BlockSpec((1, 1, d), lambda b, t: (b, 0, 0))],
        out_shape=[jax.ShapeDtypeStruct((n, d), BF16),
                   jax.ShapeDtypeStruct((batch, CONV_W - 1, d), F32),
                   jax.ShapeDtypeStruct((batch, 1, d), F32)],
        scratch_shapes=[pltpu.VMEM((ts + SUBLANES, d), F32),
                        pltpu.VMEM((ts, d), F32),
                        pltpu.VMEM((ts, d), F32),
                        pltpu.VMEM((ts, d), F32),
                        pltpu.VMEM((1, d), F32)],
        compiler_params=_cparams(("arbitrary", "arbitrary")),
        name="lru",
    )(xr, gr, conv_state, h0, conv_w, conv_b, wg, ba, bx, lam)


def _bias_kernel(f_ref, o_ref, *, nq, nkeys, width):
    f = f_ref[0]
    x = jnp.broadcast_to(f, (nq, width))
    rolled = pltpu.roll(x, width - (nq - 1), 1, stride=1, stride_axis=0)
    t = rolled[:, :nkeys]
    qc = lax.broadcasted_iota(I32, (nq, nkeys), 0) // CHUNK
    kc = lax.broadcasted_iota(I32, (nq, nkeys), 1) // CHUNK
    ok = jnp.logical_and(kc >= qc, kc <= qc + LEFT_CHUNKS)
    o_ref[0] = jnp.where(ok, t, NEG_INF)


def _bias_table(rel_bias, nq, nkeys):
    nh = rel_bias.shape[0]
    band = LEFT_CHUNKS * CHUNK
    width = -(-(nkeys + nq) // LANES) * LANES
    left = band + (nq - 1) - MAX_REL
    right = max(width - left - (2 * MAX_REL + 1), 0)
    flipped = rel_bias[:, ::-1]
    f = jnp.pad(flipped, ((0, 0), (left, right)), mode="edge")[:, :width].reshape(nh, 1, width)
    return pl.pallas_call(
        functools.partial(_bias_kernel, nq=nq, nkeys=nkeys, width=width),
        grid=(nh,),
        in_specs=[pl.BlockSpec((1, 1, width), lambda h: (h, 0, 0))],
        out_specs=pl.BlockSpec((1, nq, nkeys), lambda h: (h, 0, 0)),
        out_shape=jax.ShapeDtypeStruct((nh, nq, nkeys), F32),
        compiler_params=_cparams(("arbitrary",)),
        name="bias",
    )(f)


def _attn_kernel(*refs, part_rows, n_heads, mask_parts):
    np_ = len(part_rows)
    q_ref = refs[0]
    k_refs = refs[1:1 + np_]
    v_refs = refs[1 + np_:1 + 2 * np_]
    bias_ref, ga_ref, mr_ref, o_ref = refs[1 + 2 * np_:]
    i = pl.program_id(1)
    scale = HEAD_DIM ** -0.5
    for h in range(n_heads):
        hs = slice(h * HEAD_DIM, (h + 1) * HEAD_DIM)
        qh = (q_ref[:, hs].astype(F32) * scale).astype(BF16)
        s_parts = []
        off = 0
        for p in range(np_):
            kp = k_refs[p][:, hs].astype(BF16)
            s = lax.dot_general(qh, kp, (((1,), (1,)), ((), ())), preferred_element_type=F32)
            s = s + bias_ref[h, :, off:off + part_rows[p]]
            if mask_parts and p < np_ - 1:
                s = jnp.where(i < (np_ - 1 - p), NEG_INF, s)
            s_parts.append(s)
            off += part_rows[p]
        m = s_parts[0].max(axis=-1, keepdims=True)
        for s in s_parts[1:]:
            m = jnp.maximum(m, s.max(axis=-1, keepdims=True))
        l = None
        o = None
        for p in range(np_):
            e = jnp.exp(s_parts[p] - m)
            lp = e.sum(axis=-1, keepdims=True)
            op = jnp.dot(e.astype(BF16), v_refs[p][:, hs].astype(BF16), preferred_element_type=F32)
            l = lp if l is None else l + lp
            o = op if o is None else o + op
        att = o / l
        mixed = mr_ref[:, hs].astype(F32) + _sigmoid(ga_ref[:, hs].astype(F32)) * att
        o_ref[:, hs] = mixed.astype(BF16)


def _attn_prompt(q, k, v, bias, ga, mr, *, batch, seq):
    n, d = q.shape
    gq = ATT_GROUP * CHUNK
    nparts = LEFT_CHUNKS // ATT_GROUP + 1
    ng = seq // gq
    cur = pl.BlockSpec((gq, d), lambda b, i: (b * ng + i, 0))

    def back(p):
        return pl.BlockSpec((gq, d), lambda b, i: (b * ng + jnp.maximum(i - p, 0), 0))

    kv_specs = [back(nparts - 1 - p) for p in range(nparts)]
    n_heads = d // HEAD_DIM
    return pl.pallas_call(
        functools.partial(_attn_kernel, part_rows=(gq,) * nparts, n_heads=n_heads, mask_parts=True),
        grid=(batch, ng),
        in_specs=[cur] + kv_specs + kv_specs
                 + [pl.BlockSpec(bias.shape, lambda b, i: (0, 0, 0), pipeline_mode=pl.Buffered(1)), cur, cur],
        out_specs=cur,
        out_shape=jax.ShapeDtypeStruct((n, d), BF16),
        compiler_params=_cparams(("arbitrary", "arbitrary")),
        name="attn_prompt",
    )(q, *([k] * nparts), *([v] * nparts), bias, ga, mr)


def _attn_sample(q, k, v, ck, cv, bias, ga, mr, *, batch, seq):
    n, d = q.shape
    nc = ck.shape[0] // batch
    cur = pl.BlockSpec((seq, d), lambda b, i: (b, 0))
    cache = pl.BlockSpec((nc, d), lambda b, i: (b, 0))
    n_heads = d // HEAD_DIM
    return pl.pallas_call(
        functools.partial(_attn_kernel, part_rows=(nc, seq), n_heads=n_heads, mask_parts=False),
        grid=(batch, 1),
        in_specs=[cur, cache, cur, cache, cur,
                  pl.BlockSpec(bias.shape, lambda b, i: (0, 0, 0), pipeline_mode=pl.Buffered(1)), cur, cur],
        out_specs=cur,
        out_shape=jax.ShapeDtypeStruct((n, d), BF16),
        compiler_params=_cparams(("arbitrary", "arbitrary")),
        name="attn_sample",
    )(q, ck, k, cv, v, bias, ga, mr)


def _store_token_tiles(ref, x):
    rows = x.shape[0]
    for c in range(x.shape[1] // LANES):
        ref[pl.ds(c, rows, stride=SUBLANES), :] = x[:, c * LANES:(c + 1) * LANES]


def _load_token_tiles(ref, rows):
    return jnp.concatenate([ref[pl.ds(c, rows, stride=SUBLANES), :] for c in range(SUBLANES)], axis=1)


def _split_bf16(x):
    hi = x.astype(BF16)
    lo = (x - hi.astype(F32)).astype(BF16)
    return hi, lo


def _outproj_kernel(mp_ref, xp_ref, ms_ref, xs_ref, w_ref, g_ref, rwt_ref, rb_ref,
                    x2_ref, hn_ref, lg_ref, *, ntp):
    def run(mixed_ref, x_ref):
        x2 = x_ref[...] + jnp.dot(mixed_ref[...], w_ref[...], preferred_element_type=F32)
        x2_ref[...] = x2
        hn = _rms(x2, g_ref[...])
        _store_token_tiles(hn_ref, hn)
        h_hi, h_lo = _split_bf16(hn)
        w_hi, w_lo = _split_bf16(rwt_ref[...])
        nt = (((1,), (1,)), ((), ()))
        lg = lax.dot_general(w_hi, h_hi, nt, preferred_element_type=F32)
        lg = lg + lax.dot_general(w_hi, h_lo, nt, preferred_element_type=F32)
        lg = lg + lax.dot_general(w_lo, h_hi, nt, preferred_element_type=F32)
        lg_ref[...] = lg + rb_ref[...]

    i = pl.program_id(0)

    @pl.when(i < ntp)
    def _():
        run(mp_ref, xp_ref)

    @pl.when(i >= ntp)
    def _():
        run(ms_ref, xs_ref)


def _outproj(mixed_p, xp2d, mixed_s, xs2d, w_bf, g, rwt, rb, *, tm):
    tp, d = xp2d.shape
    tsm = xs2d.shape[0]
    ne = rwt.shape[0]
    ntp = tp // tm
    nts = tsm // tm
    total_rows = tp + tsm
    prow = pl.BlockSpec((tm, d), lambda i: (jnp.minimum(i, ntp - 1), 0))
    srow = pl.BlockSpec((tm, d), lambda i: (jnp.maximum(i - ntp, 0), 0))
    full = lambda shape: pl.BlockSpec(shape, lambda i: (0,) * len(shape))
    return pl.pallas_call(
        functools.partial(_outproj_kernel, ntp=ntp),
        grid=(ntp + nts,),
        in_specs=[prow, prow, srow, srow, full((d, d)), full((1, d)), full((ne, d)), full((ne, 1))],
        out_specs=[pl.BlockSpec((tm, d), lambda i: (i, 0)),
                   pl.BlockSpec((tm * SUBLANES, LANES), lambda i: (i, 0)),
                   pl.BlockSpec((ne, tm), lambda i: (0, i))],
        out_shape=[jax.ShapeDtypeStruct((total_rows, d), F32),
                   jax.ShapeDtypeStruct((total_rows * SUBLANES, LANES), F32),
                   jax.ShapeDtypeStruct((ne, total_rows), F32)],
        compiler_params=_cparams(("arbitrary",)),
        name="outproj",
    )(mixed_p, xp2d, mixed_s, xs2d, w_bf, g, rwt, rb)


def _route_kernel(lg_ref, e_ref, r_ref, g_ref, cnt_ref, carry_s, *, tr):
    @pl.when(pl.program_id(0) == 0)
    def _():
        carry_s[...] = jnp.zeros(carry_s.shape, F32)

    work = lg_ref[...]
    ne = work.shape[0]
    eid = lax.broadcasted_iota(I32, (ne, tr), 0).astype(F32)
    sels, vals, idxs = [], [], []
    for _ in range(TOP_K):
        m = work.max(axis=0, keepdims=True)
        idx = jnp.where(work == m, eid, float(ne)).min(axis=0, keepdims=True)
        sel = eid == idx
        sels.append(sel)
        vals.append(m)
        idxs.append(idx)
        work = jnp.where(sel, -jnp.inf, work)
    ex = [jnp.exp(v - vals[0]) for v in vals]
    den = ex[0] + ex[1] + ex[2] + ex[3]
    chosen = jnp.zeros((ne, tr), F32)
    for sel in sels:
        chosen = chosen + sel.astype(F32)
    rr = lax.broadcasted_iota(I32, (tr, tr), 0)
    cc = lax.broadcasted_iota(I32, (tr, tr), 1)
    upper = jnp.where(rr < cc, 1.0, 0.0).astype(BF16)
    before = jnp.dot(chosen.astype(BF16), upper, preferred_element_type=F32)
    carry = carry_s[:, 0:1]
    rank_all = before + carry
    for k in range(TOP_K):
        e_ref[k:k + 1, :] = idxs[k].astype(I32)
        g_ref[k:k + 1, :] = ex[k] / den
        r_ref[k:k + 1, :] = jnp.where(sels[k], rank_all, 0.0).sum(axis=0, keepdims=True).astype(I32)
    new_carry = carry + chosen.sum(axis=1, keepdims=True)
    carry_s[...] = jnp.broadcast_to(new_carry, carry_s.shape)
    cnt_ref[...] = jnp.broadcast_to(new_carry, cnt_ref.shape).astype(I32)


def _route(lgt, *, tr):
    ne, t = lgt.shape
    blk = lambda rows: pl.BlockSpec((rows, tr), lambda i: (0, i))
    return pl.pallas_call(
        functools.partial(_route_kernel, tr=tr),
        grid=(t // tr,),
        in_specs=[blk(ne)],
        out_specs=[blk(TOP_K), blk(TOP_K), blk(TOP_K), pl.BlockSpec((ne, LANES), lambda i: (0, 0))],
        out_shape=[jax.ShapeDtypeStruct((TOP_K, t), I32),
                   jax.ShapeDtypeStruct((TOP_K, t), I32),
                   jax.ShapeDtypeStruct((TOP_K, t), F32),
                   jax.ShapeDtypeStruct((ne, LANES), I32)],
        scratch_shapes=[pltpu.VMEM((ne, LANES), F32)],
        compiler_params=_cparams(("arbitrary",)),
        name="route",
    )(lgt)


def _plan_kernel(cnt_ref, meta_ref, be_ref, *, ne, nblocks, blk):
    acc = jnp.int32(0)
    for e in range(ne):
        c = cnt_ref[e, 0]
        pc = ((c + (blk - 1)) // blk) * blk
        meta_ref[e] = acc
        b0 = acc // blk
        acc = acc + pc
        meta_ref[ne + e] = acc
        b1 = acc // blk

        def fill(b, carry, e=e):
            be_ref[b] = jnp.int32(e)
            return carry

        lax.fori_loop(b0, b1, fill, 0)
    used = acc // blk
    meta_ref[2 * ne] = used

    def fill_rest(b, carry):
        be_ref[b] = jnp.int32(ne - 1)
        return carry

    lax.fori_loop(used, nblocks, fill_rest, 0)
    for j in range(2 * ne + 1, meta_ref.shape[0]):
        meta_ref[j] = jnp.int32(0)


def _plan(counts, *, nblocks, blk):
    ne = counts.shape[0]
    smem = pl.BlockSpec(memory_space=pltpu.SMEM)
    return pl.pallas_call(
        functools.partial(_plan_kernel, ne=ne, nblocks=nblocks, blk=blk),
        in_specs=[smem],
        out_specs=[smem, smem],
        out_shape=[jax.ShapeDtypeStruct((LANES,), I32), jax.ShapeDtypeStruct((nblocks,), I32)],
        name="plan",
    )(counts)


def _dispatch_kernel(meta_ref, e_ref, r_ref, hn_ref, xs_ref, zero_s, sem, *, td, ne, blk, nblocks):
    i = pl.program_id(0)

    @pl.when(i == 0)
    def _():
        zero_s[...] = jnp.zeros(zero_s.shape, F32)

        def zcopy(e):
            row0 = pl.multiple_of(jnp.maximum(meta_ref[ne + e] - blk, 0) * SUBLANES, SUBLANES)
            return pltpu.make_async_copy(zero_s, xs_ref.at[pl.ds(row0, blk * SUBLANES), :], sem)

        for e in range(ne):
            zcopy(e).start()
        for e in range(ne):
            zcopy(e).wait()

        def ztail(b, carry):
            row0 = pl.multiple_of(b * (blk * SUBLANES), blk * SUBLANES)
            cp = pltpu.make_async_copy(zero_s, xs_ref.at[pl.ds(row0, blk * SUBLANES), :], sem)
            cp.start()
            cp.wait()
            return carry

        lax.fori_loop(meta_ref[2 * ne], nblocks, ztail, 0)

    def tile_copy(r, slot):
        src = hn_ref.at[pl.ds(pl.multiple_of(r * SUBLANES, SUBLANES), SUBLANES), :]
        dst = xs_ref.at[pl.ds(pl.multiple_of(slot * SUBLANES, SUBLANES), SUBLANES), :]
        return pltpu.make_async_copy(src, dst, sem)

    def issue(r, carry):
        for k in range(TOP_K):
            tile_copy(r, meta_ref[e_ref[k, r]] + r_ref[k, r]).start(priority=k % 2)
        return carry

    lax.fori_loop(0, td, issue, 0)

    def drain(r, carry):
        for k in range(TOP_K):
            tile_copy(r, 0).wait()
        return carry

    lax.fori_loop(0, td, drain, 0)


def _dispatch(meta, e_all, r_all, hn_all, *, n_slots, td, blk, ne):
    t = hn_all.shape[0] // SUBLANES
    grid_spec = pltpu.PrefetchScalarGridSpec(
        num_scalar_prefetch=1,
        grid=(t // td,),
        in_specs=[pl.BlockSpec((TOP_K, td), lambda i, m: (0, i), memory_space=pltpu.SMEM),
                  pl.BlockSpec((TOP_K, td), lambda i, m: (0, i), memory_space=pltpu.SMEM),
                  pl.BlockSpec((td * SUBLANES, LANES), lambda i, m: (i, 0))],
        out_specs=pl.BlockSpec(memory_space=pl.ANY),
        scratch_shapes=[pltpu.VMEM((blk * SUBLANES, LANES), F32), pltpu.SemaphoreType.DMA(())],
    )
    return pl.pallas_call(
        functools.partial(_dispatch_kernel, td=td, ne=ne, blk=blk, nblocks=n_slots // blk),
        grid_spec=grid_spec,
        out_shape=jax.ShapeDtypeStruct((n_slots * SUBLANES, LANES), F32),
        compiler_params=_cparams(("arbitrary",)),
        name="dispatch",
    )(meta, e_all, r_all, hn_all)


def _experts_kernel(be_ref, meta_ref, x_ref, w1_ref, b1g_ref, b1l_ref, w2_ref, b2_ref,
                    y_ref, w1_s, w2_s, *, ne, blk):
    b = pl.program_id(0)
    used = meta_ref[2 * ne]
    half = MXU_COLS // 2
    ngroups = w1_s.shape[1] // MXU_COLS
    fresh = jnp.logical_or(b == 0, be_ref[b] != be_ref[jnp.maximum(b - 1, 0)])

    @pl.when(jnp.logical_and(fresh, b < used))
    def _():
        src = lax.broadcasted_iota(I32, (MXU_COLS, MXU_COLS), 0)
        dst = lax.broadcasted_iota(I32, (MXU_COLS, MXU_COLS), 1)
        want = jnp.where(dst < half, 2 * dst, 2 * (dst - half) + 1)
        perm = jnp.where(src == want, 1.0, 0.0).astype(BF16)
        for t in range(ngroups):
            cols = slice(t * MXU_COLS, (t + 1) * MXU_COLS)
            w = w1_ref[0, :, cols].astype(BF16)
            w1_s[:, cols] = jnp.dot(w, perm, preferred_element_type=F32).astype(BF16)
        w2_s[...] = w2_ref[0].astype(BF16)

    @pl.when(b < used)
    def _():
        xb = _load_token_tiles(x_ref, blk).astype(BF16)
        acts = []
        for t in range(ngroups):
            h = jnp.dot(xb, w1_s[:, t * MXU_COLS:(t + 1) * MXU_COLS], preferred_element_type=F32)
            hg = h[:, :half] + b1g_ref[0, :, t * half:(t + 1) * half]
            hl = h[:, half:] + b1l_ref[0, :, t * half:(t + 1) * half]
            g = jnp.minimum(hg, SWIGLU_LIMIT)
            lin = jnp.clip(hl, -SWIGLU_LIMIT, SWIGLU_LIMIT)
            acts.append((g * _sigmoid(SWIGLU_ALPHA * g) * (lin + 1.0)).astype(BF16))
        act = jnp.concatenate(acts, axis=1)
        y = jnp.dot(act, w2_s[...], preferred_element_type=F32) + b2_ref[0]
        _store_token_tiles(y_ref, y)

    @pl.when(b >= used)
    def _():
        y_ref[...] = jnp.zeros(y_ref.shape, F32)


def _experts(be, meta, xs, w1, b1g, b1l, w2, b2, *, blk):
    ne, d, dff2 = w1.shape
    dff = dff2 // 2
    n_slots = xs.shape[0] // SUBLANES
    nblocks = n_slots // blk

    def xmap(b, be_r, meta_r):
        return (jnp.minimum(b, meta_r[2 * ne] - 1), 0)

    def wmap(b, be_r, meta_r):
        return (be_r[b], 0, 0)

    grid_spec = pltpu.PrefetchScalarGridSpec(
        num_scalar_prefetch=2,
        grid=(nblocks,),
        in_specs=[pl.BlockSpec((blk * SUBLANES, LANES), xmap),
                  pl.BlockSpec((1, d, dff2), wmap),
                  pl.BlockSpec((1, 1, dff), wmap),
                  pl.BlockSpec((1, 1, dff), wmap),
                  pl.BlockSpec((1, dff, d), wmap),
                  pl.BlockSpec((1, 1, d), wmap)],
        out_specs=pl.BlockSpec((blk * SUBLANES, LANES), lambda b, be_r, meta_r: (b, 0)),
        scratch_shapes=[pltpu.VMEM((d, dff2), BF16), pltpu.VMEM((dff, d), BF16)],
    )
    return pl.pallas_call(
        functools.partial(_experts_kernel, ne=ne, blk=blk),
        grid_spec=grid_spec,
        out_shape=jax.ShapeDtypeStruct((n_slots * SUBLANES, LANES), F32),
        compiler_params=_cparams(("arbitrary",)),
        name="experts",
    )(be, meta, xs, w1, b1g, b1l, w2, b2)


def _combine_kernel(meta_ref, e_ref, r_ref, x2_ref, gt_ref, gn_ref, ys_ref, y_ref, buf, sem, *, tc):
    def tile_copy(k, r, slot):
        src = ys_ref.at[pl.ds(pl.multiple_of(slot * SUBLANES, SUBLANES), SUBLANES), :]
        dst = buf.at[k, pl.ds(pl.multiple_of(r * SUBLANES, SUBLANES), SUBLANES), :]
        return pltpu.make_async_copy(src, dst, sem)

    def issue(r, carry):
        for k in range(TOP_K):
            tile_copy(k, r, meta_ref[e_ref[k, r]] + r_ref[k, r]).start(priority=k % 2)
        return carry

    lax.fori_loop(0, tc, issue, 0)

    def drain(r, carry):
        for k in range(TOP_K):
            tile_copy(k, r, 0).wait()
        return carry

    lax.fori_loop(0, tc, drain, 0)
    acc = x2_ref[...]
    for k in range(TOP_K):
        acc = acc + gt_ref[:, k:k + 1] * _load_token_tiles(buf.at[k], tc)
    y_ref[...] = _rms(acc, gn_ref[...])


def _combine(meta, e_all, r_all, x2_all, gt_all, gn, ys, *, row_off, rows, tc):
    _, d = x2_all.shape
    boff = row_off // tc
    grid_spec = pltpu.PrefetchScalarGridSpec(
        num_scalar_prefetch=1,
        grid=(rows // tc,),
        in_specs=[pl.BlockSpec((TOP_K, tc), lambda i, m: (0, i + boff), memory_space=pltpu.SMEM),
                  pl.BlockSpec((TOP_K, tc), lambda i, m: (0, i + boff), memory_space=pltpu.SMEM),
                  pl.BlockSpec((tc, d), lambda i, m: (i + boff, 0)),
                  pl.BlockSpec((tc, TOP_K), lambda i, m: (i + boff, 0)),
                  pl.BlockSpec((1, d), lambda i, m: (0, 0)),
                  pl.BlockSpec(memory_space=pl.ANY)],
        out_specs=pl.BlockSpec((tc, d), lambda i, m: (i, 0)),
        scratch_shapes=[pltpu.VMEM((TOP_K, tc * SUBLANES, LANES), F32), pltpu.SemaphoreType.DMA(())],
    )
    return pl.pallas_call(
        functools.partial(_combine_kernel, tc=tc),
        grid_spec=grid_spec,
        out_shape=jax.ShapeDtypeStruct((rows, d), F32),
        compiler_params=_cparams(("arbitrary",)),
        name="combine",
    )(meta, e_all, r_all, x2_all, gt_all, gn, ys)


def _pick_tile(n, pref):
    t = min(pref, n)
    while n % t:
        t //= 2
    return t


def kernel(x_prompt, x_sample, cache_k, cache_v, state_conv, state_lru, norm_mix, w_in, conv_w, conv_b,
           lru_wa, lru_ba, lru_wx, lru_bx, lru_lambda, rel_bias, w_out, norm_ffn, router_w, router_b,
           w1, b1, w2, b2, norm_out):
    depth = w_in.shape[0]
    assert depth == 1, "single-layer step"
    bp, sp, d = x_prompt.shape
    bs, ss, _ = x_sample.shape
    n_heads = d // HEAD_DIM
    band = LEFT_CHUNKS * CHUNK
    n_keep = min(band, sp)
    n_cache = cache_k.shape[2]
    ne = router_w.shape[-1]
    dff = w2.shape[2]
    l = 0
    assert PAST_LEN % CHUNK == 0 and n_cache == band and ss == CHUNK
    assert d == SUBLANES * LANES, "token-tile layout holds one token per (8,128) tile"

    w_in_bf = w_in[l].astype(BF16)
    w_out_bf = w_out[l].astype(BF16)
    g_mix = norm_mix[l].reshape(1, d)
    g_ffn = norm_ffn[l].reshape(1, d)
    g_out = norm_out.reshape(1, d)
    wg = jnp.concatenate([lru_wa[l], lru_wx[l]], axis=-1).astype(BF16)
    ba = lru_ba[l].reshape(1, d)
    bx = lru_bx[l].reshape(1, d)
    lam = lru_lambda[l].reshape(1, d)
    cw = conv_w[l]
    cb = conv_b[l].reshape(1, d)
    rwt = router_w[l].T
    rb = router_b[l].reshape(ne, 1)
    b1g = b1[l][:, 0::2].reshape(ne, 1, dff)
    b1l = b1[l][:, 1::2].reshape(ne, 1, dff)
    b2r = b2[l].reshape(ne, 1, d)

    tp = bp * sp
    tsm = bs * ss
    t_all = tp + tsm

    def mixer(x3d, conv_state, h0, attend, *, keep, starts_at_zero):
        b, s, _ = x3d.shape
        x2d = x3d.reshape(b * s, d)
        tm = _pick_tile(b * s, ROW_TILE)
        xr, q, k, v, gr, ga, kt, vt = _inproj(x2d, g_mix, w_in_bf, rows_per_batch=s, n_keep=keep, tm=tm)
        ts = _pick_tile(s, ROW_TILE)
        mr, cnew, hlast = _lru(xr, gr, conv_state, h0.reshape(b, 1, d), cw, cb, wg, ba, bx, lam,
                               batch=b, seq=s, ts=ts, starts_at_zero=starts_at_zero)
        mixed = attend(q, k, v, ga, mr)
        return x2d, mixed, kt, vt, cnew, hlast.reshape(b, d)

    gq = ATT_GROUP * CHUNK
    bias_p = _bias_table(rel_bias[l], gq, band + gq)
    bias_s = _bias_table(rel_bias[l], ss, n_cache + ss)

    attend_p = lambda q, k, v, ga, mr: _attn_prompt(q, k, v, bias_p, ga, mr, batch=bp, seq=sp)
    ck = cache_k[l].reshape(bs * n_cache, d)
    cv = cache_v[l].reshape(bs * n_cache, d)
    attend_s = lambda q, k, v, ga, mr: _attn_sample(q, k, v, ck, cv, bias_s, ga, mr, batch=bs, seq=ss)

    conv0 = jnp.zeros((bp, CONV_W - 1, d), F32)
    lru0 = jnp.zeros((bp, d), F32)
    xp2d, mixed_p, ktp, vtp, cnew_p, hlast_p = mixer(x_prompt, conv0, lru0, attend_p,
                                                     keep=n_keep, starts_at_zero=True)
    xs2d, mixed_s, kts, vts, cnew_s, hlast_s = mixer(x_sample, state_conv[l], state_lru[l], attend_s,
                                                     keep=ss, starts_at_zero=False)

    tmo = _pick_tile(tsm, ROW_TILE)
    x2_all, hn_all, lgt_all = _outproj(mixed_p, xp2d, mixed_s, xs2d, w_out_bf, g_ffn, rwt, rb, tm=tmo)

    tr = _pick_tile(t_all, ROUTE_TILE)
    e_all, r_all, gates, counts = _route(lgt_all, tr=tr)
    blk = MOE_BLOCK
    nblocks = -(-(t_all * TOP_K) // blk) + ne
    meta, be = _plan(counts, nblocks=nblocks, blk=blk)
    td = _pick_tile(t_all, 256)
    xs = _dispatch(meta, e_all, r_all, hn_all, n_slots=nblocks * blk, td=td, blk=blk, ne=ne)
    ys = _experts(be, meta, xs, w1[l], b1g, b1l, w2[l], b2r, blk=blk)
    gt_all = gates.T
    tc = _pick_tile(tsm, 128)
    y_p = _combine(meta, e_all, r_all, x2_all, gt_all, g_out, ys, row_off=0, rows=tp, tc=tc)
    y_s = _combine(meta, e_all, r_all, x2_all, gt_all, g_out, ys, row_off=tp, rows=tsm, tc=tc)

    return (y_p.reshape(bp, sp, d), y_s.reshape(bs, ss, d),
            ktp.reshape(1, bp, n_keep, n_heads, HEAD_DIM), vtp.reshape(1, bp, n_keep, n_heads, HEAD_DIM),
            cnew_p[None], hlast_p[None],
            kts.reshape(1, bs, ss, n_heads, HEAD_DIM), vts.reshape(1, bs, ss, n_heads, HEAD_DIM),
            cnew_s[None], hlast_s[None])
```

```python
import functools

import jax
import jax.numpy as jnp
from jax import lax
from jax.experimental import pallas as pl
from jax.experimental.pallas import tpu as pltpu

F32 = jnp.float32
BF16 = jnp.bfloat16
I32 = jnp.int32

CHUNK = 64
LEFT_CHUNKS = 8
HEAD_DIM = 128
MAX_REL = 256
NEG_INF = -1e30
LRU_BLOCKS = 8
CONV_W = 4
LRU_C = 8.0
TOP_K = 4
SWIGLU_ALPHA = 1.702
SWIGLU_LIMIT = 7.0
RMS_EPS = 1e-6
PAST_LEN = 1024

LANES = 128
SUBLANES = 8
MXU_COLS = 256
VMEM_LIMIT = 56 * 1024 * 1024
MOE_BLOCK = 512
ROUTE_TILE = 512
ROW_TILE = 512
ATT_GROUP = 4


def _cparams(sem):
    return pltpu.CompilerParams(dimension_semantics=sem, vmem_limit_bytes=VMEM_LIMIT)


def _sigmoid(x):
    return 0.5 * jnp.tanh(0.5 * x) + 0.5


def _rms(x, g):
    ms = jnp.mean(x * x, axis=-1, keepdims=True)
    return x * lax.rsqrt(ms + RMS_EPS) * g


def _inproj_kernel(x_ref, g_ref, w_ref, xr_ref, q_ref, k_ref, v_ref, gr_ref, ga_ref,
                   kt_ref, vt_ref, *, d):
    u = _rms(x_ref[...], g_ref[...]).astype(BF16)

    def piece(j):
        return jnp.dot(u, w_ref[:, j * d:(j + 1) * d], preferred_element_type=F32)

    xr_ref[...] = piece(0)
    q_ref[...] = piece(1).astype(BF16)
    kf = piece(2)
    k_ref[...] = kf.astype(BF16)
    kt_ref[...] = kf
    vf = piece(3)
    v_ref[...] = vf.astype(BF16)
    vt_ref[...] = vf
    gr_ref[...] = piece(4).astype(BF16)
    ga_ref[...] = piece(5).astype(BF16)


def _inproj_t_kernel(x_ref, g_ref, w_ref, wt_ref, xr_ref, qt_ref, k_ref, vtr_ref, gr_ref, ga_ref,
                     kt_ref, vt_ref, *, d, ntb, nk):
    u = _rms(x_ref[...], g_ref[...]).astype(BF16)

    def piece(j):
        return jnp.dot(u, w_ref[:, j * d:(j + 1) * d], preferred_element_type=F32)

    def piece_t(j):
        return lax.dot_general(wt_ref[j * d:(j + 1) * d, :], u, (((1,), (1,)), ((), ())),
                               preferred_element_type=F32)

    xr_ref[...] = piece(0)
    qt_ref[...] = (piece_t(0) * (HEAD_DIM ** -0.5)).astype(BF16)
    kf = piece(2)
    k_ref[...] = kf.astype(BF16)
    kt_ref[...] = kf
    vtr_ref[...] = piece_t(1).astype(BF16)
    gr_ref[...] = piece(4).astype(BF16)
    ga_ref[...] = piece(5).astype(BF16)

    @pl.when(pl.program_id(0) % ntb >= ntb - nk)
    def _():
        vt_ref[...] = piece(3)


def _inproj(x2d, g, w_bf, *, rows_per_batch, n_keep, tm, transposed_qv=False):
    n, d = x2d.shape
    nt = n // tm
    if n_keep == rows_per_batch:
        assert not transposed_qv
        tail_rows = n
        tail_map = lambda i: (i, 0)
    else:
        ntb = rows_per_batch // tm
        nk = n_keep // tm
        tail_rows = (n // rows_per_batch) * n_keep
        tail_map = lambda i: ((i // ntb) * nk + jnp.maximum(i % ntb - (ntb - nk), 0), 0)
    row = pl.BlockSpec((tm, d), lambda i: (i, 0))
    col = pl.BlockSpec((d, tm), lambda i: (0, i))
    in_specs = [row,
                pl.BlockSpec((1, d), lambda i: (0, 0)),
                pl.BlockSpec((d, 6 * d), lambda i: (0, 0), pipeline_mode=pl.Buffered(1))]
    args = [x2d, g, w_bf]
    if transposed_qv:
        body = functools.partial(_inproj_t_kernel, d=d, ntb=ntb, nk=nk)
        in_specs.append(pl.BlockSpec((2 * d, d), lambda i: (0, 0), pipeline_mode=pl.Buffered(1)))
        args.append(jnp.concatenate([w_bf[:, d:2 * d].T, w_bf[:, 3 * d:4 * d].T], axis=0))
        qv_spec, qv_shape = col, jax.ShapeDtypeStruct((d, n), BF16)
    else:
        body = functools.partial(_inproj_kernel, d=d)
        qv_spec, qv_shape = row, jax.ShapeDtypeStruct((n, d), BF16)
    outs = pl.pallas_call(
        body,
        grid=(nt,),
        in_specs=in_specs,
        out_specs=[row, qv_spec, row, qv_spec, row, row,
                   pl.BlockSpec((tm, d), tail_map), pl.BlockSpec((tm, d), tail_map)],
        out_shape=[jax.ShapeDtypeStruct((n, d), F32),
                   qv_shape,
                   jax.ShapeDtypeStruct((n, d), BF16),
                   qv_shape,
                   jax.ShapeDtypeStruct((n, d), BF16),
                   jax.ShapeDtypeStruct((n, d), BF16),
                   jax.ShapeDtypeStruct((tail_rows, d), F32),
                   jax.ShapeDtypeStruct((tail_rows, d), F32)],
        compiler_params=_cparams(("arbitrary",)),
        name="inproj",
    )(*args)
    return outs


def _lru_kernel(xr_ref, gr_ref, cs_ref, h0_ref, cw_ref, cb_ref, wg_ref, ba_ref, bx_ref, lam_ref,
                mr_ref, cnew_ref, hlast_ref, xp_s, a_s, b_s, hs_s, h_s, *, ts, starts_at_zero):
    t = pl.program_id(1)
    head = SUBLANES
    hist = CONV_W - 1

    @pl.when(t == 0)
    def _():
        xp_s[0:head, :] = jnp.zeros((head, xp_s.shape[1]), F32)
        xp_s[head - hist:head, :] = cs_ref[0]
        h_s[...] = h0_ref[0]

    xp_s[head:head + ts, :] = xr_ref[...]
    cw = cw_ref[...]
    xc = cb_ref[...] + xp_s[head - hist:head - hist + ts, :] * cw[0:1, :]
    for j in range(1, CONV_W):
        xc = xc + xp_s[head - hist + j:head - hist + j + ts, :] * cw[j:j + 1, :]
    tail = xp_s[ts + head - hist:ts + head, :]
    xp_s[head - hist:head, :] = tail
    cnew_ref[0] = tail

    lam = lam_ref[...]
    z = -lam
    softplus = jnp.maximum(z, 0.0) + jnp.log1p(jnp.exp(-jnp.abs(z)))
    c = -LRU_C * softplus
    bw = xc.shape[1] // LRU_BLOCKS
    if starts_at_zero:
        first = (lax.broadcasted_iota(I32, (ts, bw), 0) + t * ts) == 0
    for g in range(LRU_BLOCKS):
        sl = slice(g * bw, (g + 1) * bw)
        xg = xc[:, sl]
        pre = jnp.dot(xg.astype(BF16), wg_ref[g], preferred_element_type=F32)
        r = _sigmoid(pre[:, :bw] + ba_ref[:, sl])
        ig = _sigmoid(pre[:, bw:] + bx_ref[:, sl])
        log_a = c[:, sl] * r
        a = jnp.exp(log_a)
        mult = jnp.sqrt(1.0 - a * a)
        if starts_at_zero:
            mult = jnp.where(first, 1.0, mult)
        a_s[:, sl] = a
        b_s[:, sl] = mult * (ig * xg)

    def scan_body(i, h):
        base = pl.multiple_of(i * SUBLANES, SUBLANES)
        a8 = a_s[pl.ds(base, SUBLANES), :]
        b8 = b_s[pl.ds(base, SUBLANES), :]
        rows = []
        for j in range(SUBLANES):
            h = a8[j:j + 1, :] * h + b8[j:j + 1, :]
            rows.append(h)
        hs_s[pl.ds(base, SUBLANES), :] = jnp.concatenate(rows, axis=0)
        return h

    h = lax.fori_loop(0, ts // SUBLANES, scan_body, h_s[...])
    h_s[...] = h
    hlast_ref[0] = h
    mr_ref[...] = (_sigmoid(gr_ref[...].astype(F32)) * hs_s[...]).astype(BF16)


def _lru(xr, gr, conv_state, h0, conv_w, conv_b, wg, ba, bx, lam, *, batch, seq, ts, starts_at_zero):
    n, d = xr.shape
    ntb = seq // ts
    row = pl.BlockSpec((ts, d), lambda b, t: (b * ntb + t, 0))
    vec = pl.BlockSpec((1, d), lambda b, t: (0, 0))
    return pl.pallas_call(
        functools.partial(_lru_kernel, ts=ts, starts_at_zero=starts_at_zero),
        grid=(batch, ntb),
        in_specs=[row, row,
                  pl.BlockSpec((1, CONV_W - 1, d), lambda b, t: (b, 0, 0)),
                  pl.BlockSpec((1, 1, d), lambda b, t: (b, 0, 0)),
                  pl.BlockSpec((CONV_W, d), lambda b, t: (0, 0)),
                  vec,
                  pl.BlockSpec(wg.shape, lambda b, t: (0, 0, 0)),
                  vec, vec, vec],
        out_specs=[row,
                   pl.BlockSpec((1, CONV_W - 1, d), lambda b, t: (b, 0, 0)),
                   pl.BlockSpec((1, 1, d), lambda b, t: (b, 0, 0))],
        out_shape=[jax.ShapeDtypeStruct((n, d), BF16),
                   jax.ShapeDtypeStruct((batch, CONV_W - 1, d), F32),
                   jax.ShapeDtypeStruct((batch, 1, d), F32)],
        scratch_shapes=[pltpu.VMEM((ts + SUBLANES, d), F32),
                        pltpu.VMEM((ts, d), F32),
                        pltpu.VMEM((ts, d), F32),
                        pltpu.VMEM((ts, d), F32),
                        pltpu.VMEM((1, d), F32)],
        compiler_params=_cparams(("arbitrary", "arbitrary")),
        name="lru",
    )(xr, gr, conv_state, h0, conv_w, conv_b, wg, ba, bx, lam)


def _bias_kernel(f_ref, o_ref, *, nq, nkeys, width):
    f = f_ref[0]
    x = jnp.broadcast_to(f, (nq, width))
    rolled = pltpu.roll(x, width - (nq - 1), 1, stride=1, stride_axis=0)
    t = rolled[:, :nkeys]
    qc = lax.broadcasted_iota(I32, (nq, nkeys), 0) // CHUNK
    kc = lax.broadcasted_iota(I32, (nq, nkeys), 1) // CHUNK
    ok = jnp.logical_and(kc >= qc, kc <= qc + LEFT_CHUNKS)
    o_ref[0] = jnp.where(ok, t, NEG_INF)


def _bias_t_kernel(g_ref, o_ref, *, nq, nkeys, width):
    x = jnp.broadcast_to(g_ref[0], (nkeys, width))
    rolled = pltpu.roll(x, nq, 1, stride=1, stride_axis=0)
    t = rolled[:, :nq]
    kc = lax.broadcasted_iota(I32, (nkeys, nq), 0) // CHUNK
    qc = lax.broadcasted_iota(I32, (nkeys, nq), 1) // CHUNK
    ok = jnp.logical_and(kc >= qc, kc <= qc + LEFT_CHUNKS)
    o_ref[0] = jnp.where(ok, t, NEG_INF)


def _bias_table(rel_bias, nq, nkeys, transposed=False):
    nh = rel_bias.shape[0]
    band = LEFT_CHUNKS * CHUNK
    width = -(-(nkeys + nq) // LANES) * LANES
    left = band + (nq - 1) - MAX_REL
    right = max(width - left - (2 * MAX_REL + 1), 0)
    flipped = rel_bias[:, ::-1]
    f = jnp.pad(flipped, ((0, 0), (left, right)), mode="edge")[:, :width]
    if transposed:
        f = f[:, ::-1]
        body = functools.partial(_bias_t_kernel, nq=nq, nkeys=nkeys, width=width)
        out_block = (1, nkeys, nq)
    else:
        body = functools.partial(_bias_kernel, nq=nq, nkeys=nkeys, width=width)
        out_block = (1, nq, nkeys)
    return pl.pallas_call(
        body,
        grid=(nh,),
        in_specs=[pl.BlockSpec((1, 1, width), lambda h: (h, 0, 0))],
        out_specs=pl.BlockSpec(out_block, lambda h: (h, 0, 0)),
        out_shape=jax.ShapeDtypeStruct((nh,) + out_block[1:], F32),
        compiler_params=_cparams(("arbitrary",)),
        name="bias",
    )(f.reshape(nh, 1, width))


def _attn_kernel(*refs, part_rows, n_heads, mask_parts):
    np_ = len(part_rows)
    q_ref = refs[0]
    k_refs = refs[1:1 + np_]
    v_refs = refs[1 + np_:1 + 2 * np_]
    bias_ref, ga_ref, mr_ref, o_ref = refs[1 + 2 * np_:]
    i = pl.program_id(1)
    scale = HEAD_DIM ** -0.5
    for h in range(n_heads):
        hs = slice(h * HEAD_DIM, (h + 1) * HEAD_DIM)
        qh = (q_ref[:, hs].astype(F32) * scale).astype(BF16)
        s_parts = []
        off = 0
        for p in range(np_):
            kp = k_refs[p][:, hs].astype(BF16)
            s = lax.dot_general(qh, kp, (((1,), (1,)), ((), ())), preferred_element_type=F32)
            s = s + bias_ref[h, :, off:off + part_rows[p]]
            if mask_parts and p < np_ - 1:
                s = jnp.where(i < (np_ - 1 - p), NEG_INF, s)
            s_parts.append(s)
            off += part_rows[p]
        m = s_parts[0].max(axis=-1, keepdims=True)
        for s in s_parts[1:]:
            m = jnp.maximum(m, s.max(axis=-1, keepdims=True))
        l = None
        o = None
        for p in range(np_):
            e = jnp.exp(s_parts[p] - m)
            lp = e.sum(axis=-1, keepdims=True)
            op = jnp.dot(e.astype(BF16), v_refs[p][:, hs].astype(BF16), preferred_element_type=F32)
            l = lp if l is None else l + lp
            o = op if o is None else o + op
        att = o / l
        mixed = mr_ref[:, hs].astype(F32) + _sigmoid(ga_ref[:, hs].astype(F32)) * att
        o_ref[:, hs] = mixed.astype(BF16)


def _attn_t_kernel(*refs, nparts, gq, n_heads):
    qt_ref = refs[0]
    k_refs = refs[1:1 + nparts]
    vt_refs = refs[1 + nparts:1 + 2 * nparts]
    bias_ref, ga_ref, mr_ref, o_ref = refs[1 + 2 * nparts:]
    i = pl.program_id(1)
    for h in range(n_heads):
        hs = slice(h * HEAD_DIM, (h + 1) * HEAD_DIM)
        qt = qt_ref[hs, :]
        s_parts = []
        for p in range(nparts):
            s = jnp.dot(k_refs[p][:, hs], qt, preferred_element_type=F32)
            s = s + bias_ref[h, p * gq:(p + 1) * gq, :]
            if p < nparts - 1:
                s = jnp.where(i < (nparts - 1 - p), NEG_INF, s)
            s_parts.append(s)
        m = s_parts[0]
        for s in s_parts[1:]:
            m = jnp.maximum(m, s)
        m = m.max(axis=0, keepdims=True)
        l = None
        ot = None
        for p in range(nparts):
            e = jnp.exp(s_parts[p] - m)
            lp = e.sum(axis=0, keepdims=True)
            op = jnp.dot(vt_refs[p][hs, :], e.astype(BF16), preferred_element_type=F32)
            l = lp if l is None else l + lp
            ot = op if ot is None else ot + op
        att = (ot / l).T
        mixed = mr_ref[:, hs].astype(F32) + _sigmoid(ga_ref[:, hs].astype(F32)) * att
        o_ref[:, hs] = mixed.astype(BF16)


def _attn_prompt(qt, k, vt, bias_t, ga, mr, *, batch, seq):
    n, d = k.shape
    gq = ATT_GROUP * CHUNK
    nparts = LEFT_CHUNKS // ATT_GROUP + 1
    ng = seq // gq
    cur = pl.BlockSpec((gq, d), lambda b, i: (b * ng + i, 0))
    cur_t = pl.BlockSpec((d, gq), lambda b, i: (0, b * ng + i))

    def back(p):
        return pl.BlockSpec((gq, d), lambda b, i: (b * ng + jnp.maximum(i - p, 0), 0))

    def back_t(p):
        return pl.BlockSpec((d, gq), lambda b, i: (0, b * ng + jnp.maximum(i - p, 0)))

    k_specs = [back(nparts - 1 - p) for p in range(nparts)]
    vt_specs = [back_t(nparts - 1 - p) for p in range(nparts)]
    n_heads = d // HEAD_DIM
    return pl.pallas_call(
        functools.partial(_attn_t_kernel, nparts=nparts, gq=gq, n_heads=n_heads),
        grid=(batch, ng),
        in_specs=[cur_t] + k_specs + vt_specs
                 + [pl.BlockSpec(bias_t.shape, lambda b, i: (0, 0, 0), pipeline_mode=pl.Buffered(1)), cur, cur],
        out_specs=cur,
        out_shape=jax.ShapeDtypeStruct((n, d), BF16),
        compiler_params=_cparams(("arbitrary", "arbitrary")),
        name="attn_prompt",
    )(qt, *([k] * nparts), *([vt] * nparts), bias_t, ga, mr)


def _attn_sample(q, k, v, ck, cv, bias, ga, mr, *, batch, seq):
    n, d = q.shape
    nc = ck.shape[0] // batch
    cur = pl.BlockSpec((seq, d), lambda b, i: (b, 0))
    cache = pl.BlockSpec((nc, d), lambda b, i: (b, 0))
    n_heads = d // HEAD_DIM
    return pl.pallas_call(
        functools.partial(_attn_kernel, part_rows=(nc, seq), n_heads=n_heads, mask_parts=False),
        grid=(batch, 1),
        in_specs=[cur, cache, cur, cache, cur,
                  pl.BlockSpec(bias.shape, lambda b, i: (0, 0, 0), pipeline_mode=pl.Buffered(1)), cur, cur],
        out_specs=cur,
        out_shape=jax.ShapeDtypeStruct((n, d), BF16),
        compiler_params=_cparams(("arbitrary", "arbitrary")),
        name="attn_sample",
    )(q, ck, k, cv, v, bias, ga, mr)


def _store_token_tiles(ref, x):
    rows = x.shape[0]
    for c in range(x.shape[1] // LANES):
        ref[pl.ds(c, rows, stride=SUBLANES), :] = x[:, c * LANES:(c + 1) * LANES]


def _load_token_tiles(ref, rows):
    return jnp.concatenate([ref[pl.ds(c, rows, stride=SUBLANES), :] for c in range(SUBLANES)], axis=1)


def _split_bf16(x):
    hi = x.astype(BF16)
    lo = (x - hi.astype(F32)).astype(BF16)
    return hi, lo


def _outproj_kernel(mp_ref, xp_ref, ms_ref, xs_ref, w_ref, g_ref, rwt_ref, rb_ref,
                    x2_ref, hn_ref, lg_ref, *, ntp):
    def run(mixed_ref, x_ref):
        x2 = x_ref[...] + jnp.dot(mixed_ref[...], w_ref[...], preferred_element_type=F32)
        x2_ref[...] = x2
        hn = _rms(x2, g_ref[...])
        _store_token_tiles(hn_ref, hn)
        h_hi, h_lo = _split_bf16(hn)
        w_hi, w_lo = _split_bf16(rwt_ref[...])
        nt = (((1,), (1,)), ((), ()))
        lg = lax.dot_general(w_hi, h_hi, nt, preferred_element_type=F32)
        lg = lg + lax.dot_general(w_hi, h_lo, nt, preferred_element_type=F32)
        lg = lg + lax.dot_general(w_lo, h_hi, nt, preferred_element_type=F32)
        lg_ref[...] = lg + rb_ref[...]

    i = pl.program_id(0)

    @pl.when(i < ntp)
    def _():
        run(mp_ref, xp_ref)

    @pl.when(i >= ntp)
    def _():
        run(ms_ref, xs_ref)


def _outproj(mixed_p, xp2d, mixed_s, xs2d, w_bf, g, rwt, rb, *, tm):
    tp, d = xp2d.shape
    tsm = xs2d.shape[0]
    ne = rwt.shape[0]
    ntp = tp // tm
    nts = tsm // tm
    total_rows = tp + tsm
    prow = pl.BlockSpec((tm, d), lambda i: (jnp.minimum(i, ntp - 1), 0))
    srow = pl.BlockSpec((tm, d), lambda i: (jnp.maximum(i - ntp, 0), 0))
    full = lambda shape: pl.BlockSpec(shape, lambda i: (0,) * len(shape))
    return pl.pallas_call(
        functools.partial(_outproj_kernel, ntp=ntp),
        grid=(ntp + nts,),
        in_specs=[prow, prow, srow, srow, full((d, d)), full((1, d)), full((ne, d)), full((ne, 1))],
        out_specs=[pl.BlockSpec((tm, d), lambda i: (i, 0)),
                   pl.BlockSpec((tm * SUBLANES, LANES), lambda i: (i, 0)),
                   pl.BlockSpec((ne, tm), lambda i: (0, i))],
        out_shape=[jax.ShapeDtypeStruct((total_rows, d), F32),
                   jax.ShapeDtypeStruct((total_rows * SUBLANES, LANES), F32),
                   jax.ShapeDtypeStruct((ne, total_rows), F32)],
        compiler_params=_cparams(("arbitrary",)),
        name="outproj",
    )(mixed_p, xp2d, mixed_s, xs2d, w_bf, g, rwt, rb)


def _route_kernel(lg_ref, e_ref, r_ref, g_ref, cnt_ref, carry_s, *, tr):
    @pl.when(pl.program_id(0) == 0)
    def _():
        carry_s[...] = jnp.zeros(carry_s.shape, F32)

    work = lg_ref[...]
    ne = work.shape[0]
    eid = lax.broadcasted_iota(I32, (ne, tr), 0).astype(F32)
    sels, vals, idxs = [], [], []
    for _ in range(TOP_K):
        m = work.max(axis=0, keepdims=True)
        idx = jnp.where(work == m, eid, float(ne)).min(axis=0, keepdims=True)
        sel = eid == idx
        sels.append(sel)
        vals.append(m)
        idxs.append(idx)
        work = jnp.where(sel, -jnp.inf, work)
    ex = [jnp.exp(v - vals[0]) for v in vals]
    den = ex[0] + ex[1] + ex[2] + ex[3]
    chosen = jnp.zeros((ne, tr), F32)
    for sel in sels:
        chosen = chosen + sel.astype(F32)
    rr = lax.broadcasted_iota(I32, (tr, tr), 0)
    cc = lax.broadcasted_iota(I32, (tr, tr), 1)
    upper = jnp.where(rr < cc, 1.0, 0.0).astype(BF16)
    before = jnp.dot(chosen.astype(BF16), upper, preferred_element_type=F32)
    carry = carry_s[:, 0:1]
    rank_all = before + carry
    for k in range(TOP_K):
        e_ref[k:k + 1, :] = idxs[k].astype(I32)
        g_ref[k:k + 1, :] = ex[k] / den
        r_ref[k:k + 1, :] = jnp.where(sels[k], rank_all, 0.0).sum(axis=0, keepdims=True).astype(I32)
    new_carry = carry + chosen.sum(axis=1, keepdims=True)
    carry_s[...] = jnp.broadcast_to(new_carry, carry_s.shape)
    cnt_ref[...] = jnp.broadcast_to(new_carry, cnt_ref.shape).astype(I32)


def _route(lgt, *, tr):
    ne, t = lgt.shape
    blk = lambda rows: pl.BlockSpec((rows, tr), lambda i: (0, i))
    return pl.pallas_call(
        functools.partial(_route_kernel, tr=tr),
        grid=(t // tr,),
        in_specs=[blk(ne)],
        out_specs=[blk(TOP_K), blk(TOP_K), blk(TOP_K), pl.BlockSpec((ne, LANES), lambda i: (0, 0))],
        out_shape=[jax.ShapeDtypeStruct((TOP_K, t), I32),
                   jax.ShapeDtypeStruct((TOP_K, t), I32),
                   jax.ShapeDtypeStruct((TOP_K, t), F32),
                   jax.ShapeDtypeStruct((ne, LANES), I32)],
        scratch_shapes=[pltpu.VMEM((ne, LANES), F32)],
        compiler_params=_cparams(("arbitrary",)),
        name="route",
    )(lgt)


def _plan_kernel(cnt_ref, meta_ref, be_ref, *, ne, nblocks, blk):
    acc = jnp.int32(0)
    for e in range(ne):
        c = cnt_ref[e, 0]
        pc = ((c + (blk - 1)) // blk) * blk
        meta_ref[e] = acc
        b0 = acc // blk
        acc = acc + pc
        meta_ref[ne + e] = acc
        b1 = acc // blk

        def fill(b, carry, e=e):
            be_ref[b] = jnp.int32(e)
            return carry

        lax.fori_loop(b0, b1, fill, 0)
    used = acc // blk
    meta_ref[2 * ne] = used

    def fill_rest(b, carry):
        be_ref[b] = jnp.int32(ne - 1)
        return carry

    lax.fori_loop(used, nblocks, fill_rest, 0)
    for j in range(2 * ne + 1, meta_ref.shape[0]):
        meta_ref[j] = jnp.int32(0)


def _plan(counts, *, nblocks, blk):
    ne = counts.shape[0]
    smem = pl.BlockSpec(memory_space=pltpu.SMEM)
    return pl.pallas_call(
        functools.partial(_plan_kernel, ne=ne, nblocks=nblocks, blk=blk),
        in_specs=[smem],
        out_specs=[smem, smem],
        out_shape=[jax.ShapeDtypeStruct((LANES,), I32), jax.ShapeDtypeStruct((nblocks,), I32)],
        name="plan",
    )(counts)


def _slots_kernel(e_ref, r_ref, cnt_ref, d_ref, *, blk):
    ne = cnt_ref.shape[0]
    tr = e_ref.shape[1]
    padded = jnp.bitwise_and(cnt_ref[:, 0:1] + (blk - 1), -blk).astype(F32)
    eid1 = lax.broadcasted_iota(I32, (ne, 1), 0)
    start = jnp.zeros((ne, 1), F32)
    for e in range(ne - 1):
        start = start + jnp.where(eid1 > e, padded[e:e + 1, :], 0.0)
    eid = lax.broadcasted_iota(I32, (ne, tr), 0)
    for k in range(TOP_K):
        hit = eid == e_ref[k:k + 1, :]
        base = jnp.where(hit, start, 0.0).sum(axis=0, keepdims=True)
        d_ref[k:k + 1, :] = base.astype(I32) + r_ref[k:k + 1, :]


def _slots(e_all, r_all, counts, *, blk, tr):
    t = e_all.shape[1]
    blk_spec = pl.BlockSpec((TOP_K, tr), lambda i: (0, i))
    return pl.pallas_call(
        functools.partial(_slots_kernel, blk=blk),
        grid=(t // tr,),
        in_specs=[blk_spec, blk_spec, pl.BlockSpec(counts.shape, lambda i: (0, 0))],
        out_specs=blk_spec,
        out_shape=jax.ShapeDtypeStruct((TOP_K, t), I32),
        compiler_params=_cparams(("arbitrary",)),
        name="slots",
    )(e_all, r_all, counts)


DISPATCH_RING = 3


def _dispatch_kernel(meta_ref, d_ref, hn_ref, xs_ref, zero_s, ring, zsem, in_sem, out_sem,
                     *, td, ne, blk, nblocks, nsteps):
    i = pl.program_id(0)
    rows = td * SUBLANES
    slot = i % DISPATCH_RING
    nxt = (i + 1) % DISPATCH_RING

    def load(step, s):
        src = hn_ref.at[pl.ds(pl.multiple_of(step * rows, rows), rows), :]
        return pltpu.make_async_copy(src, ring.at[s], in_sem.at[s])

    def wait_scatters(s):
        for _ in range(TOP_K):
            pltpu.make_async_copy(ring.at[s], xs_ref.at[pl.ds(0, rows), :], out_sem.at[s]).wait()

    @pl.when(i == 0)
    def _():
        zero_s[...] = jnp.zeros(zero_s.shape, F32)

        def zcopy(e):
            row0 = pl.multiple_of(jnp.maximum(meta_ref[ne + e] - blk, 0) * SUBLANES, SUBLANES)
            return pltpu.make_async_copy(zero_s, xs_ref.at[pl.ds(row0, blk * SUBLANES), :], zsem)

        for e in range(ne):
            zcopy(e).start()
        for e in range(ne):
            zcopy(e).wait()

        def ztail(b, carry):
            row0 = pl.multiple_of(b * (blk * SUBLANES), blk * SUBLANES)
            cp = pltpu.make_async_copy(zero_s, xs_ref.at[pl.ds(row0, blk * SUBLANES), :], zsem)
            cp.start()
            cp.wait()
            return carry

        lax.fori_loop(meta_ref[2 * ne], nblocks, ztail, 0)
        load(0, 0).start()

    @pl.when(i >= DISPATCH_RING - 1)
    def _():
        wait_scatters(nxt)

    @pl.when(i + 1 < nsteps)
    def _():
        load(i + 1, nxt).start()

    load(i, slot).wait()

    def issue(r, carry):
        src = ring.at[slot, pl.ds(pl.multiple_of(r * SUBLANES, SUBLANES), SUBLANES), :]
        for k in range(TOP_K):
            dst = xs_ref.at[pl.ds(pl.multiple_of(d_ref[k, r] * SUBLANES, SUBLANES), SUBLANES), :]
            pltpu.make_async_copy(src, dst, out_sem.at[slot]).start(priority=k % 2)
        return carry

    lax.fori_loop(0, td, issue, 0, unroll=2)

    @pl.when(i == nsteps - 1)
    def _():
        wait_scatters(slot)

        @pl.when(i >= 1)
        def _():
            wait_scatters((i + DISPATCH_RING - 1) % DISPATCH_RING)


def _dispatch(meta, dest, hn_all, *, n_slots, td, blk, ne):
    t = hn_all.shape[0] // SUBLANES
    nsteps = t // td
    assert nsteps >= DISPATCH_RING - 1
    grid_spec = pltpu.PrefetchScalarGridSpec(
        num_scalar_prefetch=1,
        grid=(nsteps,),
        in_specs=[pl.BlockSpec((TOP_K, td), lambda i, m: (0, i), memory_space=pltpu.SMEM),
                  pl.BlockSpec(memory_space=pl.ANY)],
        out_specs=pl.BlockSpec(memory_space=pl.ANY),
        scratch_shapes=[pltpu.VMEM((blk * SUBLANES, LANES), F32),
                        pltpu.VMEM((DISPATCH_RING, td * SUBLANES, LANES), F32),
                        pltpu.SemaphoreType.DMA(()),
                        pltpu.SemaphoreType.DMA((DISPATCH_RING,)),
                        pltpu.SemaphoreType.DMA((DISPATCH_RING,))],
    )
    return pl.pallas_call(
        functools.partial(_dispatch_kernel, td=td, ne=ne, blk=blk, nblocks=n_slots // blk, nsteps=nsteps),
        grid_spec=grid_spec,
        out_shape=jax.ShapeDtypeStruct((n_slots * SUBLANES, LANES), F32),
        compiler_params=_cparams(("arbitrary",)),
        name="dispatch",
    )(meta, dest, hn_all)


def _experts_kernel(be_ref, meta_ref, x_ref, w1_ref, b1g_ref, b1l_ref, w2_ref, b2_ref,
                    y_ref, w1_s, w2_s, *, ne, blk):
    b = pl.program_id(0)
    used = meta_ref[2 * ne]
    half = MXU_COLS // 2
    ngroups = w1_s.shape[1] // MXU_COLS
    fresh = jnp.logical_or(b == 0, be_ref[b] != be_ref[jnp.maximum(b - 1, 0)])

    @pl.when(jnp.logical_and(fresh, b < used))
    def _():
        src = lax.broadcasted_iota(I32, (MXU_COLS, MXU_COLS), 0)
        dst = lax.broadcasted_iota(I32, (MXU_COLS, MXU_COLS), 1)
        want = jnp.where(dst < half, 2 * dst, 2 * (dst - half) + 1)
        perm = jnp.where(src == want, 1.0, 0.0).astype(BF16)
        for t in range(ngroups):
            cols = slice(t * MXU_COLS, (t + 1) * MXU_COLS)
            w = w1_ref[0, :, cols].astype(BF16)
            w1_s[:, cols] = jnp.dot(w, perm, preferred_element_type=F32).astype(BF16)
        w2_s[...] = w2_ref[0].astype(BF16)

    @pl.when(b < used)
    def _():
        xb = _load_token_tiles(x_ref, blk).astype(BF16)
        acts = []
        for t in range(ngroups):
            h = jnp.dot(xb, w1_s[:, t * MXU_COLS:(t + 1) * MXU_COLS], preferred_element_type=F32)
            hg = h[:, :half] + b1g_ref[0, :, t * half:(t + 1) * half]
            hl = h[:, half:] + b1l_ref[0, :, t * half:(t + 1) * half]
            g = jnp.minimum(hg, SWIGLU_LIMIT)
            lin = jnp.clip(hl, -SWIGLU_LIMIT, SWIGLU_LIMIT)
            acts.append((g * _sigmoid(SWIGLU_ALPHA * g) * (lin + 1.0)).astype(BF16))
        act = jnp.concatenate(acts, axis=1)
        y = jnp.dot(act, w2_s[...], preferred_element_type=F32) + b2_ref[0]
        _store_token_tiles(y_ref, y)

    @pl.when(b >= used)
    def _():
        y_ref[...] = jnp.zeros(y_ref.shape, F32)


def _experts(be, meta, xs, w1, b1g, b1l, w2, b2, *, blk):
    ne, d, dff2 = w1.shape
    dff = dff2 // 2
    n_slots = xs.shape[0] // SUBLANES
    nblocks = n_slots // blk

    def xmap(b, be_r, meta_r):
        return (jnp.minimum(b, meta_r[2 * ne] - 1), 0)

    def wmap(b, be_r, meta_r):
        return (be_r[b], 0, 0)

    grid_spec = pltpu.PrefetchScalarGridSpec(
        num_scalar_prefetch=2,
        grid=(nblocks,),
        in_specs=[pl.BlockSpec((blk * SUBLANES, LANES), xmap),
                  pl.BlockSpec((1, d, dff2), wmap),
                  pl.BlockSpec((1, 1, dff), wmap),
                  pl.BlockSpec((1, 1, dff), wmap),
                  pl.BlockSpec((1, dff, d), wmap),
                  pl.BlockSpec((1, 1, d), wmap)],
        out_specs=pl.BlockSpec((blk * SUBLANES, LANES), lambda b, be_r, meta_r: (b, 0)),
        scratch_shapes=[pltpu.VMEM((d, dff2), BF16), pltpu.VMEM((dff, d), BF16)],
    )
    return pl.pallas_call(
        functools.partial(_experts_kernel, ne=ne, blk=blk),
        grid_spec=grid_spec,
        out_shape=jax.ShapeDtypeStruct((n_slots * SUBLANES, LANES), F32),
        compiler_params=_cparams(("arbitrary",)),
        name="experts",
    )(be, meta, xs, w1, b1g, b1l, w2, b2)


def _combine_kernel(d_ref, dn_ref, x2_ref, gt_ref, gn_ref, ys_ref, y_ref, buf, sem, *, tc, nsteps):
    i = pl.program_id(0)
    slot = i % 2
    rows = tc * SUBLANES

    def gather(idx_ref, s):
        def body(r, carry):
            for k in range(TOP_K):
                src = ys_ref.at[pl.ds(pl.multiple_of(idx_ref[k, r] * SUBLANES, SUBLANES), SUBLANES), :]
                dst = buf.at[s, k, pl.ds(pl.multiple_of(r * SUBLANES, SUBLANES), SUBLANES), :]
                pltpu.make_async_copy(src, dst, sem.at[s]).start(priority=k % 2)
            return carry

        lax.fori_loop(0, tc, body, 0, unroll=2)

    @pl.when(i == 0)
    def _():
        gather(d_ref, 0)

    @pl.when(i + 1 < nsteps)
    def _():
        gather(dn_ref, 1 - slot)

    for k in range(TOP_K):
        pltpu.make_async_copy(ys_ref.at[pl.ds(0, rows), :], buf.at[slot, k], sem.at[slot]).wait()
    acc = x2_ref[...]
    for k in range(TOP_K):
        acc = acc + gt_ref[:, k:k + 1] * _load_token_tiles(buf.at[slot, k], tc)
    y_ref[...] = _rms(acc, gn_ref[...])


def _combine(dest, x2_all, gt_all, gn, ys, *, row_off, rows, tc):
    _, d = x2_all.shape
    boff = row_off // tc
    nsteps = rows // tc
    last = boff + nsteps - 1
    smem_blk = lambda fn: pl.BlockSpec((TOP_K, tc), fn, memory_space=pltpu.SMEM)
    return pl.pallas_call(
        functools.partial(_combine_kernel, tc=tc, nsteps=nsteps),
        grid=(nsteps,),
        in_specs=[smem_blk(lambda i: (0, i + boff)),
                  smem_blk(lambda i: (0, jnp.minimum(i + boff + 1, last))),
                  pl.BlockSpec((tc, d), lambda i: (i + boff, 0)),
                  pl.BlockSpec((tc, TOP_K), lambda i: (i + boff, 0)),
                  pl.BlockSpec((1, d), lambda i: (0, 0)),
                  pl.BlockSpec(memory_space=pl.ANY)],
        out_specs=pl.BlockSpec((tc, d), lambda i: (i, 0)),
        out_shape=jax.ShapeDtypeStruct((rows, d), F32),
        scratch_shapes=[pltpu.VMEM((2, TOP_K, tc * SUBLANES, LANES), F32), pltpu.SemaphoreType.DMA((2,))],
        compiler_params=_cparams(("arbitrary",)),
        name="combine",
    )(dest, dest, x2_all, gt_all, gn, ys)


def _pick_tile(n, pref):
    t = min(pref, n)
    while n % t:
        t //= 2
    return t


def kernel(x_prompt, x_sample, cache_k, cache_v, state_conv, state_lru, norm_mix, w_in, conv_w, conv_b,
           lru_wa, lru_ba, lru_wx, lru_bx, lru_lambda, rel_bias, w_out, norm_ffn, router_w, router_b,
           w1, b1, w2, b2, norm_out):
    depth = w_in.shape[0]
    assert depth == 1, "single-layer step"
    bp, sp, d = x_prompt.shape
    bs, ss, _ = x_sample.shape
    n_heads = d // HEAD_DIM
    band = LEFT_CHUNKS * CHUNK
    n_keep = min(band, sp)
    n_cache = cache_k.shape[2]
    ne = router_w.shape[-1]
    dff = w2.shape[2]
    l = 0
    assert PAST_LEN % CHUNK == 0 and n_cache == band and ss == CHUNK
    assert d == SUBLANES * LANES, "token-tile layout holds one token per (8,128) tile"

    w_in_bf = w_in[l].astype(BF16)
    w_out_bf = w_out[l].astype(BF16)
    g_mix = norm_mix[l].reshape(1, d)
    g_ffn = norm_ffn[l].reshape(1, d)
    g_out = norm_out.reshape(1, d)
    wg = jnp.concatenate([lru_wa[l], lru_wx[l]], axis=-1).astype(BF16)
    ba = lru_ba[l].reshape(1, d)
    bx = lru_bx[l].reshape(1, d)
    lam = lru_lambda[l].reshape(1, d)
    cw = conv_w[l]
    cb = conv_b[l].reshape(1, d)
    rwt = router_w[l].T
    rb = router_b[l].reshape(ne, 1)
    b1g = b1[l][:, 0::2].reshape(ne, 1, dff)
    b1l = b1[l][:, 1::2].reshape(ne, 1, dff)
    b2r = b2[l].reshape(ne, 1, d)

    tp = bp * sp
    tsm = bs * ss
    t_all = tp + tsm

    def mixer(x3d, conv_state, h0, attend, *, keep, starts_at_zero, transposed_qv):
        b, s, _ = x3d.shape
        x2d = x3d.reshape(b * s, d)
        tm = _pick_tile(b * s, ROW_TILE)
        xr, q, k, v, gr, ga, kt, vt = _inproj(x2d, g_mix, w_in_bf, rows_per_batch=s, n_keep=keep, tm=tm,
                                              transposed_qv=transposed_qv)
        ts = _pick_tile(s, ROW_TILE)
        mr, cnew, hlast = _lru(xr, gr, conv_state, h0.reshape(b, 1, d), cw, cb, wg, ba, bx, lam,
                               batch=b, seq=s, ts=ts, starts_at_zero=starts_at_zero)
        mixed = attend(q, k, v, ga, mr)
        return x2d, mixed, kt, vt, cnew, hlast.reshape(b, d)

    gq = ATT_GROUP * CHUNK
    bias_p = _bias_table(rel_bias[l], gq, band + gq, transposed=True)
    bias_s = _bias_table(rel_bias[l], ss, n_cache + ss)

    attend_p = lambda q, k, v, ga, mr: _attn_prompt(q, k, v, bias_p, ga, mr, batch=bp, seq=sp)
    ck = cache_k[l].reshape(bs * n_cache, d)
    cv = cache_v[l].reshape(bs * n_cache, d)
    attend_s = lambda q, k, v, ga, mr: _attn_sample(q, k, v, ck, cv, bias_s, ga, mr, batch=bs, seq=ss)

    conv0 = jnp.zeros((bp, CONV_W - 1, d), F32)
    lru0 = jnp.zeros((bp, d), F32)
    xp2d, mixed_p, ktp, vtp, cnew_p, hlast_p = mixer(x_prompt, conv0, lru0, attend_p,
                                                     keep=n_keep, starts_at_zero=True, transposed_qv=True)
    xs2d, mixed_s, kts, vts, cnew_s, hlast_s = mixer(x_sample, state_conv[l], state_lru[l], attend_s,
                                                     keep=ss, starts_at_zero=False, transposed_qv=False)

    tmo = _pick_tile(tsm, ROW_TILE)
    x2_all, hn_all, lgt_all = _outproj(mixed_p, xp2d, mixed_s, xs2d, w_out_bf, g_ffn, rwt, rb, tm=tmo)

    tr = _pick_tile(t_all, ROUTE_TILE)
    e_all, r_all, gates, counts = _route(lgt_all, tr=tr)
    blk = MOE_BLOCK
    nblocks = -(-(t_all * TOP_K) // blk) + ne
    meta, be = _plan(counts, nblocks=nblocks, blk=blk)
    dest = _slots(e_all, r_all, counts, blk=blk, tr=tr)
    td = _pick_tile(t_all, 256)
    xs = _dispatch(meta, dest, hn_all, n_slots=nblocks * blk, td=td, blk=blk, ne=ne)
    ys = _experts(be, meta, xs, w1[l], b1g, b1l, w2[l], b2r, blk=blk)
    gt_all = gates.T
    tc = _pick_tile(tsm, 128)
    y_p = _combine(dest, x2_all, gt_all, g_out, ys, row_off=0, rows=tp, tc=tc)
    y_s = _combine(dest, x2_all, gt_all, g_out, ys, row_off=tp, rows=tsm, tc=tc)

    return (y_p.reshape(bp, sp, d), y_s.reshape(bs, ss, d),
            ktp.reshape(1, bp, n_keep, n_heads, HEAD_DIM), vtp.reshape(1, bp, n_keep, n_heads, HEAD_DIM),
            cnew_p[None], hlast_p[None],
            kts.reshape(1, bs, ss, n_heads, HEAD_DIM), vts.reshape(1, bs, ss, n_heads, HEAD_DIM),
            cnew_s[None], hlast_s[None])
```

```python
import functools

import jax
import jax.numpy as jnp
from jax import lax
from jax.experimental import pallas as pl
from jax.experimental.pallas import tpu as pltpu

F32 = jnp.float32
BF16 = jnp.bfloat16
I32 = jnp.int32

CHUNK = 64
LEFT_CHUNKS = 8
HEAD_DIM = 128
MAX_REL = 256
NEG_INF = -1e30
LRU_BLOCKS = 8
CONV_W = 4
LRU_C = 8.0
TOP_K = 4
SWIGLU_ALPHA = 1.702
SWIGLU_LIMIT = 7.0
RMS_EPS = 1e-6
PAST_LEN = 1024

LANES = 128
SUBLANES = 8
MXU_COLS = 256
VMEM_LIMIT = 56 * 1024 * 1024
MOE_BLOCK = 512
ROUTE_TILE = 1024
ROW_TILE = 512
ATT_GROUP = 4


def _cparams(sem):
    return pltpu.CompilerParams(dimension_semantics=sem, vmem_limit_bytes=VMEM_LIMIT)


def _sigmoid(x):
    return 0.5 * jnp.tanh(0.5 * x) + 0.5


def _rms(x, g):
    ms = jnp.mean(x * x, axis=-1, keepdims=True)
    return x * lax.rsqrt(ms + RMS_EPS) * g


def _inproj_kernel(x_ref, g_ref, w_ref, xr_ref, q_ref, k_ref, v_ref, gr_ref, ga_ref,
                   kt_ref, vt_ref, *, d):
    u = _rms(x_ref[...], g_ref[...]).astype(BF16)

    def piece(j):
        return jnp.dot(u, w_ref[:, j * d:(j + 1) * d], preferred_element_type=F32)

    xr_ref[...] = piece(0)
    q_ref[...] = (piece(1) * (HEAD_DIM ** -0.5)).astype(BF16)
    kf = piece(2)
    k_ref[...] = kf.astype(BF16)
    kt_ref[...] = kf
    vf = piece(3)
    v_ref[...] = vf.astype(BF16)
    vt_ref[...] = vf
    gr_ref[...] = piece(4).astype(BF16)
    ga_ref[...] = piece(5).astype(BF16)


def _inproj(x2d, g, w_bf, *, tm):
    n, d = x2d.shape
    row = pl.BlockSpec((tm, d), lambda i: (i, 0))
    bf = jax.ShapeDtypeStruct((n, d), BF16)
    f32 = jax.ShapeDtypeStruct((n, d), F32)
    return pl.pallas_call(
        functools.partial(_inproj_kernel, d=d),
        grid=(n // tm,),
        in_specs=[row,
                  pl.BlockSpec((1, d), lambda i: (0, 0)),
                  pl.BlockSpec((d, 6 * d), lambda i: (0, 0), pipeline_mode=pl.Buffered(1))],
        out_specs=[row] * 8,
        out_shape=[f32, bf, bf, bf, bf, bf, f32, f32],
        compiler_params=_cparams(("arbitrary",)),
        name="inproj",
    )(x2d, g, w_bf)


def _lru_kernel(xr_ref, gr_ref, cs_ref, h0_ref, cw_ref, cb_ref, wg_ref, ba_ref, bx_ref, lam_ref,
                mr_ref, cnew_ref, hlast_ref, xp_s, a_s, b_s, hs_s, h_s, *, ts, starts_at_zero):
    _lru_init(pl.program_id(1), cs_ref, h0_ref, xp_s, h_s)
    _lru_steps(xr_ref[...], gr_ref[...].astype(F32), pl.program_id(1),
               cw_ref, cb_ref, wg_ref, ba_ref, bx_ref, lam_ref,
               mr_ref, cnew_ref, hlast_ref, xp_s, a_s, b_s, hs_s, h_s,
               ts=ts, starts_at_zero=starts_at_zero)


def _lru_init(t, cs_ref, h0_ref, xp_s, h_s):
    head = SUBLANES
    hist = CONV_W - 1

    @pl.when(t == 0)
    def _():
        xp_s[0:head, :] = jnp.zeros((head, xp_s.shape[1]), F32)
        xp_s[head - hist:head, :] = cs_ref[0]
        h_s[...] = jnp.broadcast_to(h0_ref[0], h_s.shape)


def _lru_steps(xr, gr, t, cw_ref, cb_ref, wg_ref, ba_ref, bx_ref, lam_ref,
               mr_ref, cnew_ref, hlast_ref, xp_s, a_s, b_s, hs_s, h_s, *, ts, starts_at_zero):
    head = SUBLANES
    hist = CONV_W - 1
    xp_s[head:head + ts, :] = xr
    cw = cw_ref[...]
    xc = cb_ref[...] + xp_s[head - hist:head - hist + ts, :] * cw[0:1, :]
    for j in range(1, CONV_W):
        xc = xc + xp_s[head - hist + j:head - hist + j + ts, :] * cw[j:j + 1, :]
    tail = xp_s[ts + head - hist:ts + head, :]
    xp_s[head - hist:head, :] = tail
    cnew_ref[0] = tail

    lam = lam_ref[...]
    z = -lam
    softplus = jnp.maximum(z, 0.0) + jnp.log1p(jnp.exp(-jnp.abs(z)))
    c = -LRU_C * softplus
    bw = xc.shape[1] // LRU_BLOCKS
    if starts_at_zero:
        first = (lax.broadcasted_iota(I32, (ts, bw), 0) + t * ts) == 0
    for g in range(LRU_BLOCKS):
        sl = slice(g * bw, (g + 1) * bw)
        xg = xc[:, sl]
        pre = jnp.dot(xg.astype(BF16), wg_ref[g], preferred_element_type=F32)
        r = _sigmoid(pre[:, :bw] + ba_ref[:, sl])
        ig = _sigmoid(pre[:, bw:] + bx_ref[:, sl])
        log_a = c[:, sl] * r
        a = jnp.exp(log_a)
        mult = jnp.sqrt(1.0 - a * a)
        if starts_at_zero:
            mult = jnp.where(first, 1.0, mult)
        a_s[:, sl] = a
        b_s[:, sl] = mult * (ig * xg)

    sub = lax.broadcasted_iota(I32, (SUBLANES, xc.shape[1]), 0)

    def scan_body(i, h):
        base = pl.multiple_of(i * SUBLANES, SUBLANES)
        a = a_s[pl.ds(base, SUBLANES), :]
        b = b_s[pl.ds(base, SUBLANES), :]
        for shift in (1, 2, 4):
            a_prev = jnp.where(sub < shift, 1.0, pltpu.roll(a, shift, 0))
            b_prev = jnp.where(sub < shift, 0.0, pltpu.roll(b, shift, 0))
            b = a * b_prev + b
            a = a * a_prev
        rows = a * h + b
        hs_s[pl.ds(base, SUBLANES), :] = rows
        return jnp.broadcast_to(rows[SUBLANES - 1:SUBLANES, :], rows.shape)

    h = lax.fori_loop(0, ts // SUBLANES, scan_body, h_s[...])
    h_s[...] = h
    hlast_ref[0] = h[0:1, :]
    mr_ref[...] = (_sigmoid(gr) * hs_s[...]).astype(BF16)


def _lru(xr, gr, conv_state, h0, conv_w, conv_b, wg, ba, bx, lam, *, batch, seq, ts, starts_at_zero):
    n, d = xr.shape
    ntb = seq // ts
    row = pl.BlockSpec((ts, d), lambda b, t: (b * ntb + t, 0))
    vec = pl.BlockSpec((1, d), lambda b, t: (0, 0))
    return pl.pallas_call(
        functools.partial(_lru_kernel, ts=ts, starts_at_zero=starts_at_zero),
        grid=(batch, ntb),
        in_specs=[row, row,
                  pl.BlockSpec((1, CONV_W - 1, d), lambda b, t: (b, 0, 0)),
                  pl.BlockSpec((1, 1, d), lambda b, t: (b, 0, 0)),
                  pl.BlockSpec((CONV_W, d), lambda b, t: (0, 0)),
                  vec,
                  pl.BlockSpec(wg.shape, lambda b, t: (0, 0, 0)),
                  vec, vec, vec],
        out_specs=[row,
                   pl.BlockSpec((1, CONV_W - 1, d), lambda b, t: (b, 0, 0)),
                   pl.BlockSpec((1, 1, d), lambda b, t: (b, 0, 0))],
        out_shape=[jax.ShapeDtypeStruct((n, d), BF16),
                   jax.ShapeDtypeStruct((batch, CONV_W - 1, d), F32),
                   jax.ShapeDtypeStruct((batch, 1, d), F32)],
        scratch_shapes=_lru_scratch(ts, d),
        compiler_params=_cparams(("arbitrary", "arbitrary")),
        name="lru",
    )(xr, gr, conv_state, h0, conv_w, conv_b, wg, ba, bx, lam)


def _lru_scratch(ts, d):
    return [pltpu.VMEM((ts + SUBLANES, d), F32),
            pltpu.VMEM((ts, d), F32),
            pltpu.VMEM((ts, d), F32),
            pltpu.VMEM((ts, d), F32),
            pltpu.VMEM((SUBLANES, d), F32)]


def _mixer_kernel(x_ref, g_ref, w_ref, cs_ref, h0_ref, cw_ref, cb_ref, wg_ref, ba_ref, bx_ref,
                  lam_ref, q_ref, k_ref, v_ref, ga_ref, mr_ref, kt_ref, vt_ref, cnew_ref, hlast_ref,
                  xp_s, a_s, b_s, hs_s, h_s, *, d, ntb):
    t = pl.program_id(0) % ntb
    _lru_init(t, cs_ref, h0_ref, xp_s, h_s)
    u = _rms(x_ref[...], g_ref[...]).astype(BF16)

    def piece(j):
        return jnp.dot(u, w_ref[:, j * d:(j + 1) * d], preferred_element_type=F32)

    q_ref[...] = (piece(1) * (HEAD_DIM ** -0.5)).astype(BF16)
    kf = piece(2)
    k_ref[...] = kf.astype(BF16)
    kt_ref[...] = kf
    vf = piece(3)
    v_ref[...] = vf.astype(BF16)
    vt_ref[...] = vf
    ga_ref[...] = piece(5).astype(BF16)
    _lru_steps(piece(0), piece(4), t,
               cw_ref, cb_ref, wg_ref, ba_ref, bx_ref, lam_ref,
               mr_ref, cnew_ref, hlast_ref, xp_s, a_s, b_s, hs_s, h_s,
               ts=x_ref.shape[0], starts_at_zero=True)


def _mixer_prompt(x2d, g, w_bf, conv_state, h0, conv_w, conv_b, wg, ba, bx, lam, *, batch, seq, n_keep, tm):
    n, d = x2d.shape
    ntb = seq // tm
    nk = n_keep // tm
    tail_map = lambda i: ((i // ntb) * nk + jnp.maximum(i % ntb - (ntb - nk), 0), 0)
    row = pl.BlockSpec((tm, d), lambda i: (i, 0))
    vec = pl.BlockSpec((1, d), lambda i: (0, 0))
    per_batch = lambda rows: pl.BlockSpec((1, rows, d), lambda i: (i // ntb, 0, 0))
    bf = jax.ShapeDtypeStruct((n, d), BF16)
    tail = jax.ShapeDtypeStruct((batch * n_keep, d), F32)
    return pl.pallas_call(
        functools.partial(_mixer_kernel, d=d, ntb=ntb),
        grid=(n // tm,),
        in_specs=[row, vec,
                  pl.BlockSpec((d, 6 * d), lambda i: (0, 0), pipeline_mode=pl.Buffered(1)),
                  per_batch(CONV_W - 1), per_batch(1),
                  pl.BlockSpec((CONV_W, d), lambda i: (0, 0)), vec,
                  pl.BlockSpec(wg.shape, lambda i: (0, 0, 0)), vec, vec, vec],
        out_specs=[row, row, row, row, row,
                   pl.BlockSpec((tm, d), tail_map), pl.BlockSpec((tm, d), tail_map),
                   per_batch(CONV_W - 1), per_batch(1)],
        out_shape=[bf, bf, bf, bf, bf, tail, tail,
                   jax.ShapeDtypeStruct((batch, CONV_W - 1, d), F32),
                   jax.ShapeDtypeStruct((batch, 1, d), F32)],
        scratch_shapes=_lru_scratch(tm, d),
        compiler_params=_cparams(("arbitrary",)),
        name="mixer",
    )(x2d, g, w_bf, conv_state, h0, conv_w, conv_b, wg, ba, bx, lam)


def _bias_kernel(f_ref, o_ref, *, nq, nkeys, width):
    f = f_ref[0]
    x = jnp.broadcast_to(f, (nq, width))
    rolled = pltpu.roll(x, width - (nq - 1), 1, stride=1, stride_axis=0)
    t = rolled[:, :nkeys]
    qc = lax.broadcasted_iota(I32, (nq, nkeys), 0) // CHUNK
    kc = lax.broadcasted_iota(I32, (nq, nkeys), 1) // CHUNK
    ok = jnp.logical_and(kc >= qc, kc <= qc + LEFT_CHUNKS)
    o_ref[0] = jnp.where(ok, t, NEG_INF)


def _bias_table(rel_bias, nq, nkeys):
    nh = rel_bias.shape[0]
    band = LEFT_CHUNKS * CHUNK
    width = -(-(nkeys + nq) // LANES) * LANES
    left = band + (nq - 1) - MAX_REL
    right = max(width - left - (2 * MAX_REL + 1), 0)
    flipped = rel_bias[:, ::-1]
    f = jnp.pad(flipped, ((0, 0), (left, right)), mode="edge")[:, :width].reshape(nh, 1, width)
    return pl.pallas_call(
        functools.partial(_bias_kernel, nq=nq, nkeys=nkeys, width=width),
        grid=(nh,),
        in_specs=[pl.BlockSpec((1, 1, width), lambda h: (h, 0, 0))],
        out_specs=pl.BlockSpec((1, nq, nkeys), lambda h: (h, 0, 0)),
        out_shape=jax.ShapeDtypeStruct((nh, nq, nkeys), F32),
        compiler_params=_cparams(("arbitrary",)),
        name="bias",
    )(f)


def _attn_kernel(*refs, part_rows, head_major, n_heads, mask_parts):
    np_ = len(part_rows)
    q_ref = refs[0]
    k_refs = refs[1:1 + np_]
    v_refs = refs[1 + np_:1 + 2 * np_]
    bias_ref, ga_ref, mr_ref, o_ref = refs[1 + 2 * np_:]
    i = pl.program_id(1)

    def head(ref, p, h):
        if head_major[p]:
            return ref[pl.ds(h, part_rows[p], stride=n_heads), :].astype(BF16)
        return ref[:, h * HEAD_DIM:(h + 1) * HEAD_DIM].astype(BF16)

    for h in range(n_heads):
        hs = slice(h * HEAD_DIM, (h + 1) * HEAD_DIM)
        qh = q_ref[:, hs]
        s_parts = []
        off = 0
        for p in range(np_):
            kp = head(k_refs[p], p, h)
            s = lax.dot_general(qh, kp, (((1,), (1,)), ((), ())), preferred_element_type=F32)
            s = s + bias_ref[h, :, off:off + part_rows[p]]
            if mask_parts and p < np_ - 1:
                s = jnp.where(i < (np_ - 1 - p), NEG_INF, s)
            s_parts.append(s)
            off += part_rows[p]
        m = s_parts[0].max(axis=-1, keepdims=True)
        for s in s_parts[1:]:
            m = jnp.maximum(m, s.max(axis=-1, keepdims=True))
        l = None
        o = None
        for p in range(np_):
            e = jnp.exp(s_parts[p] - m)
            lp = e.sum(axis=-1, keepdims=True)
            op = jnp.dot(e.astype(BF16), head(v_refs[p], p, h), preferred_element_type=F32)
            l = lp if l is None else l + lp
            o = op if o is None else o + op
        att = o / l
        mixed = mr_ref[:, hs].astype(F32) + _sigmoid(ga_ref[:, hs].astype(F32)) * att
        o_ref[:, hs] = mixed.astype(BF16)


def _attn_prompt(q, k, v, bias, ga, mr, *, batch, seq):
    n, d = q.shape
    gq = ATT_GROUP * CHUNK
    nparts = LEFT_CHUNKS // ATT_GROUP + 1
    ng = seq // gq
    cur = pl.BlockSpec((gq, d), lambda b, i: (b * ng + i, 0))

    def back(p):
        return pl.BlockSpec((gq, d), lambda b, i: (b * ng + jnp.maximum(i - p, 0), 0))

    kv_specs = [back(nparts - 1 - p) for p in range(nparts)]
    n_heads = d // HEAD_DIM
    return pl.pallas_call(
        functools.partial(_attn_kernel, part_rows=(gq,) * nparts, head_major=(False,) * nparts,
                          n_heads=n_heads, mask_parts=True),
        grid=(batch, ng),
        in_specs=[cur] + kv_specs + kv_specs
                 + [pl.BlockSpec(bias.shape, lambda b, i: (0, 0, 0), pipeline_mode=pl.Buffered(1)), cur, cur],
        out_specs=cur,
        out_shape=jax.ShapeDtypeStruct((n, d), BF16),
        compiler_params=_cparams(("arbitrary", "arbitrary")),
        name="attn_prompt",
    )(q, *([k] * nparts), *([v] * nparts), bias, ga, mr)


def _attn_sample(q, k, v, ck, cv, bias, ga, mr, *, batch, seq):
    n, d = q.shape
    n_heads = d // HEAD_DIM
    nc = ck.shape[0] // (batch * n_heads)
    cur = pl.BlockSpec((seq, d), lambda b, i: (b, 0))
    cache = pl.BlockSpec((nc * n_heads, HEAD_DIM), lambda b, i: (b, 0))
    return pl.pallas_call(
        functools.partial(_attn_kernel, part_rows=(nc, seq), head_major=(True, False),
                          n_heads=n_heads, mask_parts=False),
        grid=(batch, 1),
        in_specs=[cur, cache, cur, cache, cur,
                  pl.BlockSpec(bias.shape, lambda b, i: (0, 0, 0), pipeline_mode=pl.Buffered(1)), cur, cur],
        out_specs=cur,
        out_shape=jax.ShapeDtypeStruct((n, d), BF16),
        compiler_params=_cparams(("arbitrary", "arbitrary")),
        name="attn_sample",
    )(q, ck, k, cv, v, bias, ga, mr)


def _store_token_tiles(ref, x):
    rows = x.shape[0]
    for c in range(x.shape[1] // LANES):
        ref[pl.ds(c, rows, stride=SUBLANES), :] = x[:, c * LANES:(c + 1) * LANES]


def _load_token_tiles(ref, rows):
    return jnp.concatenate([ref[pl.ds(c, rows, stride=SUBLANES), :] for c in range(SUBLANES)], axis=1)


def _split_bf16(x):
    hi = x.astype(BF16)
    lo = (x - hi.astype(F32)).astype(BF16)
    return hi, lo


def _outproj_kernel(mp_ref, xp_ref, ms_ref, xs_ref, w_ref, g_ref, rwt_ref, rb_ref,
                    x2_ref, hn_ref, lg_ref, *, ntp):
    def run(mixed_ref, x_ref):
        x2 = x_ref[...] + jnp.dot(mixed_ref[...], w_ref[...], preferred_element_type=F32)
        x2_ref[...] = x2
        hn = _rms(x2, g_ref[...])
        _store_token_tiles(hn_ref, hn)
        h_hi, h_lo = _split_bf16(hn)
        w_hi, w_lo = _split_bf16(rwt_ref[...])
        nt = (((1,), (1,)), ((), ()))
        lg = lax.dot_general(w_hi, h_hi, nt, preferred_element_type=F32)
        lg = lg + lax.dot_general(w_hi, h_lo, nt, preferred_element_type=F32)
        lg = lg + lax.dot_general(w_lo, h_hi, nt, preferred_element_type=F32)
        lg_ref[...] = lg + rb_ref[...]

    i = pl.program_id(0)

    @pl.when(i < ntp)
    def _():
        run(mp_ref, xp_ref)

    @pl.when(i >= ntp)
    def _():
        run(ms_ref, xs_ref)


def _outproj(mixed_p, xp2d, mixed_s, xs2d, w_bf, g, rwt, rb, *, tm):
    tp, d = xp2d.shape
    tsm = xs2d.shape[0]
    ne = rwt.shape[0]
    ntp = tp // tm
    nts = tsm // tm
    total_rows = tp + tsm
    prow = pl.BlockSpec((tm, d), lambda i: (jnp.minimum(i, ntp - 1), 0))
    srow = pl.BlockSpec((tm, d), lambda i: (jnp.maximum(i - ntp, 0), 0))
    full = lambda shape: pl.BlockSpec(shape, lambda i: (0,) * len(shape))
    return pl.pallas_call(
        functools.partial(_outproj_kernel, ntp=ntp),
        grid=(ntp + nts,),
        in_specs=[prow, prow, srow, srow, full((d, d)), full((1, d)), full((ne, d)), full((ne, 1))],
        out_specs=[pl.BlockSpec((tm, d), lambda i: (i, 0)),
                   pl.BlockSpec((tm * SUBLANES, LANES), lambda i: (i, 0)),
                   pl.BlockSpec((ne, tm), lambda i: (0, i))],
        out_shape=[jax.ShapeDtypeStruct((total_rows, d), F32),
                   jax.ShapeDtypeStruct((total_rows * SUBLANES, LANES), F32),
                   jax.ShapeDtypeStruct((ne, total_rows), F32)],
        compiler_params=_cparams(("arbitrary",)),
        name="outproj",
    )(mixed_p, xp2d, mixed_s, xs2d, w_bf, g, rwt, rb)


def _route_kernel(lg_ref, e_ref, r_ref, g_ref, cnt_ref, carry_s, *, tr):
    @pl.when(pl.program_id(0) == 0)
    def _():
        carry_s[...] = jnp.zeros(carry_s.shape, F32)

    work = lg_ref[...]
    ne = work.shape[0]
    eid = lax.broadcasted_iota(I32, (ne, tr), 0).astype(F32)
    sels, vals, idxs = [], [], []
    for _ in range(TOP_K):
        m = work.max(axis=0, keepdims=True)
        idx = jnp.where(work == m, eid, float(ne)).min(axis=0, keepdims=True)
        sel = eid == idx
        sels.append(sel)
        vals.append(m)
        idxs.append(idx)
        work = jnp.where(sel, -jnp.inf, work)
    ex = [jnp.exp(v - vals[0]) for v in vals]
    den = ex[0] + ex[1] + ex[2] + ex[3]
    chosen = jnp.zeros((ne, tr), F32)
    for sel in sels:
        chosen = chosen + sel.astype(F32)
    rr = lax.broadcasted_iota(I32, (tr, tr), 0)
    cc = lax.broadcasted_iota(I32, (tr, tr), 1)
    upper = jnp.where(rr < cc, 1.0, 0.0).astype(BF16)
    before = jnp.dot(chosen.astype(BF16), upper, preferred_element_type=F32)
    carry = carry_s[:, 0:1]
    rank_all = before + carry
    for k in range(TOP_K):
        e_ref[k:k + 1, :] = idxs[k].astype(I32)
        g_ref[k:k + 1, :] = ex[k] / den
        r_ref[k:k + 1, :] = jnp.where(sels[k], rank_all, 0.0).sum(axis=0, keepdims=True).astype(I32)
    new_carry = carry + chosen.sum(axis=1, keepdims=True)
    carry_s[...] = jnp.broadcast_to(new_carry, carry_s.shape)
    cnt_ref[...] = jnp.broadcast_to(new_carry, cnt_ref.shape).astype(I32)


def _route(lgt, *, tr):
    ne, t = lgt.shape
    blk = lambda rows: pl.BlockSpec((rows, tr), lambda i: (0, i))
    return pl.pallas_call(
        functools.partial(_route_kernel, tr=tr),
        grid=(t // tr,),
        in_specs=[blk(ne)],
        out_specs=[blk(TOP_K), blk(TOP_K), blk(TOP_K), pl.BlockSpec((ne, LANES), lambda i: (0, 0))],
        out_shape=[jax.ShapeDtypeStruct((TOP_K, t), I32),
                   jax.ShapeDtypeStruct((TOP_K, t), I32),
                   jax.ShapeDtypeStruct((TOP_K, t), F32),
                   jax.ShapeDtypeStruct((ne, LANES), I32)],
        scratch_shapes=[pltpu.VMEM((ne, LANES), F32)],
        compiler_params=_cparams(("arbitrary",)),
        name="route",
    )(lgt)


def _plan_kernel(cnt_ref, meta_ref, be_ref, *, ne, nblocks, blk):
    acc = jnp.int32(0)
    for e in range(ne):
        c = cnt_ref[e, 0]
        pc = ((c + (blk - 1)) // blk) * blk
        meta_ref[e] = acc
        b0 = acc // blk
        acc = acc + pc
        meta_ref[ne + e] = acc
        b1 = acc // blk

        def fill(b, carry, e=e):
            be_ref[b] = jnp.int32(e)
            return carry

        lax.fori_loop(b0, b1, fill, 0)
    used = acc // blk
    meta_ref[2 * ne] = used

    def fill_rest(b, carry):
        be_ref[b] = jnp.int32(ne - 1)
        return carry

    lax.fori_loop(used, nblocks, fill_rest, 0)
    for j in range(2 * ne + 1, meta_ref.shape[0]):
        meta_ref[j] = jnp.int32(0)


def _plan(counts, *, nblocks, blk):
    ne = counts.shape[0]
    smem = pl.BlockSpec(memory_space=pltpu.SMEM)
    return pl.pallas_call(
        functools.partial(_plan_kernel, ne=ne, nblocks=nblocks, blk=blk),
        in_specs=[smem],
        out_specs=[smem, smem],
        out_shape=[jax.ShapeDtypeStruct((LANES,), I32), jax.ShapeDtypeStruct((nblocks,), I32)],
        name="plan",
    )(counts)


def _slots_kernel(e_ref, r_ref, cnt_ref, d_ref, *, blk):
    ne = cnt_ref.shape[0]
    tr = e_ref.shape[1]
    padded = jnp.bitwise_and(cnt_ref[:, 0:1] + (blk - 1), -blk).astype(F32)
    eid1 = lax.broadcasted_iota(I32, (ne, 1), 0)
    start = jnp.zeros((ne, 1), F32)
    for e in range(ne - 1):
        start = start + jnp.where(eid1 > e, padded[e:e + 1, :], 0.0)
    eid = lax.broadcasted_iota(I32, (ne, tr), 0)
    for k in range(TOP_K):
        hit = eid == e_ref[k:k + 1, :]
        base = jnp.where(hit, start, 0.0).sum(axis=0, keepdims=True)
        d_ref[k:k + 1, :] = base.astype(I32) + r_ref[k:k + 1, :]


def _slots(e_all, r_all, counts, *, blk, tr):
    t = e_all.shape[1]
    blk_spec = pl.BlockSpec((TOP_K, tr), lambda i: (0, i))
    return pl.pallas_call(
        functools.partial(_slots_kernel, blk=blk),
        grid=(t // tr,),
        in_specs=[blk_spec, blk_spec, pl.BlockSpec(counts.shape, lambda i: (0, 0))],
        out_specs=blk_spec,
        out_shape=jax.ShapeDtypeStruct((TOP_K, t), I32),
        compiler_params=_cparams(("arbitrary",)),
        name="slots",
    )(e_all, r_all, counts)


DISPATCH_RING = 3


def _dispatch_kernel(meta_ref, d_ref, hn_ref, xs_ref, zero_s, ring, zsem, in_sem, out_sem,
                     *, td, ne, blk, nblocks, nsteps):
    i = pl.program_id(0)
    rows = td * SUBLANES
    slot = i % DISPATCH_RING
    nxt = (i + 1) % DISPATCH_RING

    def load(step, s):
        src = hn_ref.at[pl.ds(pl.multiple_of(step * rows, rows), rows), :]
        return pltpu.make_async_copy(src, ring.at[s], in_sem.at[s])

    def wait_scatters(s):
        for _ in range(TOP_K):
            pltpu.make_async_copy(ring.at[s], xs_ref.at[pl.ds(0, rows), :], out_sem.at[s]).wait()

    @pl.when(i == 0)
    def _():
        zero_s[...] = jnp.zeros(zero_s.shape, F32)

        def zcopy(e):
            row0 = pl.multiple_of(jnp.maximum(meta_ref[ne + e] - blk, 0) * SUBLANES, SUBLANES)
            return pltpu.make_async_copy(zero_s, xs_ref.at[pl.ds(row0, blk * SUBLANES), :], zsem)

        for e in range(ne):
            zcopy(e).start()
        for e in range(ne):
            zcopy(e).wait()

        def ztail(b, carry):
            row0 = pl.multiple_of(b * (blk * SUBLANES), blk * SUBLANES)
            cp = pltpu.make_async_copy(zero_s, xs_ref.at[pl.ds(row0, blk * SUBLANES), :], zsem)
            cp.start()
            cp.wait()
            return carry

        lax.fori_loop(meta_ref[2 * ne], nblocks, ztail, 0)
        load(0, 0).start()

    @pl.when(i >= DISPATCH_RING - 1)
    def _():
        wait_scatters(nxt)

    @pl.when(i + 1 < nsteps)
    def _():
        load(i + 1, nxt).start()

    load(i, slot).wait()

    def issue(r, carry):
        src = ring.at[slot, pl.ds(pl.multiple_of(r * SUBLANES, SUBLANES), SUBLANES), :]
        for k in range(TOP_K):
            dst = xs_ref.at[pl.ds(pl.multiple_of(d_ref[k, r] * SUBLANES, SUBLANES), SUBLANES), :]
            pltpu.make_async_copy(src, dst, out_sem.at[slot]).start(priority=k % 2)
        return carry

    lax.fori_loop(0, td, issue, 0, unroll=2)

    @pl.when(i == nsteps - 1)
    def _():
        wait_scatters(slot)

        @pl.when(i >= 1)
        def _():
            wait_scatters((i + DISPATCH_RING - 1) % DISPATCH_RING)


def _dispatch(meta, dest, hn_all, *, n_slots, td, blk, ne):
    t = hn_all.shape[0] // SUBLANES
    nsteps = t // td
    assert nsteps >= DISPATCH_RING - 1
    grid_spec = pltpu.PrefetchScalarGridSpec(
        num_scalar_prefetch=1,
        grid=(nsteps,),
        in_specs=[pl.BlockSpec((TOP_K, td), lambda i, m: (0, i), memory_space=pltpu.SMEM),
                  pl.BlockSpec(memory_space=pl.ANY)],
        out_specs=pl.BlockSpec(memory_space=pl.ANY),
        scratch_shapes=[pltpu.VMEM((blk * SUBLANES, LANES), F32),
                        pltpu.VMEM((DISPATCH_RING, td * SUBLANES, LANES), F32),
                        pltpu.SemaphoreType.DMA(()),
                        pltpu.SemaphoreType.DMA((DISPATCH_RING,)),
                        pltpu.SemaphoreType.DMA((DISPATCH_RING,))],
    )
    return pl.pallas_call(
        functools.partial(_dispatch_kernel, td=td, ne=ne, blk=blk, nblocks=n_slots // blk, nsteps=nsteps),
        grid_spec=grid_spec,
        out_shape=jax.ShapeDtypeStruct((n_slots * SUBLANES, LANES), F32),
        compiler_params=_cparams(("arbitrary",)),
        name="dispatch",
    )(meta, dest, hn_all)


def _experts_kernel(be_ref, meta_ref, x_ref, w1_ref, b1g_ref, b1l_ref, w2_ref, b2_ref,
                    y_ref, w1_s, w2_s, *, ne, blk):
    b = pl.program_id(0)
    used = meta_ref[2 * ne]
    half = MXU_COLS // 2
    ngroups = w1_s.shape[1] // MXU_COLS
    fresh = jnp.logical_or(b == 0, be_ref[b] != be_ref[jnp.maximum(b - 1, 0)])

    @pl.when(jnp.logical_and(fresh, b < used))
    def _():
        src = lax.broadcasted_iota(I32, (MXU_COLS, MXU_COLS), 0)
        dst = lax.broadcasted_iota(I32, (MXU_COLS, MXU_COLS), 1)
        want = jnp.where(dst < half, 2 * dst, 2 * (dst - half) + 1)
        perm = jnp.where(src == want, 1.0, 0.0).astype(BF16)
        for t in range(ngroups):
            cols = slice(t * MXU_COLS, (t + 1) * MXU_COLS)
            w = w1_ref[0, :, cols].astype(BF16)
            w1_s[:, cols] = jnp.dot(w, perm, preferred_element_type=F32).astype(BF16)
        w2_s[...] = w2_ref[0].astype(BF16)

    @pl.when(b < used)
    def _():
        xb = _load_token_tiles(x_ref, blk).astype(BF16)
        acts = []
        for t in range(ngroups):
            h = jnp.dot(xb, w1_s[:, t * MXU_COLS:(t + 1) * MXU_COLS], preferred_element_type=F32)
            hg = h[:, :half] + b1g_ref[0, :, t * half:(t + 1) * half]
            hl = h[:, half:] + b1l_ref[0, :, t * half:(t + 1) * half]
            g = jnp.minimum(hg, SWIGLU_LIMIT)
            lin = jnp.clip(hl, -SWIGLU_LIMIT, SWIGLU_LIMIT)
            acts.append((g * _sigmoid(SWIGLU_ALPHA * g) * (lin + 1.0)).astype(BF16))
        act = jnp.concatenate(acts, axis=1)
        y = jnp.dot(act, w2_s[...], preferred_element_type=F32) + b2_ref[0]
        _store_token_tiles(y_ref, y)

    @pl.when(b >= used)
    def _():
        y_ref[...] = jnp.zeros(y_ref.shape, F32)


def _experts(be, meta, xs, w1, b1g, b1l, w2, b2, *, blk):
    ne, d, dff2 = w1.shape
    dff = dff2 // 2
    n_slots = xs.shape[0] // SUBLANES
    nblocks = n_slots // blk

    def xmap(b, be_r, meta_r):
        return (jnp.minimum(b, meta_r[2 * ne] - 1), 0)

    def wmap(b, be_r, meta_r):
        return (be_r[b], 0, 0)

    grid_spec = pltpu.PrefetchScalarGridSpec(
        num_scalar_prefetch=2,
        grid=(nblocks,),
        in_specs=[pl.BlockSpec((blk * SUBLANES, LANES), xmap),
                  pl.BlockSpec((1, d, dff2), wmap),
                  pl.BlockSpec((1, 1, dff), wmap),
                  pl.BlockSpec((1, 1, dff), wmap),
                  pl.BlockSpec((1, dff, d), wmap),
                  pl.BlockSpec((1, 1, d), wmap)],
        out_specs=pl.BlockSpec((blk * SUBLANES, LANES), lambda b, be_r, meta_r: (b, 0)),
        scratch_shapes=[pltpu.VMEM((d, dff2), BF16), pltpu.VMEM((dff, d), BF16)],
    )
    return pl.pallas_call(
        functools.partial(_experts_kernel, ne=ne, blk=blk),
        grid_spec=grid_spec,
        out_shape=jax.ShapeDtypeStruct((n_slots * SUBLANES, LANES), F32),
        compiler_params=_cparams(("arbitrary",)),
        name="experts",
    )(be, meta, xs, w1, b1g, b1l, w2, b2)


def _combine_kernel(d_ref, dn_ref, x2_ref, gt_ref, gn_ref, ys_ref, y_ref, buf, sem, *, tc, nsteps):
    i = pl.program_id(0)
    slot = i % 2
    rows = tc * SUBLANES

    def gather(idx_ref, s):
        def body(r, carry):
            for k in range(TOP_K):
                src = ys_ref.at[pl.ds(pl.multiple_of(idx_ref[k, r] * SUBLANES, SUBLANES), SUBLANES), :]
                dst = buf.at[s, k, pl.ds(pl.multiple_of(r * SUBLANES, SUBLANES), SUBLANES), :]
                pltpu.make_async_copy(src, dst, sem.at[s]).start(priority=k % 2)
            return carry

        lax.fori_loop(0, tc, body, 0, unroll=2)

    @pl.when(i == 0)
    def _():
        gather(d_ref, 0)

    @pl.when(i + 1 < nsteps)
    def _():
        gather(dn_ref, 1 - slot)

    for k in range(TOP_K):
        pltpu.make_async_copy(ys_ref.at[pl.ds(0, rows), :], buf.at[slot, k], sem.at[slot]).wait()
    acc = x2_ref[...]
    for k in range(TOP_K):
        acc = acc + gt_ref[:, k:k + 1] * _load_token_tiles(buf.at[slot, k], tc)
    y_ref[...] = _rms(acc, gn_ref[...])


def _combine(dest, x2_all, gt_all, gn, ys, *, row_off, rows, tc):
    _, d = x2_all.shape
    boff = row_off // tc
    nsteps = rows // tc
    last = boff + nsteps - 1
    smem_blk = lambda fn: pl.BlockSpec((TOP_K, tc), fn, memory_space=pltpu.SMEM)
    return pl.pallas_call(
        functools.partial(_combine_kernel, tc=tc, nsteps=nsteps),
        grid=(nsteps,),
        in_specs=[smem_blk(lambda i: (0, i + boff)),
                  smem_blk(lambda i: (0, jnp.minimum(i + boff + 1, last))),
                  pl.BlockSpec((tc, d), lambda i: (i + boff, 0)),
                  pl.BlockSpec((tc, TOP_K), lambda i: (i + boff, 0)),
                  pl.BlockSpec((1, d), lambda i: (0, 0)),
                  pl.BlockSpec(memory_space=pl.ANY)],
        out_specs=pl.BlockSpec((tc, d), lambda i: (i, 0)),
        out_shape=jax.ShapeDtypeStruct((rows, d), F32),
        scratch_shapes=[pltpu.VMEM((2, TOP_K, tc * SUBLANES, LANES), F32), pltpu.SemaphoreType.DMA((2,))],
        compiler_params=_cparams(("arbitrary",)),
        name="combine",
    )(dest, dest, x2_all, gt_all, gn, ys)


def _pick_tile(n, pref):
    t = min(pref, n)
    while n % t:
        t //= 2
    return t


def kernel(x_prompt, x_sample, cache_k, cache_v, state_conv, state_lru, norm_mix, w_in, conv_w, conv_b,
           lru_wa, lru_ba, lru_wx, lru_bx, lru_lambda, rel_bias, w_out, norm_ffn, router_w, router_b,
           w1, b1, w2, b2, norm_out):
    depth = w_in.shape[0]
    assert depth == 1, "single-layer step"
    bp, sp, d = x_prompt.shape
    bs, ss, _ = x_sample.shape
    n_heads = d // HEAD_DIM
    band = LEFT_CHUNKS * CHUNK
    n_keep = min(band, sp)
    n_cache = cache_k.shape[2]
    ne = router_w.shape[-1]
    dff = w2.shape[2]
    l = 0
    assert PAST_LEN % CHUNK == 0 and n_cache == band and ss == CHUNK
    assert d == SUBLANES * LANES, "token-tile layout holds one token per (8,128) tile"

    w_in_bf = w_in[l].astype(BF16)
    w_out_bf = w_out[l].astype(BF16)
    g_mix = norm_mix[l].reshape(1, d)
    g_ffn = norm_ffn[l].reshape(1, d)
    g_out = norm_out.reshape(1, d)
    wg = jnp.concatenate([lru_wa[l], lru_wx[l]], axis=-1).astype(BF16)
    ba = lru_ba[l].reshape(1, d)
    bx = lru_bx[l].reshape(1, d)
    lam = lru_lambda[l].reshape(1, d)
    cw = conv_w[l]
    cb = conv_b[l].reshape(1, d)
    rwt = router_w[l].T
    rb = router_b[l].reshape(ne, 1)
    b1g = b1[l][:, 0::2].reshape(ne, 1, dff)
    b1l = b1[l][:, 1::2].reshape(ne, 1, dff)
    b2r = b2[l].reshape(ne, 1, d)

    tp = bp * sp
    tsm = bs * ss
    t_all = tp + tsm

    gq = ATT_GROUP * CHUNK
    bias_p = _bias_table(rel_bias[l], gq, band + gq)
    bias_s = _bias_table(rel_bias[l], ss, n_cache + ss)

    xp2d = x_prompt.reshape(tp, d)
    conv0 = jnp.zeros((bp, CONV_W - 1, d), F32)
    lru0 = jnp.zeros((bp, 1, d), F32)
    qp, kp, vp, ga_p, mr_p, ktp, vtp, cnew_p, hlast_p = _mixer_prompt(
        xp2d, g_mix, w_in_bf, conv0, lru0, cw, cb, wg, ba, bx, lam,
        batch=bp, seq=sp, n_keep=n_keep, tm=_pick_tile(sp, ROW_TILE))
    mixed_p = _attn_prompt(qp, kp, vp, bias_p, ga_p, mr_p, batch=bp, seq=sp)

    xs2d = x_sample.reshape(tsm, d)
    xr, q, k, v, gr, ga_s, kts, vts = _inproj(xs2d, g_mix, w_in_bf, tm=_pick_tile(tsm, ROW_TILE))
    mr_s, cnew_s, hlast_s = _lru(xr, gr, state_conv[l], state_lru[l].reshape(bs, 1, d), cw, cb, wg, ba, bx, lam,
                                 batch=bs, seq=ss, ts=ss, starts_at_zero=False)
    ck = cache_k[l].reshape(bs * n_cache * n_heads, HEAD_DIM)
    cv = cache_v[l].reshape(bs * n_cache * n_heads, HEAD_DIM)
    mixed_s = _attn_sample(q, k, v, ck, cv, bias_s, ga_s, mr_s, batch=bs, seq=ss)

    tmo = _pick_tile(tsm, ROW_TILE)
    x2_all, hn_all, lgt_all = _outproj(mixed_p, xp2d, mixed_s, xs2d, w_out_bf, g_ffn, rwt, rb, tm=tmo)

    tr = _pick_tile(t_all, ROUTE_TILE)
    e_all, r_all, gates, counts = _route(lgt_all, tr=tr)
    blk = MOE_BLOCK
    nblocks = -(-(t_all * TOP_K) // blk) + ne
    meta, be = _plan(counts, nblocks=nblocks, blk=blk)
    dest = _slots(e_all, r_all, counts, blk=blk, tr=tr)
    td = _pick_tile(t_all, 512)
    xs = _dispatch(meta, dest, hn_all, n_slots=nblocks * blk, td=td, blk=blk, ne=ne)
    ys = _experts(be, meta, xs, w1[l], b1g, b1l, w2[l], b2r, blk=blk)
    gt_all = gates.T
    tc = _pick_tile(tsm, 256)
    y_p = _combine(dest, x2_all, gt_all, g_out, ys, row_off=0, rows=tp, tc=tc)
    y_s = _combine(dest, x2_all, gt_all, g_out, ys, row_off=tp, rows=tsm, tc=tc)

    return (y_p.reshape(bp, sp, d), y_s.reshape(bs, ss, d),
            ktp.reshape(1, bp, n_keep, n_heads, HEAD_DIM), vtp.reshape(1, bp, n_keep, n_heads, HEAD_DIM),
            cnew_p[None], hlast_p.reshape(1, bp, d),
            kts.reshape(1, bs, ss, n_heads, HEAD_DIM), vts.reshape(1, bs, ss, n_heads, HEAD_DIM),
            cnew_s[None], hlast_s.reshape(1, bs, d))
```

```python
import functools

import jax
import jax.numpy as jnp
from jax import lax
from jax.experimental import pallas as pl
from jax.experimental.pallas import tpu as pltpu

F32 = jnp.float32
BF16 = jnp.bfloat16
I32 = jnp.int32

CHUNK = 64
LEFT_CHUNKS = 8
HEAD_DIM = 128
MAX_REL = 256
NEG_INF = -1e30
LRU_BLOCKS = 8
CONV_W = 4
LRU_C = 8.0
TOP_K = 4
SWIGLU_ALPHA = 1.702
SWIGLU_LIMIT = 7.0
RMS_EPS = 1e-6
PAST_LEN = 1024
LOG2_E = 1.4426950408889634
QK_SCALE = HEAD_DIM ** -0.5 * LOG2_E

LANES = 128
SUBLANES = 8
MXU_COLS = 256
VMEM_LIMIT = 56 * 1024 * 1024
MOE_BLOCK = 512
ROUTE_TILE = 1024
ROW_TILE = 512
ATT_GROUP = 4


def _cparams(sem):
    return pltpu.CompilerParams(dimension_semantics=sem, vmem_limit_bytes=VMEM_LIMIT)


def _sigmoid(x):
    return 0.5 * jnp.tanh(0.5 * x) + 0.5


def _rms(x, g):
    ms = jnp.mean(x * x, axis=-1, keepdims=True)
    return x * lax.rsqrt(ms + RMS_EPS) * g


def _inproj_kernel(x_ref, g_ref, w_ref, xr_ref, q_ref, k_ref, v_ref, gr_ref, ga_ref,
                   kt_ref, vt_ref, *, d):
    u = _rms(x_ref[...], g_ref[...]).astype(BF16)

    def piece(j):
        return jnp.dot(u, w_ref[:, j * d:(j + 1) * d], preferred_element_type=F32)

    xr_ref[...] = piece(0)
    q_ref[...] = (piece(1) * QK_SCALE).astype(BF16)
    kf = piece(2)
    k_ref[...] = kf.astype(BF16)
    kt_ref[...] = kf
    vf = piece(3)
    v_ref[...] = vf.astype(BF16)
    vt_ref[...] = vf
    gr_ref[...] = piece(4).astype(BF16)
    ga_ref[...] = piece(5).astype(BF16)


def _inproj(x2d, g, w_bf, *, tm):
    n, d = x2d.shape
    row = pl.BlockSpec((tm, d), lambda i: (i, 0))
    bf = jax.ShapeDtypeStruct((n, d), BF16)
    f32 = jax.ShapeDtypeStruct((n, d), F32)
    return pl.pallas_call(
        functools.partial(_inproj_kernel, d=d),
        grid=(n // tm,),
        in_specs=[row,
                  pl.BlockSpec((1, d), lambda i: (0, 0)),
                  pl.BlockSpec((d, 6 * d), lambda i: (0, 0), pipeline_mode=pl.Buffered(1))],
        out_specs=[row] * 8,
        out_shape=[f32, bf, bf, bf, bf, bf, f32, f32],
        compiler_params=_cparams(("arbitrary",)),
        name="inproj",
    )(x2d, g, w_bf)


def _lru_kernel(xr_ref, gr_ref, cs_ref, h0_ref, cw_ref, cb_ref, wg_ref, ba_ref, bx_ref, lam_ref,
                mr_ref, cnew_ref, hlast_ref, xp_s, a_s, b_s, hs_s, h_s, *, ts, starts_at_zero):
    _lru_init(pl.program_id(1), cs_ref, h0_ref, xp_s, h_s)
    _lru_steps(xr_ref[...], gr_ref[...].astype(F32), pl.program_id(1),
               cw_ref, cb_ref, wg_ref, ba_ref, bx_ref, lam_ref,
               mr_ref, cnew_ref, hlast_ref, xp_s, a_s, b_s, hs_s, h_s,
               ts=ts, starts_at_zero=starts_at_zero)


def _lru_init(t, cs_ref, h0_ref, xp_s, h_s):
    head = SUBLANES
    hist = CONV_W - 1

    @pl.when(t == 0)
    def _():
        xp_s[0:head, :] = jnp.zeros((head, xp_s.shape[1]), F32)
        xp_s[head - hist:head, :] = cs_ref[0]
        h_s[...] = jnp.broadcast_to(h0_ref[0], h_s.shape)


def _lru_steps(xr, gr, t, cw_ref, cb_ref, wg_ref, ba_ref, bx_ref, lam_ref,
               mr_ref, cnew_ref, hlast_ref, xp_s, a_s, b_s, hs_s, h_s, *, ts, starts_at_zero):
    head = SUBLANES
    hist = CONV_W - 1
    xp_s[head:head + ts, :] = xr
    cw = cw_ref[...]
    xc = cb_ref[...] + xp_s[head - hist:head - hist + ts, :] * cw[0:1, :]
    for j in range(1, CONV_W):
        xc = xc + xp_s[head - hist + j:head - hist + j + ts, :] * cw[j:j + 1, :]
    tail = xp_s[ts + head - hist:ts + head, :]
    xp_s[head - hist:head, :] = tail
    cnew_ref[0] = tail

    lam = lam_ref[...]
    z = -lam
    softplus = jnp.maximum(z, 0.0) + jnp.log1p(jnp.exp(-jnp.abs(z)))
    c = -LRU_C * softplus
    bw = xc.shape[1] // LRU_BLOCKS
    if starts_at_zero:
        first = (lax.broadcasted_iota(I32, (ts, bw), 0) + t * ts) == 0
    for g in range(LRU_BLOCKS):
        sl = slice(g * bw, (g + 1) * bw)
        xg = xc[:, sl]
        pre = jnp.dot(xg.astype(BF16), wg_ref[g], preferred_element_type=F32)
        r = _sigmoid(pre[:, :bw] + ba_ref[:, sl])
        ig = _sigmoid(pre[:, bw:] + bx_ref[:, sl])
        log_a = c[:, sl] * r
        a = jnp.exp(log_a)
        mult = jnp.sqrt(1.0 - a * a)
        if starts_at_zero:
            mult = jnp.where(first, 1.0, mult)
        a_s[:, sl] = a
        b_s[:, sl] = mult * (ig * xg)

    sub = lax.broadcasted_iota(I32, (SUBLANES, xc.shape[1]), 0)

    def scan_body(i, h):
        base = pl.multiple_of(i * SUBLANES, SUBLANES)
        a = a_s[pl.ds(base, SUBLANES), :]
        b = b_s[pl.ds(base, SUBLANES), :]
        for shift in (1, 2, 4):
            a_prev = jnp.where(sub < shift, 1.0, pltpu.roll(a, shift, 0))
            b_prev = jnp.where(sub < shift, 0.0, pltpu.roll(b, shift, 0))
            b = a * b_prev + b
            a = a * a_prev
        rows = a * h + b
        hs_s[pl.ds(base, SUBLANES), :] = rows
        return jnp.broadcast_to(rows[SUBLANES - 1:SUBLANES, :], rows.shape)

    h = lax.fori_loop(0, ts // SUBLANES, scan_body, h_s[...])
    h_s[...] = h
    hlast_ref[0] = h[0:1, :]
    mr_ref[...] = (_sigmoid(gr) * hs_s[...]).astype(BF16)


def _lru(xr, gr, conv_state, h0, conv_w, conv_b, wg, ba, bx, lam, *, batch, seq, ts, starts_at_zero):
    n, d = xr.shape
    ntb = seq // ts
    row = pl.BlockSpec((ts, d), lambda b, t: (b * ntb + t, 0))
    vec = pl.BlockSpec((1, d), lambda b, t: (0, 0))
    return pl.pallas_call(
        functools.partial(_lru_kernel, ts=ts, starts_at_zero=starts_at_zero),
        grid=(batch, ntb),
        in_specs=[row, row,
                  pl.BlockSpec((1, CONV_W - 1, d), lambda b, t: (b, 0, 0)),
                  pl.BlockSpec((1, 1, d), lambda b, t: (b, 0, 0)),
                  pl.BlockSpec((CONV_W, d), lambda b, t: (0, 0)),
                  vec,
                  pl.BlockSpec(wg.shape, lambda b, t: (0, 0, 0)),
                  vec, vec, vec],
        out_specs=[row,
                   pl.BlockSpec((1, CONV_W - 1, d), lambda b, t: (b, 0, 0)),
                   pl.BlockSpec((1, 1, d), lambda b, t: (b, 0, 0))],
        out_shape=[jax.ShapeDtypeStruct((n, d), BF16),
                   jax.ShapeDtypeStruct((batch, CONV_W - 1, d), F32),
                   jax.ShapeDtypeStruct((batch, 1, d), F32)],
        scratch_shapes=_lru_scratch(ts, d),
        compiler_params=_cparams(("arbitrary", "arbitrary")),
        name="lru",
    )(xr, gr, conv_state, h0, conv_w, conv_b, wg, ba, bx, lam)


def _lru_scratch(ts, d):
    return [pltpu.VMEM((ts + SUBLANES, d), F32),
            pltpu.VMEM((ts, d), F32),
            pltpu.VMEM((ts, d), F32),
            pltpu.VMEM((ts, d), F32),
            pltpu.VMEM((SUBLANES, d), F32)]


def _mixer_kernel(x_ref, g_ref, w_ref, cs_ref, h0_ref, cw_ref, cb_ref, wg_ref, ba_ref, bx_ref,
                  lam_ref, q_ref, k_ref, v_ref, ga_ref, mr_ref, kt_ref, vt_ref, cnew_ref, hlast_ref,
                  xp_s, a_s, b_s, hs_s, h_s, *, d, ntb):
    t = pl.program_id(0) % ntb
    _lru_init(t, cs_ref, h0_ref, xp_s, h_s)
    u = _rms(x_ref[...], g_ref[...]).astype(BF16)

    def piece(j):
        return jnp.dot(u, w_ref[:, j * d:(j + 1) * d], preferred_element_type=F32)

    q_ref[...] = (piece(1) * QK_SCALE).astype(BF16)
    kf = piece(2)
    k_ref[...] = kf.astype(BF16)
    kt_ref[...] = kf
    vf = piece(3)
    v_ref[...] = vf.astype(BF16)
    vt_ref[...] = vf
    ga_ref[...] = piece(5).astype(BF16)
    _lru_steps(piece(0), piece(4), t,
               cw_ref, cb_ref, wg_ref, ba_ref, bx_ref, lam_ref,
               mr_ref, cnew_ref, hlast_ref, xp_s, a_s, b_s, hs_s, h_s,
               ts=x_ref.shape[0], starts_at_zero=True)


def _mixer_prompt(x2d, g, w_bf, conv_state, h0, conv_w, conv_b, wg, ba, bx, lam, *, batch, seq, n_keep, tm):
    n, d = x2d.shape
    ntb = seq // tm
    nk = n_keep // tm
    tail_map = lambda i: ((i // ntb) * nk + jnp.maximum(i % ntb - (ntb - nk), 0), 0)
    row = pl.BlockSpec((tm, d), lambda i: (i, 0))
    vec = pl.BlockSpec((1, d), lambda i: (0, 0))
    per_batch = lambda rows: pl.BlockSpec((1, rows, d), lambda i: (i // ntb, 0, 0))
    bf = jax.ShapeDtypeStruct((n, d), BF16)
    tail = jax.ShapeDtypeStruct((batch * n_keep, d), F32)
    return pl.pallas_call(
        functools.partial(_mixer_kernel, d=d, ntb=ntb),
        grid=(n // tm,),
        in_specs=[row, vec,
                  pl.BlockSpec((d, 6 * d), lambda i: (0, 0), pipeline_mode=pl.Buffered(1)),
                  per_batch(CONV_W - 1), per_batch(1),
                  pl.BlockSpec((CONV_W, d), lambda i: (0, 0)), vec,
                  pl.BlockSpec(wg.shape, lambda i: (0, 0, 0)), vec, vec, vec],
        out_specs=[row, row, row, row, row,
                   pl.BlockSpec((tm, d), tail_map), pl.BlockSpec((tm, d), tail_map),
                   per_batch(CONV_W - 1), per_batch(1)],
        out_shape=[bf, bf, bf, bf, bf, tail, tail,
                   jax.ShapeDtypeStruct((batch, CONV_W - 1, d), F32),
                   jax.ShapeDtypeStruct((batch, 1, d), F32)],
        scratch_shapes=_lru_scratch(tm, d),
        compiler_params=_cparams(("arbitrary",)),
        name="mixer",
    )(x2d, g, w_bf, conv_state, h0, conv_w, conv_b, wg, ba, bx, lam)


def _bias_kernel(f_ref, o_ref, *, nq, nkeys, width):
    f = f_ref[0]
    x = jnp.broadcast_to(f, (nq, width))
    rolled = pltpu.roll(x, width - (nq - 1), 1, stride=1, stride_axis=0)
    t = rolled[:, :nkeys]
    qc = lax.broadcasted_iota(I32, (nq, nkeys), 0) // CHUNK
    kc = lax.broadcasted_iota(I32, (nq, nkeys), 1) // CHUNK
    ok = jnp.logical_and(kc >= qc, kc <= qc + LEFT_CHUNKS)
    o_ref[0] = jnp.where(ok, t * LOG2_E, NEG_INF)


def _bias_table(rel_bias, nq, nkeys):
    nh = rel_bias.shape[0]
    band = LEFT_CHUNKS * CHUNK
    width = -(-(nkeys + nq) // LANES) * LANES
    left = band + (nq - 1) - MAX_REL
    right = max(width - left - (2 * MAX_REL + 1), 0)
    flipped = rel_bias[:, ::-1]
    f = jnp.pad(flipped, ((0, 0), (left, right)), mode="edge")[:, :width].reshape(nh, 1, width)
    return pl.pallas_call(
        functools.partial(_bias_kernel, nq=nq, nkeys=nkeys, width=width),
        grid=(nh,),
        in_specs=[pl.BlockSpec((1, 1, width), lambda h: (h, 0, 0))],
        out_specs=pl.BlockSpec((1, nq, nkeys), lambda h: (h, 0, 0)),
        out_shape=jax.ShapeDtypeStruct((nh, nq, nkeys), F32),
        compiler_params=_cparams(("arbitrary",)),
        name="bias",
    )(f)


def _attn_kernel(*refs, part_rows, head_major, n_heads, mask_parts):
    np_ = len(part_rows)
    q_ref = refs[0]
    k_refs = refs[1:1 + np_]
    v_refs = refs[1 + np_:1 + 2 * np_]
    bias_ref, ga_ref, mr_ref, o_ref = refs[1 + 2 * np_:]
    i = pl.program_id(1)

    def head(ref, p, h):
        if head_major[p]:
            return ref[pl.ds(h, part_rows[p], stride=n_heads), :].astype(BF16)
        return ref[:, h * HEAD_DIM:(h + 1) * HEAD_DIM].astype(BF16)

    def one_head(h, masked):
        hs = slice(h * HEAD_DIM, (h + 1) * HEAD_DIM)
        qh = q_ref[:, hs]
        s_parts = []
        off = 0
        for p in range(np_):
            kp = head(k_refs[p], p, h)
            s = lax.dot_general(qh, kp, (((1,), (1,)), ((), ())), preferred_element_type=F32)
            s = s + bias_ref[h, :, off:off + part_rows[p]]
            if masked and p < np_ - 1:
                s = jnp.where(i < (np_ - 1 - p), NEG_INF, s)
            s_parts.append(s)
            off += part_rows[p]
        if len(set(part_rows)) == 1:
            m = functools.reduce(jnp.maximum, s_parts).max(axis=-1, keepdims=True)
        else:
            m = functools.reduce(jnp.maximum, [s.max(axis=-1, keepdims=True) for s in s_parts])
        o = None
        for p in range(np_):
            e = jnp.exp2(s_parts[p] - m).astype(BF16)
            vp = head(v_refs[p], p, h)
            v_aug = jnp.concatenate([vp, jnp.ones(vp.shape, BF16)], axis=1)
            op = jnp.dot(e, v_aug, preferred_element_type=F32)
            o = op if o is None else o + op
        att = o[:, :HEAD_DIM] / o[:, HEAD_DIM:]
        mixed = mr_ref[:, hs].astype(F32) + _sigmoid(ga_ref[:, hs].astype(F32)) * att
        o_ref[:, hs] = mixed.astype(BF16)

    for h in range(n_heads):
        one_head(h, mask_parts)


def _attn_prompt(q, k, v, bias, ga, mr, *, batch, seq):
    n, d = q.shape
    gq = ATT_GROUP * CHUNK
    nparts = LEFT_CHUNKS // ATT_GROUP + 1
    ng = seq // gq
    cur = pl.BlockSpec((gq, d), lambda b, i: (b * ng + i, 0))

    def back(p):
        return pl.BlockSpec((gq, d), lambda b, i: (b * ng + jnp.maximum(i - p, 0), 0))

    kv_specs = [back(nparts - 1 - p) for p in range(nparts)]
    n_heads = d // HEAD_DIM
    return pl.pallas_call(
        functools.partial(_attn_kernel, part_rows=(gq,) * nparts, head_major=(False,) * nparts,
                          n_heads=n_heads, mask_parts=True),
        grid=(batch, ng),
        in_specs=[cur] + kv_specs + kv_specs
                 + [pl.BlockSpec(bias.shape, lambda b, i: (0, 0, 0), pipeline_mode=pl.Buffered(1)), cur, cur],
        out_specs=cur,
        out_shape=jax.ShapeDtypeStruct((n, d), BF16),
        compiler_params=_cparams(("arbitrary", "arbitrary")),
        name="attn_prompt",
    )(q, *([k] * nparts), *([v] * nparts), bias, ga, mr)


def _attn_sample(q, k, v, ck, cv, bias, ga, mr, *, batch, seq):
    n, d = q.shape
    n_heads = d // HEAD_DIM
    nc = ck.shape[0] // (batch * n_heads)
    cur = pl.BlockSpec((seq, d), lambda b, i: (b, 0))
    cache = pl.BlockSpec((nc * n_heads, HEAD_DIM), lambda b, i: (b, 0))
    return pl.pallas_call(
        functools.partial(_attn_kernel, part_rows=(nc, seq), head_major=(True, False),
                          n_heads=n_heads, mask_parts=False),
        grid=(batch, 1),
        in_specs=[cur, cache, cur, cache, cur,
                  pl.BlockSpec(bias.shape, lambda b, i: (0, 0, 0), pipeline_mode=pl.Buffered(1)), cur, cur],
        out_specs=cur,
        out_shape=jax.ShapeDtypeStruct((n, d), BF16),
        compiler_params=_cparams(("arbitrary", "arbitrary")),
        name="attn_sample",
    )(q, ck, k, cv, v, bias, ga, mr)


def _store_token_tiles(ref, x):
    rows = x.shape[0]
    for c in range(x.shape[1] // LANES):
        ref[pl.ds(c, rows, stride=SUBLANES), :] = x[:, c * LANES:(c + 1) * LANES]


def _load_token_tiles(ref, rows):
    return jnp.concatenate([ref[pl.ds(c, rows, stride=SUBLANES), :] for c in range(SUBLANES)], axis=1)


def _split_bf16(x):
    hi = x.astype(BF16)
    lo = (x - hi.astype(F32)).astype(BF16)
    return hi, lo


def _outproj_kernel(mp_ref, xp_ref, ms_ref, xs_ref, w_ref, g_ref, rwt_ref, rb_ref,
                    x2_ref, hn_ref, lg_ref, *, ntp):
    def run(mixed_ref, x_ref):
        x2 = x_ref[...] + jnp.dot(mixed_ref[...], w_ref[...], preferred_element_type=F32)
        x2_ref[...] = x2
        hn = _rms(x2, g_ref[...])
        _store_token_tiles(hn_ref, hn)
        h_hi, h_lo = _split_bf16(hn)
        w_hi, w_lo = _split_bf16(rwt_ref[...])
        nt = (((1,), (1,)), ((), ()))
        lg = lax.dot_general(w_hi, h_hi, nt, preferred_element_type=F32)
        lg = lg + lax.dot_general(w_hi, h_lo, nt, preferred_element_type=F32)
        lg = lg + lax.dot_general(w_lo, h_hi, nt, preferred_element_type=F32)
        lg_ref[...] = lg + rb_ref[...]

    i = pl.program_id(0)

    @pl.when(i < ntp)
    def _():
        run(mp_ref, xp_ref)

    @pl.when(i >= ntp)
    def _():
        run(ms_ref, xs_ref)


def _outproj(mixed_p, xp2d, mixed_s, xs2d, w_bf, g, rwt, rb, *, tm):
    tp, d = xp2d.shape
    tsm = xs2d.shape[0]
    ne = rwt.shape[0]
    ntp = tp // tm
    nts = tsm // tm
    total_rows = tp + tsm
    prow = pl.BlockSpec((tm, d), lambda i: (jnp.minimum(i, ntp - 1), 0))
    srow = pl.BlockSpec((tm, d), lambda i: (jnp.maximum(i - ntp, 0), 0))
    full = lambda shape: pl.BlockSpec(shape, lambda i: (0,) * len(shape))
    return pl.pallas_call(
        functools.partial(_outproj_kernel, ntp=ntp),
        grid=(ntp + nts,),
        in_specs=[prow, prow, srow, srow, full((d, d)), full((1, d)), full((ne, d)), full((ne, 1))],
        out_specs=[pl.BlockSpec((tm, d), lambda i: (i, 0)),
                   pl.BlockSpec((tm * SUBLANES, LANES), lambda i: (i, 0)),
                   pl.BlockSpec((ne, tm), lambda i: (0, i))],
        out_shape=[jax.ShapeDtypeStruct((total_rows, d), F32),
                   jax.ShapeDtypeStruct((total_rows * SUBLANES, LANES), F32),
                   jax.ShapeDtypeStruct((ne, total_rows), F32)],
        compiler_params=_cparams(("arbitrary",)),
        name="outproj",
    )(mixed_p, xp2d, mixed_s, xs2d, w_bf, g, rwt, rb)


def _route_kernel(lg_ref, e_ref, r_ref, g_ref, cnt_ref, carry_s, *, tr):
    @pl.when(pl.program_id(0) == 0)
    def _():
        carry_s[...] = jnp.zeros(carry_s.shape, F32)

    work = lg_ref[...]
    ne = work.shape[0]
    eid = lax.broadcasted_iota(I32, (ne, tr), 0).astype(F32)
    sels, vals, idxs = [], [], []
    for _ in range(TOP_K):
        m = work.max(axis=0, keepdims=True)
        idx = jnp.where(work == m, eid, float(ne)).min(axis=0, keepdims=True)
        sel = eid == idx
        sels.append(sel)
        vals.append(m)
        idxs.append(idx)
        work = jnp.where(sel, -jnp.inf, work)
    ex = [jnp.exp(v - vals[0]) for v in vals]
    den = ex[0] + ex[1] + ex[2] + ex[3]
    chosen = jnp.zeros((ne, tr), F32)
    for sel in sels:
        chosen = chosen + sel.astype(F32)
    rr = lax.broadcasted_iota(I32, (tr, tr), 0)
    cc = lax.broadcasted_iota(I32, (tr, tr), 1)
    upper = jnp.where(rr < cc, 1.0, 0.0).astype(BF16)
    before = jnp.dot(chosen.astype(BF16), upper, preferred_element_type=F32)
    carry = carry_s[:, 0:1]
    rank_all = before + carry
    for k in range(TOP_K):
        e_ref[k:k + 1, :] = idxs[k].astype(I32)
        g_ref[k:k + 1, :] = ex[k] / den
        r_ref[k:k + 1, :] = jnp.where(sels[k], rank_all, 0.0).sum(axis=0, keepdims=True).astype(I32)
    new_carry = carry + chosen.sum(axis=1, keepdims=True)
    carry_s[...] = jnp.broadcast_to(new_carry, carry_s.shape)
    cnt_ref[...] = jnp.broadcast_to(new_carry, cnt_ref.shape).astype(I32)


def _route(lgt, *, tr):
    ne, t = lgt.shape
    blk = lambda rows: pl.BlockSpec((rows, tr), lambda i: (0, i))
    return pl.pallas_call(
        functools.partial(_route_kernel, tr=tr),
        grid=(t // tr,),
        in_specs=[blk(ne)],
        out_specs=[blk(TOP_K), blk(TOP_K), blk(TOP_K), pl.BlockSpec((ne, LANES), lambda i: (0, 0))],
        out_shape=[jax.ShapeDtypeStruct((TOP_K, t), I32),
                   jax.ShapeDtypeStruct((TOP_K, t), I32),
                   jax.ShapeDtypeStruct((TOP_K, t), F32),
                   jax.ShapeDtypeStruct((ne, LANES), I32)],
        scratch_shapes=[pltpu.VMEM((ne, LANES), F32)],
        compiler_params=_cparams(("arbitrary",)),
        name="route",
    )(lgt)


def _plan_kernel(cnt_ref, meta_ref, be_ref, *, ne, nblocks, blk):
    acc = jnp.int32(0)
    for e in range(ne):
        c = cnt_ref[e, 0]
        pc = ((c + (blk - 1)) // blk) * blk
        meta_ref[e] = acc
        b0 = acc // blk
        acc = acc + pc
        meta_ref[ne + e] = acc
        b1 = acc // blk

        def fill(b, carry, e=e):
            be_ref[b] = jnp.int32(e)
            return carry

        lax.fori_loop(b0, b1, fill, 0)
    used = acc // blk
    meta_ref[2 * ne] = used

    def fill_rest(b, carry):
        be_ref[b] = jnp.int32(ne - 1)
        return carry

    lax.fori_loop(used, nblocks, fill_rest, 0)
    for j in range(2 * ne + 1, meta_ref.shape[0]):
        meta_ref[j] = jnp.int32(0)


def _plan(counts, *, nblocks, blk):
    ne = counts.shape[0]
    smem = pl.BlockSpec(memory_space=pltpu.SMEM)
    return pl.pallas_call(
        functools.partial(_plan_kernel, ne=ne, nblocks=nblocks, blk=blk),
        in_specs=[smem],
        out_specs=[smem, smem],
        out_shape=[jax.ShapeDtypeStruct((LANES,), I32), jax.ShapeDtypeStruct((nblocks,), I32)],
        name="plan",
    )(counts)


def _slots_kernel(e_ref, r_ref, cnt_ref, d_ref, *, blk):
    ne = cnt_ref.shape[0]
    tr = e_ref.shape[1]
    padded = jnp.bitwise_and(cnt_ref[:, 0:1] + (blk - 1), -blk).astype(F32)
    eid1 = lax.broadcasted_iota(I32, (ne, 1), 0)
    start = jnp.zeros((ne, 1), F32)
    for e in range(ne - 1):
        start = start + jnp.where(eid1 > e, padded[e:e + 1, :], 0.0)
    eid = lax.broadcasted_iota(I32, (ne, tr), 0)
    for k in range(TOP_K):
        hit = eid == e_ref[k:k + 1, :]
        base = jnp.where(hit, start, 0.0).sum(axis=0, keepdims=True)
        d_ref[k:k + 1, :] = base.astype(I32) + r_ref[k:k + 1, :]


def _slots(e_all, r_all, counts, *, blk, tr):
    t = e_all.shape[1]
    blk_spec = pl.BlockSpec((TOP_K, tr), lambda i: (0, i))
    return pl.pallas_call(
        functools.partial(_slots_kernel, blk=blk),
        grid=(t // tr,),
        in_specs=[blk_spec, blk_spec, pl.BlockSpec(counts.shape, lambda i: (0, 0))],
        out_specs=blk_spec,
        out_shape=jax.ShapeDtypeStruct((TOP_K, t), I32),
        compiler_params=_cparams(("arbitrary",)),
        name="slots",
    )(e_all, r_all, counts)


DISPATCH_RING = 3


def _dispatch_kernel(meta_ref, d_ref, hn_ref, xs_ref, zero_s, ring, zsem, in_sem, out_sem,
                     *, td, ne, blk, nblocks, nsteps):
    i = pl.program_id(0)
    rows = td * SUBLANES
    slot = i % DISPATCH_RING
    nxt = (i + 1) % DISPATCH_RING

    def load(step, s):
        src = hn_ref.at[pl.ds(pl.multiple_of(step * rows, rows), rows), :]
        return pltpu.make_async_copy(src, ring.at[s], in_sem.at[s])

    def wait_scatters(s):
        for _ in range(TOP_K):
            pltpu.make_async_copy(ring.at[s], xs_ref.at[pl.ds(0, rows), :], out_sem.at[s]).wait()

    @pl.when(i == 0)
    def _():
        zero_s[...] = jnp.zeros(zero_s.shape, F32)

        def zcopy(e):
            row0 = pl.multiple_of(jnp.maximum(meta_ref[ne + e] - blk, 0) * SUBLANES, SUBLANES)
            return pltpu.make_async_copy(zero_s, xs_ref.at[pl.ds(row0, blk * SUBLANES), :], zsem)

        for e in range(ne):
            zcopy(e).start()
        for e in range(ne):
            zcopy(e).wait()

        def ztail(b, carry):
            row0 = pl.multiple_of(b * (blk * SUBLANES), blk * SUBLANES)
            cp = pltpu.make_async_copy(zero_s, xs_ref.at[pl.ds(row0, blk * SUBLANES), :], zsem)
            cp.start()
            cp.wait()
            return carry

        lax.fori_loop(meta_ref[2 * ne], nblocks, ztail, 0)
        load(0, 0).start()

    @pl.when(i >= DISPATCH_RING - 1)
    def _():
        wait_scatters(nxt)

    @pl.when(i + 1 < nsteps)
    def _():
        load(i + 1, nxt).start()

    load(i, slot).wait()

    def issue(r, carry):
        src = ring.at[slot, pl.ds(pl.multiple_of(r * SUBLANES, SUBLANES), SUBLANES), :]
        for k in range(TOP_K):
            dst = xs_ref.at[pl.ds(pl.multiple_of(d_ref[r * TOP_K + k] * SUBLANES, SUBLANES), SUBLANES), :]
            pltpu.make_async_copy(src, dst, out_sem.at[slot]).start(priority=k % 2)
        return carry

    lax.fori_loop(0, td, issue, 0, unroll=2)

    @pl.when(i == nsteps - 1)
    def _():
        wait_scatters(slot)

        @pl.when(i >= 1)
        def _():
            wait_scatters((i + DISPATCH_RING - 1) % DISPATCH_RING)


def _dispatch(meta, dest, hn_all, *, n_slots, td, blk, ne):
    t = hn_all.shape[0] // SUBLANES
    nsteps = t // td
    assert nsteps >= DISPATCH_RING - 1
    grid_spec = pltpu.PrefetchScalarGridSpec(
        num_scalar_prefetch=1,
        grid=(nsteps,),
        in_specs=[pl.BlockSpec((td * TOP_K,), lambda i, m: (i,), memory_space=pltpu.SMEM),
                  pl.BlockSpec(memory_space=pl.ANY)],
        out_specs=pl.BlockSpec(memory_space=pl.ANY),
        scratch_shapes=[pltpu.VMEM((blk * SUBLANES, LANES), F32),
                        pltpu.VMEM((DISPATCH_RING, td * SUBLANES, LANES), F32),
                        pltpu.SemaphoreType.DMA(()),
                        pltpu.SemaphoreType.DMA((DISPATCH_RING,)),
                        pltpu.SemaphoreType.DMA((DISPATCH_RING,))],
    )
    return pl.pallas_call(
        functools.partial(_dispatch_kernel, td=td, ne=ne, blk=blk, nblocks=n_slots // blk, nsteps=nsteps),
        grid_spec=grid_spec,
        out_shape=jax.ShapeDtypeStruct((n_slots * SUBLANES, LANES), F32),
        compiler_params=_cparams(("arbitrary",)),
        name="dispatch",
    )(meta, dest, hn_all)


def _experts_kernel(be_ref, meta_ref, x_ref, w1_ref, b1g_ref, b1l_ref, w2_ref, b2_ref,
                    y_ref, w1_s, w2_s, *, ne, blk):
    b = pl.program_id(0)
    used = meta_ref[2 * ne]
    half = MXU_COLS // 2
    ngroups = w1_s.shape[1] // MXU_COLS
    fresh = jnp.logical_or(b == 0, be_ref[b] != be_ref[jnp.maximum(b - 1, 0)])

    @pl.when(jnp.logical_and(fresh, b < used))
    def _():
        src = lax.broadcasted_iota(I32, (MXU_COLS, MXU_COLS), 0)
        dst = lax.broadcasted_iota(I32, (MXU_COLS, MXU_COLS), 1)
        want = jnp.where(dst < half, 2 * dst, 2 * (dst - half) + 1)
        perm = jnp.where(src == want, 1.0, 0.0).astype(BF16)
        for t in range(ngroups):
            cols = slice(t * MXU_COLS, (t + 1) * MXU_COLS)
            w = w1_ref[0, :, cols].astype(BF16)
            w1_s[:, cols] = jnp.dot(w, perm, preferred_element_type=F32).astype(BF16)
        w2_s[...] = w2_ref[0].astype(BF16)

    @pl.when(b < used)
    def _():
        xb = _load_token_tiles(x_ref, blk).astype(BF16)
        acts = []
        for t in range(ngroups):
            h = jnp.dot(xb, w1_s[:, t * MXU_COLS:(t + 1) * MXU_COLS], preferred_element_type=F32)
            hg = h[:, :half] + b1g_ref[0, :, t * half:(t + 1) * half]
            hl = h[:, half:] + b1l_ref[0, :, t * half:(t + 1) * half]
            g = jnp.minimum(hg, SWIGLU_LIMIT)
            lin = jnp.clip(hl, -SWIGLU_LIMIT, SWIGLU_LIMIT)
            acts.append((g * _sigmoid(SWIGLU_ALPHA * g) * (lin + 1.0)).astype(BF16))
        act = jnp.concatenate(acts, axis=1)
        y = jnp.dot(act, w2_s[...], preferred_element_type=F32) + b2_ref[0]
        _store_token_tiles(y_ref, y)

    @pl.when(b >= used)
    def _():
        y_ref[...] = jnp.zeros(y_ref.shape, F32)


def _experts(be, meta, xs, w1, b1g, b1l, w2, b2, *, blk):
    ne, d, dff2 = w1.shape
    dff = dff2 // 2
    n_slots = xs.shape[0] // SUBLANES
    nblocks = n_slots // blk

    def xmap(b, be_r, meta_r):
        return (jnp.minimum(b, meta_r[2 * ne] - 1), 0)

    def wmap(b, be_r, meta_r):
        return (be_r[b], 0, 0)

    grid_spec = pltpu.PrefetchScalarGridSpec(
        num_scalar_prefetch=2,
        grid=(nblocks,),
        in_specs=[pl.BlockSpec((blk * SUBLANES, LANES), xmap),
                  pl.BlockSpec((1, d, dff2), wmap),
                  pl.BlockSpec((1, 1, dff), wmap),
                  pl.BlockSpec((1, 1, dff), wmap),
                  pl.BlockSpec((1, dff, d), wmap),
                  pl.BlockSpec((1, 1, d), wmap)],
        out_specs=pl.BlockSpec((blk * SUBLANES, LANES), lambda b, be_r, meta_r: (b, 0)),
        scratch_shapes=[pltpu.VMEM((d, dff2), BF16), pltpu.VMEM((dff, d), BF16)],
    )
    return pl.pallas_call(
        functools.partial(_experts_kernel, ne=ne, blk=blk),
        grid_spec=grid_spec,
        out_shape=jax.ShapeDtypeStruct((n_slots * SUBLANES, LANES), F32),
        compiler_params=_cparams(("arbitrary",)),
        name="experts",
    )(be, meta, xs, w1, b1g, b1l, w2, b2)


def _combine_kernel(d_ref, dn_ref, x2_ref, gt_ref, gn_ref, ys_ref, y_ref, buf, sem, *, tc, nsteps):
    i = pl.program_id(0)
    slot = i % 2
    rows = tc * SUBLANES

    def gather(idx_ref, s):
        def body(r, carry):
            for k in range(TOP_K):
                src = ys_ref.at[pl.ds(pl.multiple_of(idx_ref[r * TOP_K + k] * SUBLANES, SUBLANES), SUBLANES), :]
                dst = buf.at[s, k, pl.ds(pl.multiple_of(r * SUBLANES, SUBLANES), SUBLANES), :]
                pltpu.make_async_copy(src, dst, sem.at[s]).start(priority=k % 2)
            return carry

        lax.fori_loop(0, tc, body, 0, unroll=4)

    @pl.when(i == 0)
    def _():
        gather(d_ref, 0)

    @pl.when(i + 1 < nsteps)
    def _():
        gather(dn_ref, 1 - slot)

    for k in range(TOP_K):
        pltpu.make_async_copy(ys_ref.at[pl.ds(0, rows), :], buf.at[slot, k], sem.at[slot]).wait()
    acc = x2_ref[...]
    for k in range(TOP_K):
        acc = acc + gt_ref[:, k:k + 1] * _load_token_tiles(buf.at[slot, k], tc)
    y_ref[...] = _rms(acc, gn_ref[...])


def _combine(dest, x2_all, gt_all, gn, ys, *, row_off, rows, tc):
    _, d = x2_all.shape
    boff = row_off // tc
    nsteps = rows // tc
    last = boff + nsteps - 1
    smem_blk = lambda fn: pl.BlockSpec((tc * TOP_K,), fn, memory_space=pltpu.SMEM)
    return pl.pallas_call(
        functools.partial(_combine_kernel, tc=tc, nsteps=nsteps),
        grid=(nsteps,),
        in_specs=[smem_blk(lambda i: (i + boff,)),
                  smem_blk(lambda i: (jnp.minimum(i + boff + 1, last),)),
                  pl.BlockSpec((tc, d), lambda i: (i + boff, 0)),
                  pl.BlockSpec((tc, TOP_K), lambda i: (i + boff, 0)),
                  pl.BlockSpec((1, d), lambda i: (0, 0)),
                  pl.BlockSpec(memory_space=pl.ANY)],
        out_specs=pl.BlockSpec((tc, d), lambda i: (i, 0)),
        out_shape=jax.ShapeDtypeStruct((rows, d), F32),
        scratch_shapes=[pltpu.VMEM((2, TOP_K, tc * SUBLANES, LANES), F32), pltpu.SemaphoreType.DMA((2,))],
        compiler_params=_cparams(("arbitrary",)),
        name="combine",
    )(dest, dest, x2_all, gt_all, gn, ys)


def _pick_tile(n, pref):
    t = min(pref, n)
    while n % t:
        t //= 2
    return t


def kernel(x_prompt, x_sample, cache_k, cache_v, state_conv, state_lru, norm_mix, w_in, conv_w, conv_b,
           lru_wa, lru_ba, lru_wx, lru_bx, lru_lambda, rel_bias, w_out, norm_ffn, router_w, router_b,
           w1, b1, w2, b2, norm_out):
    depth = w_in.shape[0]
    assert depth == 1, "single-layer step"
    bp, sp, d = x_prompt.shape
    bs, ss, _ = x_sample.shape
    n_heads = d // HEAD_DIM
    band = LEFT_CHUNKS * CHUNK
    n_keep = min(band, sp)
    n_cache = cache_k.shape[2]
    ne = router_w.shape[-1]
    dff = w2.shape[2]
    l = 0
    assert PAST_LEN % CHUNK == 0 and n_cache == band and ss == CHUNK
    assert d == SUBLANES * LANES, "token-tile layout holds one token per (8,128) tile"

    w_in_bf = w_in[l].astype(BF16)
    w_out_bf = w_out[l].astype(BF16)
    g_mix = norm_mix[l].reshape(1, d)
    g_ffn = norm_ffn[l].reshape(1, d)
    g_out = norm_out.reshape(1, d)
    wg = jnp.concatenate([lru_wa[l], lru_wx[l]], axis=-1).astype(BF16)
    ba = lru_ba[l].reshape(1, d)
    bx = lru_bx[l].reshape(1, d)
    lam = lru_lambda[l].reshape(1, d)
    cw = conv_w[l]
    cb = conv_b[l].reshape(1, d)
    rwt = router_w[l].T
    rb = router_b[l].reshape(ne, 1)
    b1g = b1[l][:, 0::2].reshape(ne, 1, dff)
    b1l = b1[l][:, 1::2].reshape(ne, 1, dff)
    b2r = b2[l].reshape(ne, 1, d)

    tp = bp * sp
    tsm = bs * ss
    t_all = tp + tsm

    gq = ATT_GROUP * CHUNK
    bias_p = _bias_table(rel_bias[l], gq, band + gq)
    bias_s = _bias_table(rel_bias[l], ss, n_cache + ss)

    xp2d = x_prompt.reshape(tp, d)
    conv0 = jnp.zeros((bp, CONV_W - 1, d), F32)
    lru0 = jnp.zeros((bp, 1, d), F32)
    qp, kp, vp, ga_p, mr_p, ktp, vtp, cnew_p, hlast_p = _mixer_prompt(
        xp2d, g_mix, w_in_bf, conv0, lru0, cw, cb, wg, ba, bx, lam,
        batch=bp, seq=sp, n_keep=n_keep, tm=_pick_tile(sp, ROW_TILE))
    mixed_p = _attn_prompt(qp, kp, vp, bias_p, ga_p, mr_p, batch=bp, seq=sp)

    xs2d = x_sample.reshape(tsm, d)
    xr, q, k, v, gr, ga_s, kts, vts = _inproj(xs2d, g_mix, w_in_bf, tm=_pick_tile(tsm, ROW_TILE))
    mr_s, cnew_s, hlast_s = _lru(xr, gr, state_conv[l], state_lru[l].reshape(bs, 1, d), cw, cb, wg, ba, bx, lam,
                                 batch=bs, seq=ss, ts=ss, starts_at_zero=False)
    ck = cache_k[l].reshape(bs * n_cache * n_heads, HEAD_DIM)
    cv = cache_v[l].reshape(bs * n_cache * n_heads, HEAD_DIM)
    mixed_s = _attn_sample(q, k, v, ck, cv, bias_s, ga_s, mr_s, batch=bs, seq=ss)

    tmo = _pick_tile(tsm, ROW_TILE)
    x2_all, hn_all, lgt_all = _outproj(mixed_p, xp2d, mixed_s, xs2d, w_out_bf, g_ffn, rwt, rb, tm=tmo)

    tr = _pick_tile(t_all, ROUTE_TILE)
    e_all, r_all, gates, counts = _route(lgt_all, tr=tr)
    blk = MOE_BLOCK
    nblocks = -(-(t_all * TOP_K) // blk) + ne
    meta, be = _plan(counts, nblocks=nblocks, blk=blk)
    dest = _slots(e_all, r_all, counts, blk=blk, tr=tr).T.reshape(t_all * TOP_K)
    td = _pick_tile(t_all, 512)
    xs = _dispatch(meta, dest, hn_all, n_slots=nblocks * blk, td=td, blk=blk, ne=ne)
    ys = _experts(be, meta, xs, w1[l], b1g, b1l, w2[l], b2r, blk=blk)
    gt_all = gates.T
    tc = _pick_tile(tsm, 256)
    y_p = _combine(dest, x2_all, gt_all, g_out, ys, row_off=0, rows=tp, tc=tc)
    y_s = _combine(dest, x2_all, gt_all, g_out, ys, row_off=tp, rows=tsm, tc=tc)

    return (y_p.reshape(bp, sp, d), y_s.reshape(bs, ss, d),
            ktp.reshape(1, bp, n_keep, n_heads, HEAD_DIM), vtp.reshape(1, bp, n_keep, n_heads, HEAD_DIM),
            cnew_p[None], hlast_p.reshape(1, bp, d),
            kts.reshape(1, bs, ss, n_heads, HEAD_DIM), vts.reshape(1, bs, ss, n_heads, HEAD_DIM),
            cnew_s[None], hlast_s.reshape(1, bs, d))
```

```python
import functools

import jax
import jax.numpy as jnp
from jax import lax
from jax.experimental import pallas as pl
from jax.experimental.pallas import tpu as pltpu

F32 = jnp.float32
BF16 = jnp.bfloat16
I32 = jnp.int32

CHUNK = 64
LEFT_CHUNKS = 8
HEAD_DIM = 128
MAX_REL = 256
NEG_INF = -1e30
LRU_BLOCKS = 8
CONV_W = 4
LRU_C = 8.0
TOP_K = 4
SWIGLU_ALPHA = 1.702
SWIGLU_LIMIT = 7.0
RMS_EPS = 1e-6
PAST_LEN = 1024
LOG2_E = 1.4426950408889634
QK_SCALE = HEAD_DIM ** -0.5 * LOG2_E

LANES = 128
SUBLANES = 8
MXU_COLS = 256
VMEM_LIMIT = 56 * 1024 * 1024
MOE_BLOCK = 512
ROUTE_TILE = 1024
ROW_TILE = 512
ATT_GROUP = 4


def _cparams(sem):
    return pltpu.CompilerParams(dimension_semantics=sem, vmem_limit_bytes=VMEM_LIMIT)


def _sigmoid(x):
    return 0.5 * jnp.tanh(0.5 * x) + 0.5


def _rms(x, g):
    ms = jnp.mean(x * x, axis=-1, keepdims=True)
    return x * lax.rsqrt(ms + RMS_EPS) * g


def _inproj_kernel(x_ref, g_ref, w_ref, xr_ref, q_ref, k_ref, v_ref, gr_ref, ga_ref,
                   kt_ref, vt_ref, *, d):
    u = _rms(x_ref[...], g_ref[...]).astype(BF16)

    def piece(j):
        return jnp.dot(u, w_ref[:, j * d:(j + 1) * d], preferred_element_type=F32)

    xr_ref[...] = piece(0)
    q_ref[...] = (piece(1) * QK_SCALE).astype(BF16)
    kf = piece(2)
    k_ref[...] = kf.astype(BF16)
    kt_ref[...] = kf
    vf = piece(3)
    v_ref[...] = vf.astype(BF16)
    vt_ref[...] = vf
    gr_ref[...] = piece(4).astype(BF16)
    ga_ref[...] = _sigmoid(piece(5)).astype(BF16)


def _inproj(x2d, g, w_bf, *, tm):
    n, d = x2d.shape
    row = pl.BlockSpec((tm, d), lambda i: (i, 0))
    bf = jax.ShapeDtypeStruct((n, d), BF16)
    f32 = jax.ShapeDtypeStruct((n, d), F32)
    return pl.pallas_call(
        functools.partial(_inproj_kernel, d=d),
        grid=(n // tm,),
        in_specs=[row,
                  pl.BlockSpec((1, d), lambda i: (0, 0)),
                  pl.BlockSpec((d, 6 * d), lambda i: (0, 0), pipeline_mode=pl.Buffered(1))],
        out_specs=[row] * 8,
        out_shape=[f32, bf, bf, bf, bf, bf, f32, f32],
        compiler_params=_cparams(("arbitrary",)),
        name="inproj",
    )(x2d, g, w_bf)


def _lru_kernel(xr_ref, gr_ref, cs_ref, h0_ref, cw_ref, cb_ref, wg_ref, ba_ref, bx_ref, lam_ref,
                mr_ref, cnew_ref, hlast_ref, xp_s, a_s, b_s, hs_s, h_s, *, ts, starts_at_zero):
    _lru_init(pl.program_id(1), cs_ref, h0_ref, xp_s, h_s)
    _lru_steps(xr_ref[...], gr_ref[...].astype(F32), pl.program_id(1),
               cw_ref, cb_ref, wg_ref, ba_ref, bx_ref, lam_ref,
               mr_ref, cnew_ref, hlast_ref, xp_s, a_s, b_s, hs_s, h_s,
               ts=ts, starts_at_zero=starts_at_zero)


def _lru_init(t, cs_ref, h0_ref, xp_s, h_s):
    head = SUBLANES
    hist = CONV_W - 1

    @pl.when(t == 0)
    def _():
        xp_s[0:head, :] = jnp.zeros((head, xp_s.shape[1]), F32)
        xp_s[head - hist:head, :] = cs_ref[0]
        h_s[...] = jnp.broadcast_to(h0_ref[0], h_s.shape)


def _lru_steps(xr, gr, t, cw_ref, cb_ref, wg_ref, ba_ref, bx_ref, lam_ref,
               mr_ref, cnew_ref, hlast_ref, xp_s, a_s, b_s, hs_s, h_s, *, ts, starts_at_zero):
    head = SUBLANES
    hist = CONV_W - 1
    xp_s[head:head + ts, :] = xr
    cw = cw_ref[...]
    xc = cb_ref[...] + xp_s[head - hist:head - hist + ts, :] * cw[0:1, :]
    for j in range(1, CONV_W):
        xc = xc + xp_s[head - hist + j:head - hist + j + ts, :] * cw[j:j + 1, :]
    tail = xp_s[ts + head - hist:ts + head, :]
    xp_s[head - hist:head, :] = tail
    cnew_ref[0] = tail

    lam = lam_ref[...]
    z = -lam
    softplus = jnp.maximum(z, 0.0) + jnp.log1p(jnp.exp(-jnp.abs(z)))
    c = -LRU_C * softplus
    bw = xc.shape[1] // LRU_BLOCKS
    if starts_at_zero:
        first = (lax.broadcasted_iota(I32, (ts, bw), 0) + t * ts) == 0
    for g in range(LRU_BLOCKS):
        sl = slice(g * bw, (g + 1) * bw)
        xg = xc[:, sl]
        pre = jnp.dot(xg.astype(BF16), wg_ref[g], preferred_element_type=F32)
        r = _sigmoid(pre[:, :bw] + ba_ref[:, sl])
        ig = _sigmoid(pre[:, bw:] + bx_ref[:, sl])
        log_a = c[:, sl] * r
        a = jnp.exp(log_a)
        mult = jnp.sqrt(1.0 - a * a)
        if starts_at_zero:
            mult = jnp.where(first, 1.0, mult)
        a_s[:, sl] = a
        b_s[:, sl] = mult * (ig * xg)

    sub = lax.broadcasted_iota(I32, (SUBLANES, xc.shape[1]), 0)

    def scan_body(i, h):
        base = pl.multiple_of(i * SUBLANES, SUBLANES)
        a = a_s[pl.ds(base, SUBLANES), :]
        b = b_s[pl.ds(base, SUBLANES), :]
        for shift in (1, 2, 4):
            a_prev = jnp.where(sub < shift, 1.0, pltpu.roll(a, shift, 0))
            b_prev = jnp.where(sub < shift, 0.0, pltpu.roll(b, shift, 0))
            b = a * b_prev + b
            a = a * a_prev
        rows = a * h + b
        hs_s[pl.ds(base, SUBLANES), :] = rows
        return jnp.broadcast_to(rows[SUBLANES - 1:SUBLANES, :], rows.shape)

    h = lax.fori_loop(0, ts // SUBLANES, scan_body, h_s[...])
    h_s[...] = h
    hlast_ref[0] = h[0:1, :]
    mr_ref[...] = (_sigmoid(gr) * hs_s[...]).astype(BF16)


def _lru(xr, gr, conv_state, h0, conv_w, conv_b, wg, ba, bx, lam, *, batch, seq, ts, starts_at_zero):
    n, d = xr.shape
    ntb = seq // ts
    row = pl.BlockSpec((ts, d), lambda b, t: (b * ntb + t, 0))
    vec = pl.BlockSpec((1, d), lambda b, t: (0, 0))
    return pl.pallas_call(
        functools.partial(_lru_kernel, ts=ts, starts_at_zero=starts_at_zero),
        grid=(batch, ntb),
        in_specs=[row, row,
                  pl.BlockSpec((1, CONV_W - 1, d), lambda b, t: (b, 0, 0)),
                  pl.BlockSpec((1, 1, d), lambda b, t: (b, 0, 0)),
                  pl.BlockSpec((CONV_W, d), lambda b, t: (0, 0)),
                  vec,
                  pl.BlockSpec(wg.shape, lambda b, t: (0, 0, 0)),
                  vec, vec, vec],
        out_specs=[row,
                   pl.BlockSpec((1, CONV_W - 1, d), lambda b, t: (b, 0, 0)),
                   pl.BlockSpec((1, 1, d), lambda b, t: (b, 0, 0))],
        out_shape=[jax.ShapeDtypeStruct((n, d), BF16),
                   jax.ShapeDtypeStruct((batch, CONV_W - 1, d), F32),
                   jax.ShapeDtypeStruct((batch, 1, d), F32)],
        scratch_shapes=_lru_scratch(ts, d),
        compiler_params=_cparams(("arbitrary", "arbitrary")),
        name="lru",
    )(xr, gr, conv_state, h0, conv_w, conv_b, wg, ba, bx, lam)


def _lru_scratch(ts, d):
    return [pltpu.VMEM((ts + SUBLANES, d), F32),
            pltpu.VMEM((ts, d), F32),
            pltpu.VMEM((ts, d), F32),
            pltpu.VMEM((ts, d), F32),
            pltpu.VMEM((SUBLANES, d), F32)]


def _mixer_kernel(x_ref, g_ref, w_ref, cs_ref, h0_ref, cw_ref, cb_ref, wg_ref, ba_ref, bx_ref,
                  lam_ref, q_ref, k_ref, v_ref, ga_ref, mr_ref, kt_ref, vt_ref, cnew_ref, hlast_ref,
                  xp_s, a_s, b_s, hs_s, h_s, *, d, ntb):
    t = pl.program_id(0) % ntb
    _lru_init(t, cs_ref, h0_ref, xp_s, h_s)
    u = _rms(x_ref[...], g_ref[...]).astype(BF16)

    def piece(j):
        return jnp.dot(u, w_ref[:, j * d:(j + 1) * d], preferred_element_type=F32)

    q_ref[...] = (piece(1) * QK_SCALE).astype(BF16)
    kf = piece(2)
    k_ref[...] = kf.astype(BF16)
    kt_ref[...] = kf
    vf = piece(3)
    v_ref[...] = vf.astype(BF16)
    vt_ref[...] = vf
    ga_ref[...] = _sigmoid(piece(5)).astype(BF16)
    _lru_steps(piece(0), piece(4), t,
               cw_ref, cb_ref, wg_ref, ba_ref, bx_ref, lam_ref,
               mr_ref, cnew_ref, hlast_ref, xp_s, a_s, b_s, hs_s, h_s,
               ts=x_ref.shape[0], starts_at_zero=True)


def _mixer_prompt(x2d, g, w_bf, conv_state, h0, conv_w, conv_b, wg, ba, bx, lam, *, batch, seq, n_keep, tm):
    n, d = x2d.shape
    ntb = seq // tm
    nk = n_keep // tm
    tail_map = lambda i: ((i // ntb) * nk + jnp.maximum(i % ntb - (ntb - nk), 0), 0)
    row = pl.BlockSpec((tm, d), lambda i: (i, 0))
    vec = pl.BlockSpec((1, d), lambda i: (0, 0))
    per_batch = lambda rows: pl.BlockSpec((1, rows, d), lambda i: (i // ntb, 0, 0))
    bf = jax.ShapeDtypeStruct((n, d), BF16)
    tail = jax.ShapeDtypeStruct((batch * n_keep, d), F32)
    return pl.pallas_call(
        functools.partial(_mixer_kernel, d=d, ntb=ntb),
        grid=(n // tm,),
        in_specs=[row, vec,
                  pl.BlockSpec((d, 6 * d), lambda i: (0, 0), pipeline_mode=pl.Buffered(1)),
                  per_batch(CONV_W - 1), per_batch(1),
                  pl.BlockSpec((CONV_W, d), lambda i: (0, 0)), vec,
                  pl.BlockSpec(wg.shape, lambda i: (0, 0, 0)), vec, vec, vec],
        out_specs=[row, row, row, row, row,
                   pl.BlockSpec((tm, d), tail_map), pl.BlockSpec((tm, d), tail_map),
                   per_batch(CONV_W - 1), per_batch(1)],
        out_shape=[bf, bf, bf, bf, bf, tail, tail,
                   jax.ShapeDtypeStruct((batch, CONV_W - 1, d), F32),
                   jax.ShapeDtypeStruct((batch, 1, d), F32)],
        scratch_shapes=_lru_scratch(tm, d),
        compiler_params=_cparams(("arbitrary",)),
        name="mixer",
    )(x2d, g, w_bf, conv_state, h0, conv_w, conv_b, wg, ba, bx, lam)


def _bias_kernel(f_ref, o_ref, *, nq, nkeys, width):
    f = f_ref[0]
    x = jnp.broadcast_to(f, (nq, width))
    rolled = pltpu.roll(x, width - (nq - 1), 1, stride=1, stride_axis=0)
    t = rolled[:, :nkeys]
    qc = lax.broadcasted_iota(I32, (nq, nkeys), 0) // CHUNK
    kc = lax.broadcasted_iota(I32, (nq, nkeys), 1) // CHUNK
    ok = jnp.logical_and(kc >= qc, kc <= qc + LEFT_CHUNKS)
    o_ref[0] = jnp.where(ok, t * LOG2_E, NEG_INF)


def _bias_table(rel_bias, nq, nkeys):
    nh = rel_bias.shape[0]
    band = LEFT_CHUNKS * CHUNK
    width = -(-(nkeys + nq) // LANES) * LANES
    left = band + (nq - 1) - MAX_REL
    right = max(width - left - (2 * MAX_REL + 1), 0)
    flipped = rel_bias[:, ::-1]
    f = jnp.pad(flipped, ((0, 0), (left, right)), mode="edge")[:, :width].reshape(nh, 1, width)
    return pl.pallas_call(
        functools.partial(_bias_kernel, nq=nq, nkeys=nkeys, width=width),
        grid=(nh,),
        in_specs=[pl.BlockSpec((1, 1, width), lambda h: (h, 0, 0))],
        out_specs=pl.BlockSpec((1, nq, nkeys), lambda h: (h, 0, 0)),
        out_shape=jax.ShapeDtypeStruct((nh, nq, nkeys), F32),
        compiler_params=_cparams(("arbitrary",)),
        name="bias",
    )(f)


def _attn_kernel(*refs, part_rows, head_major, n_heads, mask_parts):
    np_ = len(part_rows)
    q_ref = refs[0]
    k_refs = refs[1:1 + np_]
    v_refs = refs[1 + np_:1 + 2 * np_]
    bias_ref, ga_ref, mr_ref, o_ref = refs[1 + 2 * np_:]
    i = pl.program_id(1)

    def head(ref, p, h):
        if head_major[p]:
            return ref[pl.ds(h, part_rows[p], stride=n_heads), :].astype(BF16)
        return ref[:, h * HEAD_DIM:(h + 1) * HEAD_DIM].astype(BF16)

    def one_head(h, masked):
        hs = slice(h * HEAD_DIM, (h + 1) * HEAD_DIM)
        qh = q_ref[:, hs]
        s_parts = []
        off = 0
        for p in range(np_):
            kp = head(k_refs[p], p, h)
            s = lax.dot_general(qh, kp, (((1,), (1,)), ((), ())), preferred_element_type=F32)
            s = s + bias_ref[h, :, off:off + part_rows[p]]
            if masked and p < np_ - 1:
                s = jnp.where(i < (np_ - 1 - p), NEG_INF, s)
            s_parts.append(s)
            off += part_rows[p]
        if len(set(part_rows)) == 1:
            m = functools.reduce(jnp.maximum, s_parts).max(axis=-1, keepdims=True)
        else:
            m = functools.reduce(jnp.maximum, [s.max(axis=-1, keepdims=True) for s in s_parts])
        o = None
        for p in range(np_):
            e = jnp.exp2(s_parts[p] - m).astype(BF16)
            vp = head(v_refs[p], p, h)
            v_aug = jnp.concatenate([vp, jnp.ones(vp.shape, BF16)], axis=1)
            op = jnp.dot(e, v_aug, preferred_element_type=F32)
            o = op if o is None else o + op
        att = o[:, :HEAD_DIM] / o[:, HEAD_DIM:]
        mixed = mr_ref[:, hs].astype(F32) + ga_ref[:, hs].astype(F32) * att
        o_ref[:, hs] = mixed.astype(BF16)

    for h in range(n_heads):
        one_head(h, mask_parts)


def _attn_prompt(q, k, v, bias, ga, mr, *, batch, seq):
    n, d = q.shape
    gq = ATT_GROUP * CHUNK
    nparts = LEFT_CHUNKS // ATT_GROUP + 1
    ng = seq // gq
    cur = pl.BlockSpec((gq, d), lambda b, i: (b * ng + i, 0))

    def back(p):
        return pl.BlockSpec((gq, d), lambda b, i: (b * ng + jnp.maximum(i - p, 0), 0))

    kv_specs = [back(nparts - 1 - p) for p in range(nparts)]
    n_heads = d // HEAD_DIM
    return pl.pallas_call(
        functools.partial(_attn_kernel, part_rows=(gq,) * nparts, head_major=(False,) * nparts,
                          n_heads=n_heads, mask_parts=True),
        grid=(batch, ng),
        in_specs=[cur] + kv_specs + kv_specs
                 + [pl.BlockSpec(bias.shape, lambda b, i: (0, 0, 0), pipeline_mode=pl.Buffered(1)), cur, cur],
        out_specs=cur,
        out_shape=jax.ShapeDtypeStruct((n, d), BF16),
        compiler_params=_cparams(("arbitrary", "arbitrary")),
        name="attn_prompt",
    )(q, *([k] * nparts), *([v] * nparts), bias, ga, mr)


def _attn_sample(q, k, v, ck, cv, bias, ga, mr, *, batch, seq):
    n, d = q.shape
    n_heads = d // HEAD_DIM
    nc = ck.shape[0] // (batch * n_heads)
    cur = pl.BlockSpec((seq, d), lambda b, i: (b, 0))
    cache = pl.BlockSpec((nc * n_heads, HEAD_DIM), lambda b, i: (b, 0))
    return pl.pallas_call(
        functools.partial(_attn_kernel, part_rows=(nc, seq), head_major=(True, False),
                          n_heads=n_heads, mask_parts=False),
        grid=(batch, 1),
        in_specs=[cur, cache, cur, cache, cur,
                  pl.BlockSpec(bias.shape, lambda b, i: (0, 0, 0), pipeline_mode=pl.Buffered(1)), cur, cur],
        out_specs=cur,
        out_shape=jax.ShapeDtypeStruct((n, d), BF16),
        compiler_params=_cparams(("arbitrary", "arbitrary")),
        name="attn_sample",
    )(q, ck, k, cv, v, bias, ga, mr)


def _store_token_tiles(ref, x):
    rows = x.shape[0]
    for c in range(x.shape[1] // LANES):
        ref[pl.ds(c, rows, stride=SUBLANES), :] = x[:, c * LANES:(c + 1) * LANES]


def _load_token_tiles(ref, rows):
    return jnp.concatenate([ref[pl.ds(c, rows, stride=SUBLANES), :] for c in range(SUBLANES)], axis=1)


def _split_bf16(x):
    hi = x.astype(BF16)
    lo = (x - hi.astype(F32)).astype(BF16)
    return hi, lo


def _outproj_kernel(mp_ref, xp_ref, ms_ref, xs_ref, w_ref, g_ref, rwt_ref, rb_ref,
                    x2_ref, hn_ref, lg_ref, *, ntp):
    def run(mixed_ref, x_ref):
        x2 = x_ref[...] + jnp.dot(mixed_ref[...], w_ref[...], preferred_element_type=F32)
        x2_ref[...] = x2
        hn = _rms(x2, g_ref[...])
        _store_token_tiles(hn_ref, hn)
        h_hi, h_lo = _split_bf16(hn)
        w_hi, w_lo = _split_bf16(rwt_ref[...])
        nt = (((1,), (1,)), ((), ()))
        ne = w_hi.shape[0]
        both = lax.dot_general(jnp.concatenate([w_hi, w_lo], axis=0), h_hi, nt, preferred_element_type=F32)
        lg = both[:ne] + both[ne:] + lax.dot_general(w_hi, h_lo, nt, preferred_element_type=F32)
        lg_ref[...] = lg + rb_ref[...]

    i = pl.program_id(0)

    @pl.when(i < ntp)
    def _():
        run(mp_ref, xp_ref)

    @pl.when(i >= ntp)
    def _():
        run(ms_ref, xs_ref)


def _outproj(mixed_p, xp2d, mixed_s, xs2d, w_bf, g, rwt, rb, *, tm):
    tp, d = xp2d.shape
    tsm = xs2d.shape[0]
    ne = rwt.shape[0]
    ntp = tp // tm
    nts = tsm // tm
    total_rows = tp + tsm
    prow = pl.BlockSpec((tm, d), lambda i: (jnp.minimum(i, ntp - 1), 0))
    srow = pl.BlockSpec((tm, d), lambda i: (jnp.maximum(i - ntp, 0), 0))
    full = lambda shape: pl.BlockSpec(shape, lambda i: (0,) * len(shape))
    return pl.pallas_call(
        functools.partial(_outproj_kernel, ntp=ntp),
        grid=(ntp + nts,),
        in_specs=[prow, prow, srow, srow, full((d, d)), full((1, d)), full((ne, d)), full((ne, 1))],
        out_specs=[pl.BlockSpec((tm, d), lambda i: (i, 0)),
                   pl.BlockSpec((tm * SUBLANES, LANES), lambda i: (i, 0)),
                   pl.BlockSpec((ne, tm), lambda i: (0, i))],
        out_shape=[jax.ShapeDtypeStruct((total_rows, d), F32),
                   jax.ShapeDtypeStruct((total_rows * SUBLANES, LANES), F32),
                   jax.ShapeDtypeStruct((ne, total_rows), F32)],
        compiler_params=_cparams(("arbitrary",)),
        name="outproj",
    )(mixed_p, xp2d, mixed_s, xs2d, w_bf, g, rwt, rb)


def _route_kernel(lg_ref, e_ref, r_ref, g_ref, cnt_ref, carry_s, *, tr):
    @pl.when(pl.program_id(0) == 0)
    def _():
        carry_s[...] = jnp.zeros(carry_s.shape, F32)

    work = lg_ref[...]
    ne = work.shape[0]
    eid = lax.broadcasted_iota(I32, (ne, tr), 0).astype(F32)
    sels, vals, idxs = [], [], []
    for _ in range(TOP_K):
        m = work.max(axis=0, keepdims=True)
        idx = jnp.where(work == m, eid, float(ne)).min(axis=0, keepdims=True)
        sel = eid == idx
        sels.append(sel)
        vals.append(m)
        idxs.append(idx)
        work = jnp.where(sel, -jnp.inf, work)
    ex = [jnp.exp(v - vals[0]) for v in vals]
    den = ex[0] + ex[1] + ex[2] + ex[3]
    chosen = jnp.zeros((ne, tr), F32)
    for sel in sels:
        chosen = chosen + sel.astype(F32)
    rr = lax.broadcasted_iota(I32, (tr, tr), 0)
    cc = lax.broadcasted_iota(I32, (tr, tr), 1)
    upper = jnp.where(rr < cc, 1.0, 0.0).astype(BF16)
    before = jnp.dot(chosen.astype(BF16), upper, preferred_element_type=F32)
    carry = carry_s[:, 0:1]
    rank_all = before + carry
    for k in range(TOP_K):
        e_ref[k:k + 1, :] = idxs[k].astype(I32)
        g_ref[k:k + 1, :] = ex[k] / den
        r_ref[k:k + 1, :] = jnp.where(sels[k], rank_all, 0.0).sum(axis=0, keepdims=True).astype(I32)
    new_carry = carry + chosen.sum(axis=1, keepdims=True)
    carry_s[...] = jnp.broadcast_to(new_carry, carry_s.shape)
    cnt_ref[...] = jnp.broadcast_to(new_carry, cnt_ref.shape).astype(I32)


def _route(lgt, *, tr):
    ne, t = lgt.shape
    blk = lambda rows: pl.BlockSpec((rows, tr), lambda i: (0, i))
    return pl.pallas_call(
        functools.partial(_route_kernel, tr=tr),
        grid=(t // tr,),
        in_specs=[blk(ne)],
        out_specs=[blk(TOP_K), blk(TOP_K), blk(TOP_K), pl.BlockSpec((ne, LANES), lambda i: (0, 0))],
        out_shape=[jax.ShapeDtypeStruct((TOP_K, t), I32),
                   jax.ShapeDtypeStruct((TOP_K, t), I32),
                   jax.ShapeDtypeStruct((TOP_K, t), F32),
                   jax.ShapeDtypeStruct((ne, LANES), I32)],
        scratch_shapes=[pltpu.VMEM((ne, LANES), F32)],
        compiler_params=_cparams(("arbitrary",)),
        name="route",
    )(lgt)


def _plan_kernel(cnt_ref, meta_ref, be_ref, *, ne, nblocks, blk):
    acc = jnp.int32(0)
    for e in range(ne):
        c = cnt_ref[e, 0]
        pc = ((c + (blk - 1)) // blk) * blk
        meta_ref[e] = acc
        b0 = acc // blk
        acc = acc + pc
        meta_ref[ne + e] = acc
        b1 = acc // blk

        def fill(b, carry, e=e):
            be_ref[b] = jnp.int32(e)
            return carry

        lax.fori_loop(b0, b1, fill, 0)
    used = acc // blk
    meta_ref[2 * ne] = used

    def fill_rest(b, carry):
        be_ref[b] = jnp.int32(ne - 1)
        return carry

    lax.fori_loop(used, nblocks, fill_rest, 0)
    for j in range(2 * ne + 1, meta_ref.shape[0]):
        meta_ref[j] = jnp.int32(0)


def _plan(counts, *, nblocks, blk):
    ne = counts.shape[0]
    smem = pl.BlockSpec(memory_space=pltpu.SMEM)
    return pl.pallas_call(
        functools.partial(_plan_kernel, ne=ne, nblocks=nblocks, blk=blk),
        in_specs=[smem],
        out_specs=[smem, smem],
        out_shape=[jax.ShapeDtypeStruct((LANES,), I32), jax.ShapeDtypeStruct((nblocks,), I32)],
        name="plan",
    )(counts)


def _slots_kernel(e_ref, r_ref, cnt_ref, d_ref, *, blk):
    ne = cnt_ref.shape[0]
    tr = e_ref.shape[1]
    padded = jnp.bitwise_and(cnt_ref[:, 0:1] + (blk - 1), -blk).astype(F32)
    eid1 = lax.broadcasted_iota(I32, (ne, 1), 0)
    start = jnp.zeros((ne, 1), F32)
    for e in range(ne - 1):
        start = start + jnp.where(eid1 > e, padded[e:e + 1, :], 0.0)
    eid = lax.broadcasted_iota(I32, (ne, tr), 0)
    for k in range(TOP_K):
        hit = eid == e_ref[k:k + 1, :]
        base = jnp.where(hit, start, 0.0).sum(axis=0, keepdims=True)
        d_ref[k:k + 1, :] = base.astype(I32) + r_ref[k:k + 1, :]


def _slots(e_all, r_all, counts, *, blk, tr):
    t = e_all.shape[1]
    blk_spec = pl.BlockSpec((TOP_K, tr), lambda i: (0, i))
    return pl.pallas_call(
        functools.partial(_slots_kernel, blk=blk),
        grid=(t // tr,),
        in_specs=[blk_spec, blk_spec, pl.BlockSpec(counts.shape, lambda i: (0, 0))],
        out_specs=blk_spec,
        out_shape=jax.ShapeDtypeStruct((TOP_K, t), I32),
        compiler_params=_cparams(("arbitrary",)),
        name="slots",
    )(e_all, r_all, counts)


DISPATCH_RING = 3


def _dispatch_kernel(meta_ref, d_ref, hn_ref, xs_ref, zero_s, ring, zsem, in_sem, out_sem,
                     *, td, ne, blk, nblocks, nsteps):
    i = pl.program_id(0)
    rows = td * SUBLANES
    slot = i % DISPATCH_RING
    nxt = (i + 1) % DISPATCH_RING

    def load(step, s):
        src = hn_ref.at[pl.ds(pl.multiple_of(step * rows, rows), rows), :]
        return pltpu.make_async_copy(src, ring.at[s], in_sem.at[s])

    def wait_scatters(s):
        for _ in range(TOP_K):
            pltpu.make_async_copy(ring.at[s], xs_ref.at[pl.ds(0, rows), :], out_sem.at[s]).wait()

    @pl.when(i == 0)
    def _():
        zero_s[...] = jnp.zeros(zero_s.shape, F32)

        def zcopy(e):
            row0 = pl.multiple_of(jnp.maximum(meta_ref[ne + e] - blk, 0) * SUBLANES, SUBLANES)
            return pltpu.make_async_copy(zero_s, xs_ref.at[pl.ds(row0, blk * SUBLANES), :], zsem)

        for e in range(ne):
            zcopy(e).start()
        for e in range(ne):
            zcopy(e).wait()

        def ztail(b, carry):
            row0 = pl.multiple_of(b * (blk * SUBLANES), blk * SUBLANES)
            cp = pltpu.make_async_copy(zero_s, xs_ref.at[pl.ds(row0, blk * SUBLANES), :], zsem)
            cp.start()
            cp.wait()
            return carry

        lax.fori_loop(meta_ref[2 * ne], nblocks, ztail, 0)
        load(0, 0).start()

    @pl.when(i >= DISPATCH_RING - 1)
    def _():
        wait_scatters(nxt)

    @pl.when(i + 1 < nsteps)
    def _():
        load(i + 1, nxt).start()

    load(i, slot).wait()

    def issue(r, carry):
        src = ring.at[slot, pl.ds(pl.multiple_of(r * SUBLANES, SUBLANES), SUBLANES), :]
        for k in range(TOP_K):
            dst = xs_ref.at[pl.ds(pl.multiple_of(d_ref[r * TOP_K + k] * SUBLANES, SUBLANES), SUBLANES), :]
            pltpu.make_async_copy(src, dst, out_sem.at[slot]).start(priority=k % 2)
        return carry

    lax.fori_loop(0, td, issue, 0, unroll=2)

    @pl.when(i == nsteps - 1)
    def _():
        wait_scatters(slot)

        @pl.when(i >= 1)
        def _():
            wait_scatters((i + DISPATCH_RING - 1) % DISPATCH_RING)


def _dispatch(meta, dest, hn_all, *, n_slots, td, blk, ne):
    t = hn_all.shape[0] // SUBLANES
    nsteps = t // td
    assert nsteps >= DISPATCH_RING - 1
    grid_spec = pltpu.PrefetchScalarGridSpec(
        num_scalar_prefetch=1,
        grid=(nsteps,),
        in_specs=[pl.BlockSpec((td * TOP_K,), lambda i, m: (i,), memory_space=pltpu.SMEM),
                  pl.BlockSpec(memory_space=pl.ANY)],
        out_specs=pl.BlockSpec(memory_space=pl.ANY),
        scratch_shapes=[pltpu.VMEM((blk * SUBLANES, LANES), F32),
                        pltpu.VMEM((DISPATCH_RING, td * SUBLANES, LANES), F32),
                        pltpu.SemaphoreType.DMA(()),
                        pltpu.SemaphoreType.DMA((DISPATCH_RING,)),
                        pltpu.SemaphoreType.DMA((DISPATCH_RING,))],
    )
    return pl.pallas_call(
        functools.partial(_dispatch_kernel, td=td, ne=ne, blk=blk, nblocks=n_slots // blk, nsteps=nsteps),
        grid_spec=grid_spec,
        out_shape=jax.ShapeDtypeStruct((n_slots * SUBLANES, LANES), F32),
        compiler_params=_cparams(("arbitrary",)),
        name="dispatch",
    )(meta, dest, hn_all)


def _experts_kernel(be_ref, meta_ref, x_ref, w1_ref, b1g_ref, b1l_ref, w2_ref, b2_ref,
                    y_ref, w1_s, w2_s, *, ne, blk):
    b = pl.program_id(0)
    used = meta_ref[2 * ne]
    half = MXU_COLS // 2
    ngroups = w1_s.shape[1] // MXU_COLS
    fresh = jnp.logical_or(b == 0, be_ref[b] != be_ref[jnp.maximum(b - 1, 0)])

    @pl.when(jnp.logical_and(fresh, b < used))
    def _():
        src = lax.broadcasted_iota(I32, (MXU_COLS, MXU_COLS), 0)
        dst = lax.broadcasted_iota(I32, (MXU_COLS, MXU_COLS), 1)
        want = jnp.where(dst < half, 2 * dst, 2 * (dst - half) + 1)
        perm = jnp.where(src == want, 1.0, 0.0).astype(BF16)
        for t in range(ngroups):
            cols = slice(t * MXU_COLS, (t + 1) * MXU_COLS)
            w = w1_ref[0, :, cols].astype(BF16)
            w1_s[:, cols] = jnp.dot(w, perm, preferred_element_type=F32).astype(BF16)
        w2_s[...] = w2_ref[0].astype(BF16)

    @pl.when(b < used)
    def _():
        xb = _load_token_tiles(x_ref, blk).astype(BF16)
        acts = []
        for t in range(ngroups):
            h = jnp.dot(xb, w1_s[:, t * MXU_COLS:(t + 1) * MXU_COLS], preferred_element_type=F32)
            hg = h[:, :half] + b1g_ref[0, :, t * half:(t + 1) * half]
            hl = h[:, half:] + b1l_ref[0, :, t * half:(t + 1) * half]
            g = jnp.minimum(hg, SWIGLU_LIMIT)
            lin = jnp.clip(hl, -SWIGLU_LIMIT, SWIGLU_LIMIT)
            acts.append((g * _sigmoid(SWIGLU_ALPHA * g) * (lin + 1.0)).astype(BF16))
        act = jnp.concatenate(acts, axis=1)
        y = jnp.dot(act, w2_s[...], preferred_element_type=F32) + b2_ref[0]
        _store_token_tiles(y_ref, y)

    @pl.when(b >= used)
    def _():
        y_ref[...] = jnp.zeros(y_ref.shape, F32)


def _experts(be, meta, xs, w1, b1g, b1l, w2, b2, *, blk):
    ne, d, dff2 = w1.shape
    dff = dff2 // 2
    n_slots = xs.shape[0] // SUBLANES
    nblocks = n_slots // blk

    def xmap(b, be_r, meta_r):
        return (jnp.minimum(b, meta_r[2 * ne] - 1), 0)

    def wmap(b, be_r, meta_r):
        return (be_r[b], 0, 0)

    grid_spec = pltpu.PrefetchScalarGridSpec(
        num_scalar_prefetch=2,
        grid=(nblocks,),
        in_specs=[pl.BlockSpec((blk * SUBLANES, LANES), xmap),
                  pl.BlockSpec((1, d, dff2), wmap),
                  pl.BlockSpec((1, 1, dff), wmap),
                  pl.BlockSpec((1, 1, dff), wmap),
                  pl.BlockSpec((1, dff, d), wmap),
                  pl.BlockSpec((1, 1, d), wmap)],
        out_specs=pl.BlockSpec((blk * SUBLANES, LANES), lambda b, be_r, meta_r: (b, 0)),
        scratch_shapes=[pltpu.VMEM((d, dff2), BF16), pltpu.VMEM((dff, d), BF16)],
    )
    return pl.pallas_call(
        functools.partial(_experts_kernel, ne=ne, blk=blk),
        grid_spec=grid_spec,
        out_shape=jax.ShapeDtypeStruct((n_slots * SUBLANES, LANES), F32),
        compiler_params=_cparams(("arbitrary",)),
        name="experts",
    )(be, meta, xs, w1, b1g, b1l, w2, b2)


def _combine_kernel(d_ref, dn_ref, x2_ref, gt_ref, gn_ref, ys_ref, y_ref, buf, sem, *, tc, nsteps):
    i = pl.program_id(0)
    slot = i % 2
    rows = tc * SUBLANES

    def gather(idx_ref, s):
        def body(r, carry):
            for k in range(TOP_K):
                src = ys_ref.at[pl.ds(pl.multiple_of(idx_ref[r * TOP_K + k] * SUBLANES, SUBLANES), SUBLANES), :]
                dst = buf.at[s, k, pl.ds(pl.multiple_of(r * SUBLANES, SUBLANES), SUBLANES), :]
                pltpu.make_async_copy(src, dst, sem.at[s]).start(priority=k % 2)
            return carry

        lax.fori_loop(0, tc, body, 0, unroll=4)

    @pl.when(i == 0)
    def _():
        gather(d_ref, 0)

    @pl.when(i + 1 < nsteps)
    def _():
        gather(dn_ref, 1 - slot)

    for k in range(TOP_K):
        pltpu.make_async_copy(ys_ref.at[pl.ds(0, rows), :], buf.at[slot, k], sem.at[slot]).wait()
    acc = x2_ref[...]
    for k in range(TOP_K):
        acc = acc + gt_ref[:, k:k + 1] * _load_token_tiles(buf.at[slot, k], tc)
    y_ref[...] = _rms(acc, gn_ref[...])


def _combine(dest, x2_all, gt_all, gn, ys, *, row_off, rows, tc):
    _, d = x2_all.shape
    boff = row_off // tc
    nsteps = rows // tc
    last = boff + nsteps - 1
    smem_blk = lambda fn: pl.BlockSpec((tc * TOP_K,), fn, memory_space=pltpu.SMEM)
    return pl.pallas_call(
        functools.partial(_combine_kernel, tc=tc, nsteps=nsteps),
        grid=(nsteps,),
        in_specs=[smem_blk(lambda i: (i + boff,)),
                  smem_blk(lambda i: (jnp.minimum(i + boff + 1, last),)),
                  pl.BlockSpec((tc, d), lambda i: (i + boff, 0)),
                  pl.BlockSpec((tc, TOP_K), lambda i: (i + boff, 0)),
                  pl.BlockSpec((1, d), lambda i: (0, 0)),
                  pl.BlockSpec(memory_space=pl.ANY)],
        out_specs=pl.BlockSpec((tc, d), lambda i: (i, 0)),
        out_shape=jax.ShapeDtypeStruct((rows, d), F32),
        scratch_shapes=[pltpu.VMEM((2, TOP_K, tc * SUBLANES, LANES), F32), pltpu.SemaphoreType.DMA((2,))],
        compiler_params=_cparams(("arbitrary",)),
        name="combine",
    )(dest, dest, x2_all, gt_all, gn, ys)


def _pick_tile(n, pref):
    t = min(pref, n)
    while n % t:
        t //= 2
    return t


def kernel(x_prompt, x_sample, cache_k, cache_v, state_conv, state_lru, norm_mix, w_in, conv_w, conv_b,
           lru_wa, lru_ba, lru_wx, lru_bx, lru_lambda, rel_bias, w_out, norm_ffn, router_w, router_b,
           w1, b1, w2, b2, norm_out):
    depth = w_in.shape[0]
    assert depth == 1, "single-layer step"
    bp, sp, d = x_prompt.shape
    bs, ss, _ = x_sample.shape
    n_heads = d // HEAD_DIM
    band = LEFT_CHUNKS * CHUNK
    n_keep = min(band, sp)
    n_cache = cache_k.shape[2]
    ne = router_w.shape[-1]
    dff = w2.shape[2]
    l = 0
    assert PAST_LEN % CHUNK == 0 and n_cache == band and ss == CHUNK
    assert d == SUBLANES * LANES, "token-tile layout holds one token per (8,128) tile"

    w_in_bf = w_in[l].astype(BF16)
    w_out_bf = w_out[l].astype(BF16)
    g_mix = norm_mix[l].reshape(1, d)
    g_ffn = norm_ffn[l].reshape(1, d)
    g_out = norm_out.reshape(1, d)
    wg = jnp.concatenate([lru_wa[l], lru_wx[l]], axis=-1).astype(BF16)
    ba = lru_ba[l].reshape(1, d)
    bx = lru_bx[l].reshape(1, d)
    lam = lru_lambda[l].reshape(1, d)
    cw = conv_w[l]
    cb = conv_b[l].reshape(1, d)
    rwt = router_w[l].T
    rb = router_b[l].reshape(ne, 1)
    b1g = b1[l][:, 0::2].reshape(ne, 1, dff)
    b1l = b1[l][:, 1::2].reshape(ne, 1, dff)
    b2r = b2[l].reshape(ne, 1, d)

    tp = bp * sp
    tsm = bs * ss
    t_all = tp + tsm

    gq = ATT_GROUP * CHUNK
    bias_p = _bias_table(rel_bias[l], gq, band + gq)
    bias_s = _bias_table(rel_bias[l], ss, n_cache + ss)

    xp2d = x_prompt.reshape(tp, d)
    conv0 = jnp.zeros((bp, CONV_W - 1, d), F32)
    lru0 = jnp.zeros((bp, 1, d), F32)
    qp, kp, vp, ga_p, mr_p, ktp, vtp, cnew_p, hlast_p = _mixer_prompt(
        xp2d, g_mix, w_in_bf, conv0, lru0, cw, cb, wg, ba, bx, lam,
        batch=bp, seq=sp, n_keep=n_keep, tm=_pick_tile(sp, ROW_TILE))
    mixed_p = _attn_prompt(qp, kp, vp, bias_p, ga_p, mr_p, batch=bp, seq=sp)

    xs2d = x_sample.reshape(tsm, d)
    xr, q, k, v, gr, ga_s, kts, vts = _inproj(xs2d, g_mix, w_in_bf, tm=_pick_tile(tsm, ROW_TILE))
    mr_s, cnew_s, hlast_s = _lru(xr, gr, state_conv[l], state_lru[l].reshape(bs, 1, d), cw, cb, wg, ba, bx, lam,
                                 batch=bs, seq=ss, ts=ss, starts_at_zero=False)
    ck = cache_k[l].reshape(bs * n_cache * n_heads, HEAD_DIM)
    cv = cache_v[l].reshape(bs * n_cache * n_heads, HEAD_DIM)
    mixed_s = _attn_sample(q, k, v, ck, cv, bias_s, ga_s, mr_s, batch=bs, seq=ss)

    tmo = _pick_tile(tsm, ROW_TILE)
    x2_all, hn_all, lgt_all = _outproj(mixed_p, xp2d, mixed_s, xs2d, w_out_bf, g_ffn, rwt, rb, tm=tmo)

    tr = _pick_tile(t_all, ROUTE_TILE)
    e_all, r_all, gates, counts = _route(lgt_all, tr=tr)
    blk = MOE_BLOCK
    nblocks = -(-(t_all * TOP_K) // blk) + ne
    meta, be = _plan(counts, nblocks=nblocks, blk=blk)
    dest = _slots(e_all, r_all, counts, blk=blk, tr=tr).T.reshape(t_all * TOP_K)
    td = _pick_tile(t_all, 512)
    xs = _dispatch(meta, dest, hn_all, n_slots=nblocks * blk, td=td, blk=blk, ne=ne)
    ys = _experts(be, meta, xs, w1[l], b1g, b1l, w2[l], b2r, blk=blk)
    gt_all = gates.T
    tc = _pick_tile(tsm, 256)
    y_p = _combine(dest, x2_all, gt_all, g_out, ys, row_off=0, rows=tp, tc=tc)
    y_s = _combine(dest, x2_all, gt_all, g_out, ys, row_off=tp, rows=tsm, tc=tc)

    return (y_p.reshape(bp, sp, d), y_s.reshape(bs, ss, d),
            ktp.reshape(1, bp, n_keep, n_heads, HEAD_DIM), vtp.reshape(1, bp, n_keep, n_heads, HEAD_DIM),
            cnew_p[None], hlast_p.reshape(1, bp, d),
            kts.reshape(1, bs, ss, n_heads, HEAD_DIM), vts.reshape(1, bs, ss, n_heads, HEAD_DIM),
            cnew_s[None], hlast_s.reshape(1, bs, d))
```

```python
import functools

import jax
import jax.numpy as jnp
from jax import lax
from jax.experimental import pallas as pl
from jax.experimental.pallas import tpu as pltpu

F32 = jnp.float32
BF16 = jnp.bfloat16
I32 = jnp.int32

CHUNK = 64
LEFT_CHUNKS = 8
HEAD_DIM = 128
MAX_REL = 256
NEG_INF = -1e30
LRU_BLOCKS = 8
CONV_W = 4
LRU_C = 8.0
TOP_K = 4
SWIGLU_ALPHA = 1.702
SWIGLU_LIMIT = 7.0
RMS_EPS = 1e-6
PAST_LEN = 1024
LOG2_E = 1.4426950408889634
QK_SCALE = HEAD_DIM ** -0.5 * LOG2_E

LANES = 128
SUBLANES = 8
MXU_COLS = 256
VMEM_LIMIT = 56 * 1024 * 1024
MOE_BLOCK = 512
ROUTE_TILE = 1024
ROW_TILE = 512
ATT_GROUP = 4


def _cparams(sem):
    return pltpu.CompilerParams(dimension_semantics=sem, vmem_limit_bytes=VMEM_LIMIT)


def _sigmoid(x):
    return 0.5 * jnp.tanh(0.5 * x) + 0.5


def _rms(x, g):
    ms = jnp.mean(x * x, axis=-1, keepdims=True)
    return x * lax.rsqrt(ms + RMS_EPS) * g


def _inproj_kernel(x_ref, g_ref, w_ref, xr_ref, q_ref, k_ref, v_ref, gr_ref, ga_ref,
                   kt_ref, vt_ref, *, d):
    u = _rms(x_ref[...], g_ref[...]).astype(BF16)

    def piece(j):
        return jnp.dot(u, w_ref[:, j * d:(j + 1) * d], preferred_element_type=F32)

    xr_ref[...] = piece(0)
    q_ref[...] = (piece(1) * QK_SCALE).astype(BF16)
    kf = piece(2)
    k_ref[...] = kf.astype(BF16)
    kt_ref[...] = kf
    vf = piece(3)
    v_ref[...] = vf.astype(BF16)
    vt_ref[...] = vf
    gr_ref[...] = piece(4).astype(BF16)
    ga_ref[...] = _sigmoid(piece(5)).astype(BF16)


def _inproj(x2d, g, w_bf, *, tm):
    n, d = x2d.shape
    row = pl.BlockSpec((tm, d), lambda i: (i, 0))
    bf = jax.ShapeDtypeStruct((n, d), BF16)
    f32 = jax.ShapeDtypeStruct((n, d), F32)
    return pl.pallas_call(
        functools.partial(_inproj_kernel, d=d),
        grid=(n // tm,),
        in_specs=[row,
                  pl.BlockSpec((1, d), lambda i: (0, 0)),
                  pl.BlockSpec((d, 6 * d), lambda i: (0, 0), pipeline_mode=pl.Buffered(1))],
        out_specs=[row] * 8,
        out_shape=[f32, bf, bf, bf, bf, bf, f32, f32],
        compiler_params=_cparams(("arbitrary",)),
        name="inproj",
    )(x2d, g, w_bf)


def _lru_kernel(xr_ref, gr_ref, cs_ref, h0_ref, cw_ref, cb_ref, wg_ref, ba_ref, bx_ref, lam_ref,
                mr_ref, cnew_ref, hlast_ref, xp_s, a_s, b_s, hs_s, h_s, *, ts, starts_at_zero):
    _lru_init(pl.program_id(1), cs_ref, h0_ref, xp_s, h_s)
    _lru_steps(xr_ref[...], gr_ref[...].astype(F32), pl.program_id(1),
               cw_ref, cb_ref, wg_ref, ba_ref, bx_ref, lam_ref,
               mr_ref, cnew_ref, hlast_ref, xp_s, a_s, b_s, hs_s, h_s,
               ts=ts, starts_at_zero=starts_at_zero)


def _lru_init(t, cs_ref, h0_ref, xp_s, h_s):
    head = SUBLANES
    hist = CONV_W - 1

    @pl.when(t == 0)
    def _():
        xp_s[0:head, :] = jnp.zeros((head, xp_s.shape[1]), F32)
        xp_s[head - hist:head, :] = cs_ref[0]
        h_s[...] = jnp.broadcast_to(h0_ref[0], h_s.shape)


def _lru_steps(xr, gr, t, cw_ref, cb_ref, wg_ref, ba_ref, bx_ref, lam_ref,
               mr_ref, cnew_ref, hlast_ref, xp_s, a_s, b_s, hs_s, h_s, *, ts, starts_at_zero):
    head = SUBLANES
    hist = CONV_W - 1
    xp_s[head:head + ts, :] = xr
    cw = cw_ref[...]
    xc = cb_ref[...] + xp_s[head - hist:head - hist + ts, :] * cw[0:1, :]
    for j in range(1, CONV_W):
        xc = xc + xp_s[head - hist + j:head - hist + j + ts, :] * cw[j:j + 1, :]
    tail = xp_s[ts + head - hist:ts + head, :]
    xp_s[head - hist:head, :] = tail
    cnew_ref[0] = tail

    lam = lam_ref[...]
    z = -lam
    softplus = jnp.maximum(z, 0.0) + jnp.log1p(jnp.exp(-jnp.abs(z)))
    c = -LRU_C * softplus
    bw = xc.shape[1] // LRU_BLOCKS
    if starts_at_zero:
        first = (lax.broadcasted_iota(I32, (ts, bw), 0) + t * ts) == 0
    for g in range(LRU_BLOCKS):
        sl = slice(g * bw, (g + 1) * bw)
        xg = xc[:, sl]
        pre = jnp.dot(xg.astype(BF16), wg_ref[g], preferred_element_type=F32)
        r = _sigmoid(pre[:, :bw] + ba_ref[:, sl])
        ig = _sigmoid(pre[:, bw:] + bx_ref[:, sl])
        log_a = c[:, sl] * r
        a = jnp.exp(log_a)
        mult = jnp.sqrt(1.0 - a * a)
        if starts_at_zero:
            mult = jnp.where(first, 1.0, mult)
        a_s[:, sl] = a
        b_s[:, sl] = mult * (ig * xg)

    sub = lax.broadcasted_iota(I32, (SUBLANES, xc.shape[1]), 0)

    def scan_body(i, h):
        base = pl.multiple_of(i * SUBLANES, SUBLANES)
        a = a_s[pl.ds(base, SUBLANES), :]
        b = b_s[pl.ds(base, SUBLANES), :]
        for shift in (1, 2, 4):
            a_prev = jnp.where(sub < shift, 1.0, pltpu.roll(a, shift, 0))
            b_prev = jnp.where(sub < shift, 0.0, pltpu.roll(b, shift, 0))
            b = a * b_prev + b
            a = a * a_prev
        rows = a * h + b
        hs_s[pl.ds(base, SUBLANES), :] = rows
        return jnp.broadcast_to(rows[SUBLANES - 1:SUBLANES, :], rows.shape)

    h = lax.fori_loop(0, ts // SUBLANES, scan_body, h_s[...])
    h_s[...] = h
    hlast_ref[0] = h[0:1, :]
    mr_ref[...] = (_sigmoid(gr) * hs_s[...]).astype(BF16)


def _lru(xr, gr, conv_state, h0, conv_w, conv_b, wg, ba, bx, lam, *, batch, seq, ts, starts_at_zero):
    n, d = xr.shape
    ntb = seq // ts
    row = pl.BlockSpec((ts, d), lambda b, t: (b * ntb + t, 0))
    vec = pl.BlockSpec((1, d), lambda b, t: (0, 0))
    return pl.pallas_call(
        functools.partial(_lru_kernel, ts=ts, starts_at_zero=starts_at_zero),
        grid=(batch, ntb),
        in_specs=[row, row,
                  pl.BlockSpec((1, CONV_W - 1, d), lambda b, t: (b, 0, 0)),
                  pl.BlockSpec((1, 1, d), lambda b, t: (b, 0, 0)),
                  pl.BlockSpec((CONV_W, d), lambda b, t: (0, 0)),
                  vec,
                  pl.BlockSpec(wg.shape, lambda b, t: (0, 0, 0)),
                  vec, vec, vec],
        out_specs=[row,
                   pl.BlockSpec((1, CONV_W - 1, d), lambda b, t: (b, 0, 0)),
                   pl.BlockSpec((1, 1, d), lambda b, t: (b, 0, 0))],
        out_shape=[jax.ShapeDtypeStruct((n, d), BF16),
                   jax.ShapeDtypeStruct((batch, CONV_W - 1, d), F32),
                   jax.ShapeDtypeStruct((batch, 1, d), F32)],
        scratch_shapes=_lru_scratch(ts, d),
        compiler_params=_cparams(("arbitrary", "arbitrary")),
        name="lru",
    )(xr, gr, conv_state, h0, conv_w, conv_b, wg, ba, bx, lam)


def _lru_scratch(ts, d):
    return [pltpu.VMEM((ts + SUBLANES, d), F32),
            pltpu.VMEM((ts, d), F32),
            pltpu.VMEM((ts, d), F32),
            pltpu.VMEM((ts, d), F32),
            pltpu.VMEM((SUBLANES, d), F32)]


def _mixer_kernel(x_ref, g_ref, w_ref, cs_ref, h0_ref, cw_ref, cb_ref, wg_ref, ba_ref, bx_ref,
                  lam_ref, q_ref, k_ref, v_ref, ga_ref, mr_ref, kt_ref, vt_ref, cnew_ref, hlast_ref,
                  xp_s, a_s, b_s, hs_s, h_s, *, d, ntb):
    t = pl.program_id(0) % ntb
    _lru_init(t, cs_ref, h0_ref, xp_s, h_s)
    u = _rms(x_ref[...], g_ref[...]).astype(BF16)

    def piece(j):
        return jnp.dot(u, w_ref[:, j * d:(j + 1) * d], preferred_element_type=F32)

    q_ref[...] = (piece(1) * QK_SCALE).astype(BF16)
    kf = piece(2)
    k_ref[...] = kf.astype(BF16)
    kt_ref[...] = kf
    vf = piece(3)
    v_ref[...] = vf.astype(BF16)
    vt_ref[...] = vf
    ga_ref[...] = _sigmoid(piece(5)).astype(BF16)
    _lru_steps(piece(0), piece(4), t,
               cw_ref, cb_ref, wg_ref, ba_ref, bx_ref, lam_ref,
               mr_ref, cnew_ref, hlast_ref, xp_s, a_s, b_s, hs_s, h_s,
               ts=x_ref.shape[0], starts_at_zero=True)


def _mixer_prompt(x2d, g, w_bf, conv_state, h0, conv_w, conv_b, wg, ba, bx, lam, *, batch, seq, n_keep, tm):
    n, d = x2d.shape
    ntb = seq // tm
    nk = n_keep // tm
    tail_map = lambda i: ((i // ntb) * nk + jnp.maximum(i % ntb - (ntb - nk), 0), 0)
    row = pl.BlockSpec((tm, d), lambda i: (i, 0))
    vec = pl.BlockSpec((1, d), lambda i: (0, 0))
    per_batch = lambda rows: pl.BlockSpec((1, rows, d), lambda i: (i // ntb, 0, 0))
    bf = jax.ShapeDtypeStruct((n, d), BF16)
    tail = jax.ShapeDtypeStruct((batch * n_keep, d), F32)
    return pl.pallas_call(
        functools.partial(_mixer_kernel, d=d, ntb=ntb),
        grid=(n // tm,),
        in_specs=[row, vec,
                  pl.BlockSpec((d, 6 * d), lambda i: (0, 0), pipeline_mode=pl.Buffered(1)),
                  per_batch(CONV_W - 1), per_batch(1),
                  pl.BlockSpec((CONV_W, d), lambda i: (0, 0)), vec,
                  pl.BlockSpec(wg.shape, lambda i: (0, 0, 0)), vec, vec, vec],
        out_specs=[row, row, row, row, row,
                   pl.BlockSpec((tm, d), tail_map), pl.BlockSpec((tm, d), tail_map),
                   per_batch(CONV_W - 1), per_batch(1)],
        out_shape=[bf, bf, bf, bf, bf, tail, tail,
                   jax.ShapeDtypeStruct((batch, CONV_W - 1, d), F32),
                   jax.ShapeDtypeStruct((batch, 1, d), F32)],
        scratch_shapes=_lru_scratch(tm, d),
        compiler_params=_cparams(("arbitrary",)),
        name="mixer",
    )(x2d, g, w_bf, conv_state, h0, conv_w, conv_b, wg, ba, bx, lam)


def _bias_kernel(f_ref, o_ref, *, nq, nkeys, width):
    f = f_ref[0]
    x = jnp.broadcast_to(f, (nq, width))
    rolled = pltpu.roll(x, width - (nq - 1), 1, stride=1, stride_axis=0)
    t = rolled[:, :nkeys]
    qc = lax.broadcasted_iota(I32, (nq, nkeys), 0) // CHUNK
    kc = lax.broadcasted_iota(I32, (nq, nkeys), 1) // CHUNK
    ok = jnp.logical_and(kc >= qc, kc <= qc + LEFT_CHUNKS)
    o_ref[0] = jnp.where(ok, t * LOG2_E, NEG_INF)


def _bias_table(rel_bias, nq, nkeys):
    nh = rel_bias.shape[0]
    band = LEFT_CHUNKS * CHUNK
    width = -(-(nkeys + nq) // LANES) * LANES
    left = band + (nq - 1) - MAX_REL
    right = max(width - left - (2 * MAX_REL + 1), 0)
    flipped = rel_bias[:, ::-1]
    f = jnp.pad(flipped, ((0, 0), (left, right)), mode="edge")[:, :width].reshape(nh, 1, width)
    return pl.pallas_call(
        functools.partial(_bias_kernel, nq=nq, nkeys=nkeys, width=width),
        grid=(nh,),
        in_specs=[pl.BlockSpec((1, 1, width), lambda h: (h, 0, 0))],
        out_specs=pl.BlockSpec((1, nq, nkeys), lambda h: (h, 0, 0)),
        out_shape=jax.ShapeDtypeStruct((nh, nq, nkeys), F32),
        compiler_params=_cparams(("arbitrary",)),
        name="bias",
    )(f)


def _attn_kernel(*refs, part_rows, head_major, n_heads, mask_parts):
    np_ = len(part_rows)
    q_ref = refs[0]
    k_refs = refs[1:1 + np_]
    v_refs = refs[1 + np_:1 + 2 * np_]
    bias_ref, ga_ref, mr_ref, o_ref = refs[1 + 2 * np_:]
    i = pl.program_id(1)

    def head(ref, p, h):
        if head_major[p]:
            return ref[pl.ds(h, part_rows[p], stride=n_heads), :].astype(BF16)
        return ref[:, h * HEAD_DIM:(h + 1) * HEAD_DIM].astype(BF16)

    def one_head(h, masked):
        hs = slice(h * HEAD_DIM, (h + 1) * HEAD_DIM)
        qh = q_ref[:, hs]
        s_parts = []
        off = 0
        for p in range(np_):
            kp = head(k_refs[p], p, h)
            s = lax.dot_general(qh, kp, (((1,), (1,)), ((), ())), preferred_element_type=F32)
            s = s + bias_ref[h, :, off:off + part_rows[p]]
            if masked and p < np_ - 1:
                s = jnp.where(i < (np_ - 1 - p), NEG_INF, s)
            s_parts.append(s)
            off += part_rows[p]
        if len(set(part_rows)) == 1:
            m = functools.reduce(jnp.maximum, s_parts).max(axis=-1, keepdims=True)
        else:
            m = functools.reduce(jnp.maximum, [s.max(axis=-1, keepdims=True) for s in s_parts])
        o = None
        for p in range(np_):
            e = jnp.exp2(s_parts[p] - m).astype(BF16)
            vp = head(v_refs[p], p, h)
            v_aug = jnp.concatenate([vp, jnp.ones(vp.shape, BF16)], axis=1)
            op = jnp.dot(e, v_aug, preferred_element_type=F32)
            o = op if o is None else o + op
        att = o[:, :HEAD_DIM] / o[:, HEAD_DIM:]
        mixed = mr_ref[:, hs].astype(F32) + ga_ref[:, hs].astype(F32) * att
        o_ref[:, hs] = mixed.astype(BF16)

    for h in range(n_heads):
        one_head(h, mask_parts)


def _attn_prompt(q, k, v, bias, ga, mr, *, batch, seq):
    n, d = q.shape
    gq = ATT_GROUP * CHUNK
    nparts = LEFT_CHUNKS // ATT_GROUP + 1
    ng = seq // gq
    cur = pl.BlockSpec((gq, d), lambda b, i: (b * ng + i, 0))

    def back(p):
        return pl.BlockSpec((gq, d), lambda b, i: (b * ng + jnp.maximum(i - p, 0), 0))

    kv_specs = [back(nparts - 1 - p) for p in range(nparts)]
    n_heads = d // HEAD_DIM
    return pl.pallas_call(
        functools.partial(_attn_kernel, part_rows=(gq,) * nparts, head_major=(False,) * nparts,
                          n_heads=n_heads, mask_parts=True),
        grid=(batch, ng),
        in_specs=[cur] + kv_specs + kv_specs
                 + [pl.BlockSpec(bias.shape, lambda b, i: (0, 0, 0), pipeline_mode=pl.Buffered(1)), cur, cur],
        out_specs=cur,
        out_shape=jax.ShapeDtypeStruct((n, d), BF16),
        compiler_params=_cparams(("arbitrary", "arbitrary")),
        name="attn_prompt",
    )(q, *([k] * nparts), *([v] * nparts), bias, ga, mr)


def _attn_sample(q, k, v, ck, cv, bias, ga, mr, *, batch, seq):
    n, d = q.shape
    n_heads = d // HEAD_DIM
    nc = ck.shape[0] // (batch * n_heads)
    cur = pl.BlockSpec((seq, d), lambda b, i: (b, 0))
    cache = pl.BlockSpec((nc * n_heads, HEAD_DIM), lambda b, i: (b, 0))
    return pl.pallas_call(
        functools.partial(_attn_kernel, part_rows=(nc, seq), head_major=(True, False),
                          n_heads=n_heads, mask_parts=False),
        grid=(batch, 1),
        in_specs=[cur, cache, cur, cache, cur,
                  pl.BlockSpec(bias.shape, lambda b, i: (0, 0, 0), pipeline_mode=pl.Buffered(1)), cur, cur],
        out_specs=cur,
        out_shape=jax.ShapeDtypeStruct((n, d), BF16),
        compiler_params=_cparams(("arbitrary", "arbitrary")),
        name="attn_sample",
    )(q, ck, k, cv, v, bias, ga, mr)


def _store_token_tiles(ref, x):
    rows = x.shape[0]
    for c in range(x.shape[1] // LANES):
        ref[pl.ds(c, rows, stride=SUBLANES), :] = x[:, c * LANES:(c + 1) * LANES]


def _load_token_tiles(ref, rows):
    return jnp.concatenate([ref[pl.ds(c, rows, stride=SUBLANES), :] for c in range(SUBLANES)], axis=1)


def _split_bf16(x):
    hi = x.astype(BF16)
    lo = (x - hi.astype(F32)).astype(BF16)
    return hi, lo


def _outproj_kernel(mp_ref, xp_ref, ms_ref, xs_ref, w_ref, g_ref, rwt_ref, rb_ref,
                    x2_ref, hn_ref, lg_ref, *, ntp):
    def run(mixed_ref, x_ref):
        x2 = x_ref[...] + jnp.dot(mixed_ref[...], w_ref[...], preferred_element_type=F32)
        x2_ref[...] = x2
        hn = _rms(x2, g_ref[...])
        _store_token_tiles(hn_ref, hn)
        h_hi, h_lo = _split_bf16(hn)
        w_hi, w_lo = _split_bf16(rwt_ref[...])
        nt = (((1,), (1,)), ((), ()))
        ne = w_hi.shape[0]
        both = lax.dot_general(jnp.concatenate([w_hi, w_lo], axis=0), h_hi, nt, preferred_element_type=F32)
        lg = both[:ne] + both[ne:] + lax.dot_general(w_hi, h_lo, nt, preferred_element_type=F32)
        lg_ref[...] = lg + rb_ref[...]

    i = pl.program_id(0)

    @pl.when(i < ntp)
    def _():
        run(mp_ref, xp_ref)

    @pl.when(i >= ntp)
    def _():
        run(ms_ref, xs_ref)


def _outproj(mixed_p, xp2d, mixed_s, xs2d, w_bf, g, rwt, rb, *, tm):
    tp, d = xp2d.shape
    tsm = xs2d.shape[0]
    ne = rwt.shape[0]
    ntp = tp // tm
    nts = tsm // tm
    total_rows = tp + tsm
    prow = pl.BlockSpec((tm, d), lambda i: (jnp.minimum(i, ntp - 1), 0))
    srow = pl.BlockSpec((tm, d), lambda i: (jnp.maximum(i - ntp, 0), 0))
    full = lambda shape: pl.BlockSpec(shape, lambda i: (0,) * len(shape))
    return pl.pallas_call(
        functools.partial(_outproj_kernel, ntp=ntp),
        grid=(ntp + nts,),
        in_specs=[prow, prow, srow, srow, full((d, d)), full((1, d)), full((ne, d)), full((ne, 1))],
        out_specs=[pl.BlockSpec((tm, d), lambda i: (i, 0)),
                   pl.BlockSpec((tm * SUBLANES, LANES), lambda i: (i, 0)),
                   pl.BlockSpec((ne, tm), lambda i: (0, i))],
        out_shape=[jax.ShapeDtypeStruct((total_rows, d), F32),
                   jax.ShapeDtypeStruct((total_rows * SUBLANES, LANES), F32),
                   jax.ShapeDtypeStruct((ne, total_rows), F32)],
        compiler_params=_cparams(("arbitrary",)),
        name="outproj",
    )(mixed_p, xp2d, mixed_s, xs2d, w_bf, g, rwt, rb)


def _route_kernel(lg_ref, e_ref, r_ref, g_ref, cnt_ref, carry_s, *, tr):
    @pl.when(pl.program_id(0) == 0)
    def _():
        carry_s[...] = jnp.zeros(carry_s.shape, F32)

    work = lg_ref[...]
    ne = work.shape[0]
    eid = lax.broadcasted_iota(I32, (ne, tr), 0).astype(F32)
    sels, vals, idxs = [], [], []
    for _ in range(TOP_K):
        m = work.max(axis=0, keepdims=True)
        idx = jnp.where(work == m, eid, float(ne)).min(axis=0, keepdims=True)
        sel = eid == idx
        sels.append(sel)
        vals.append(m)
        idxs.append(idx)
        work = jnp.where(sel, -jnp.inf, work)
    ex = [jnp.exp(v - vals[0]) for v in vals]
    den = ex[0] + ex[1] + ex[2] + ex[3]
    chosen = jnp.zeros((ne, tr), F32)
    for sel in sels:
        chosen = chosen + sel.astype(F32)
    rr = lax.broadcasted_iota(I32, (tr, tr), 0)
    cc = lax.broadcasted_iota(I32, (tr, tr), 1)
    upper = jnp.where(rr < cc, 1.0, 0.0).astype(BF16)
    before = jnp.dot(chosen.astype(BF16), upper, preferred_element_type=F32)
    carry = carry_s[:, 0:1]
    rank_all = before + carry
    for k in range(TOP_K):
        e_ref[k:k + 1, :] = idxs[k].astype(I32)
        g_ref[k:k + 1, :] = ex[k] / den
        r_ref[k:k + 1, :] = jnp.where(sels[k], rank_all, 0.0).sum(axis=0, keepdims=True).astype(I32)
    new_carry = carry + chosen.sum(axis=1, keepdims=True)
    carry_s[...] = jnp.broadcast_to(new_carry, carry_s.shape)
    cnt_ref[...] = jnp.broadcast_to(new_carry, cnt_ref.shape).astype(I32)


def _route(lgt, *, tr):
    ne, t = lgt.shape
    blk = lambda rows: pl.BlockSpec((rows, tr), lambda i: (0, i))
    return pl.pallas_call(
        functools.partial(_route_kernel, tr=tr),
        grid=(t // tr,),
        in_specs=[blk(ne)],
        out_specs=[blk(TOP_K), blk(TOP_K), blk(TOP_K), pl.BlockSpec((ne, LANES), lambda i: (0, 0))],
        out_shape=[jax.ShapeDtypeStruct((TOP_K, t), I32),
                   jax.ShapeDtypeStruct((TOP_K, t), I32),
                   jax.ShapeDtypeStruct((TOP_K, t), F32),
                   jax.ShapeDtypeStruct((ne, LANES), I32)],
        scratch_shapes=[pltpu.VMEM((ne, LANES), F32)],
        compiler_params=_cparams(("arbitrary",)),
        name="route",
    )(lgt)


def _plan_kernel(cnt_ref, meta_ref, be_ref, nx_ref, *, ne, nblocks, blk):
    acc = jnp.int32(0)
    for e in range(ne):
        c = cnt_ref[e, 0]
        pc = ((c + (blk - 1)) // blk) * blk
        meta_ref[e] = acc
        b0 = acc // blk
        acc = acc + pc
        meta_ref[ne + e] = acc
        b1 = acc // blk

        def fill(b, carry, e=e):
            be_ref[b] = jnp.int32(e)
            return carry

        lax.fori_loop(b0, b1, fill, 0)
    used = acc // blk
    meta_ref[2 * ne] = used

    def fill_rest(b, carry):
        be_ref[b] = jnp.int32(ne - 1)
        return carry

    lax.fori_loop(used, nblocks, fill_rest, 0)
    for j in range(2 * ne + 1, meta_ref.shape[0]):
        meta_ref[j] = jnp.int32(0)

    following = jnp.int32(-1)
    for e in reversed(range(ne)):
        b0 = meta_ref[e] // blk
        b1 = meta_ref[ne + e] // blk

        def fill_next(b, carry, following=following):
            nx_ref[b] = following
            return carry

        lax.fori_loop(b0, b1, fill_next, 0)
        following = jnp.where(b1 > b0, jnp.int32(e), following)

    def fill_next_rest(b, carry):
        nx_ref[b] = jnp.int32(-1)
        return carry

    lax.fori_loop(used, nblocks, fill_next_rest, 0)


def _plan(counts, *, nblocks, blk):
    ne = counts.shape[0]
    smem = pl.BlockSpec(memory_space=pltpu.SMEM)
    return pl.pallas_call(
        functools.partial(_plan_kernel, ne=ne, nblocks=nblocks, blk=blk),
        in_specs=[smem],
        out_specs=[smem, smem, smem],
        out_shape=[jax.ShapeDtypeStruct((LANES,), I32), jax.ShapeDtypeStruct((nblocks,), I32),
                   jax.ShapeDtypeStruct((nblocks,), I32)],
        name="plan",
    )(counts)


def _slots_kernel(e_ref, r_ref, cnt_ref, d_ref, *, blk):
    ne = cnt_ref.shape[0]
    tr = e_ref.shape[1]
    padded = jnp.bitwise_and(cnt_ref[:, 0:1] + (blk - 1), -blk).astype(F32)
    eid1 = lax.broadcasted_iota(I32, (ne, 1), 0)
    start = jnp.zeros((ne, 1), F32)
    for e in range(ne - 1):
        start = start + jnp.where(eid1 > e, padded[e:e + 1, :], 0.0)
    eid = lax.broadcasted_iota(I32, (ne, tr), 0)
    for k in range(TOP_K):
        hit = eid == e_ref[k:k + 1, :]
        base = jnp.where(hit, start, 0.0).sum(axis=0, keepdims=True)
        d_ref[k:k + 1, :] = base.astype(I32) + r_ref[k:k + 1, :]


def _slots(e_all, r_all, counts, *, blk, tr):
    t = e_all.shape[1]
    blk_spec = pl.BlockSpec((TOP_K, tr), lambda i: (0, i))
    return pl.pallas_call(
        functools.partial(_slots_kernel, blk=blk),
        grid=(t // tr,),
        in_specs=[blk_spec, blk_spec, pl.BlockSpec(counts.shape, lambda i: (0, 0))],
        out_specs=blk_spec,
        out_shape=jax.ShapeDtypeStruct((TOP_K, t), I32),
        compiler_params=_cparams(("arbitrary",)),
        name="slots",
    )(e_all, r_all, counts)


DISPATCH_RING = 3


def _dispatch_kernel(meta_ref, d_ref, hn_ref, xs_ref, zero_s, ring, zsem, in_sem, out_sem,
                     *, td, ne, blk, nblocks, nsteps):
    i = pl.program_id(0)
    rows = td * SUBLANES
    slot = i % DISPATCH_RING
    nxt = (i + 1) % DISPATCH_RING

    def load(step, s):
        src = hn_ref.at[pl.ds(pl.multiple_of(step * rows, rows), rows), :]
        return pltpu.make_async_copy(src, ring.at[s], in_sem.at[s])

    def wait_scatters(s):
        for _ in range(TOP_K):
            pltpu.make_async_copy(ring.at[s], xs_ref.at[pl.ds(0, rows), :], out_sem.at[s]).wait()

    @pl.when(i == 0)
    def _():
        zero_s[...] = jnp.zeros(zero_s.shape, F32)

        def zcopy(e):
            row0 = pl.multiple_of(jnp.maximum(meta_ref[ne + e] - blk, 0) * SUBLANES, SUBLANES)
            return pltpu.make_async_copy(zero_s, xs_ref.at[pl.ds(row0, blk * SUBLANES), :], zsem)

        for e in range(ne):
            zcopy(e).start()
        for e in range(ne):
            zcopy(e).wait()

        def ztail(b, carry):
            row0 = pl.multiple_of(b * (blk * SUBLANES), blk * SUBLANES)
            cp = pltpu.make_async_copy(zero_s, xs_ref.at[pl.ds(row0, blk * SUBLANES), :], zsem)
            cp.start()
            cp.wait()
            return carry

        lax.fori_loop(meta_ref[2 * ne], nblocks, ztail, 0)
        load(0, 0).start()

    @pl.when(i >= DISPATCH_RING - 1)
    def _():
        wait_scatters(nxt)

    @pl.when(i + 1 < nsteps)
    def _():
        load(i + 1, nxt).start()

    load(i, slot).wait()

    def issue(r, carry):
        src = ring.at[slot, pl.ds(pl.multiple_of(r * SUBLANES, SUBLANES), SUBLANES), :]
        for k in range(TOP_K):
            dst = xs_ref.at[pl.ds(pl.multiple_of(d_ref[r * TOP_K + k] * SUBLANES, SUBLANES), SUBLANES), :]
            pltpu.make_async_copy(src, dst, out_sem.at[slot]).start(priority=k % 2)
        return carry

    lax.fori_loop(0, td, issue, 0, unroll=2)

    @pl.when(i == nsteps - 1)
    def _():
        wait_scatters(slot)

        @pl.when(i >= 1)
        def _():
            wait_scatters((i + DISPATCH_RING - 1) % DISPATCH_RING)


def _dispatch(meta, dest, hn_all, *, n_slots, td, blk, ne):
    t = hn_all.shape[0] // SUBLANES
    nsteps = t // td
    assert nsteps >= DISPATCH_RING - 1
    grid_spec = pltpu.PrefetchScalarGridSpec(
        num_scalar_prefetch=1,
        grid=(nsteps,),
        in_specs=[pl.BlockSpec((td * TOP_K,), lambda i, m: (i,), memory_space=pltpu.SMEM),
                  pl.BlockSpec(memory_space=pl.ANY)],
        out_specs=pl.BlockSpec(memory_space=pl.ANY),
        scratch_shapes=[pltpu.VMEM((blk * SUBLANES, LANES), F32),
                        pltpu.VMEM((DISPATCH_RING, td * SUBLANES, LANES), F32),
                        pltpu.SemaphoreType.DMA(()),
                        pltpu.SemaphoreType.DMA((DISPATCH_RING,)),
                        pltpu.SemaphoreType.DMA((DISPATCH_RING,))],
    )
    return pl.pallas_call(
        functools.partial(_dispatch_kernel, td=td, ne=ne, blk=blk, nblocks=n_slots // blk, nsteps=nsteps),
        grid_spec=grid_spec,
        out_shape=jax.ShapeDtypeStruct((n_slots * SUBLANES, LANES), F32),
        compiler_params=_cparams(("arbitrary",)),
        name="dispatch",
    )(meta, dest, hn_all)


def _experts_kernel(be_ref, meta_ref, nx_ref, x_ref, w1_ref, b1g_ref, b1l_ref, w2_ref, b2_ref,
                    y_ref, wf1_s, wf2_s, w1_s, w2_s, sem, group_s, *, ne, blk):
    b = pl.program_id(0)
    used = meta_ref[2 * ne]
    half = MXU_COLS // 2
    ngroups = w1_s.shape[1] // MXU_COLS
    fresh = jnp.logical_or(b == 0, be_ref[b] != be_ref[jnp.maximum(b - 1, 0)])

    def fetch(e, s):
        return (pltpu.make_async_copy(w1_ref.at[e], wf1_s.at[s], sem.at[0, s]),
                pltpu.make_async_copy(w2_ref.at[e], wf2_s.at[s], sem.at[1, s]))

    @pl.when(b == 0)
    def _():
        group_s[0] = 0
        for cp in fetch(be_ref[0], 0):
            cp.start()

    @pl.when(jnp.logical_and(fresh, b < used))
    def _():
        g = group_s[0]
        s = g % 2
        for cp in fetch(be_ref[b], s):
            cp.wait()

        @pl.when(nx_ref[b] >= 0)
        def _():
            for cp in fetch(nx_ref[b], 1 - s):
                cp.start()

        group_s[0] = g + 1
        src = lax.broadcasted_iota(I32, (MXU_COLS, MXU_COLS), 0)
        dst = lax.broadcasted_iota(I32, (MXU_COLS, MXU_COLS), 1)
        want = jnp.where(dst < half, 2 * dst, 2 * (dst - half) + 1)
        perm = jnp.where(src == want, 1.0, 0.0).astype(BF16)
        for t in range(ngroups):
            cols = slice(t * MXU_COLS, (t + 1) * MXU_COLS)
            w = wf1_s[s, :, cols].astype(BF16)
            w1_s[:, cols] = jnp.dot(w, perm, preferred_element_type=F32).astype(BF16)
        w2_s[...] = wf2_s[s].astype(BF16)

    @pl.when(b < used)
    def _():
        xb = _load_token_tiles(x_ref, blk).astype(BF16)
        acts = []
        for t in range(ngroups):
            h = jnp.dot(xb, w1_s[:, t * MXU_COLS:(t + 1) * MXU_COLS], preferred_element_type=F32)
            hg = h[:, :half] + b1g_ref[0, :, t * half:(t + 1) * half]
            hl = h[:, half:] + b1l_ref[0, :, t * half:(t + 1) * half]
            g = jnp.minimum(hg, SWIGLU_LIMIT)
            lin = jnp.clip(hl, -SWIGLU_LIMIT, SWIGLU_LIMIT)
            acts.append((g * _sigmoid(SWIGLU_ALPHA * g) * (lin + 1.0)).astype(BF16))
        act = jnp.concatenate(acts, axis=1)
        y = jnp.dot(act, w2_s[...], preferred_element_type=F32) + b2_ref[0]
        _store_token_tiles(y_ref, y)

    @pl.when(b >= used)
    def _():
        y_ref[...] = jnp.zeros(y_ref.shape, F32)


def _experts(be, meta, nx, xs, w1, b1g, b1l, w2, b2, *, blk):
    ne, d, dff2 = w1.shape
    dff = dff2 // 2
    n_slots = xs.shape[0] // SUBLANES
    nblocks = n_slots // blk

    def xmap(b, be_r, meta_r, nx_r):
        return (jnp.minimum(b, meta_r[2 * ne] - 1), 0)

    def wmap(b, be_r, meta_r, nx_r):
        return (be_r[b], 0, 0)

    grid_spec = pltpu.PrefetchScalarGridSpec(
        num_scalar_prefetch=3,
        grid=(nblocks,),
        in_specs=[pl.BlockSpec((blk * SUBLANES, LANES), xmap),
                  pl.BlockSpec(memory_space=pl.ANY),
                  pl.BlockSpec((1, 1, dff), wmap),
                  pl.BlockSpec((1, 1, dff), wmap),
                  pl.BlockSpec(memory_space=pl.ANY),
                  pl.BlockSpec((1, 1, d), wmap)],
        out_specs=pl.BlockSpec((blk * SUBLANES, LANES), lambda b, be_r, meta_r, nx_r: (b, 0)),
        scratch_shapes=[pltpu.VMEM((2, d, dff2), F32), pltpu.VMEM((2, dff, d), F32),
                        pltpu.VMEM((d, dff2), BF16), pltpu.VMEM((dff, d), BF16),
                        pltpu.SemaphoreType.DMA((2, 2)), pltpu.SMEM((1,), I32)],
    )
    return pl.pallas_call(
        functools.partial(_experts_kernel, ne=ne, blk=blk),
        grid_spec=grid_spec,
        out_shape=jax.ShapeDtypeStruct((n_slots * SUBLANES, LANES), F32),
        compiler_params=_cparams(("arbitrary",)),
        name="experts",
    )(be, meta, nx, xs, w1, b1g, b1l, w2, b2)


def _combine_kernel(d_ref, dn_ref, x2_ref, gt_ref, gn_ref, ys_ref, y_ref, buf, sem, *, tc, nsteps):
    i = pl.program_id(0)
    slot = i % 2
    rows = tc * SUBLANES

    def gather(idx_ref, s):
        def body(r, carry):
            for k in range(TOP_K):
                src = ys_ref.at[pl.ds(pl.multiple_of(idx_ref[r * TOP_K + k] * SUBLANES, SUBLANES), SUBLANES), :]
                dst = buf.at[s, k, pl.ds(pl.multiple_of(r * SUBLANES, SUBLANES), SUBLANES), :]
                pltpu.make_async_copy(src, dst, sem.at[s]).start(priority=k % 2)
            return carry

        lax.fori_loop(0, tc, body, 0, unroll=4)

    @pl.when(i == 0)
    def _():
        gather(d_ref, 0)

    @pl.when(i + 1 < nsteps)
    def _():
        gather(dn_ref, 1 - slot)

    for k in range(TOP_K):
        pltpu.make_async_copy(ys_ref.at[pl.ds(0, rows), :], buf.at[slot, k], sem.at[slot]).wait()
    acc = x2_ref[...]
    for k in range(TOP_K):
        acc = acc + gt_ref[:, k:k + 1] * _load_token_tiles(buf.at[slot, k], tc)
    y_ref[...] = _rms(acc, gn_ref[...])


def _combine(dest, x2_all, gt_all, gn, ys, *, row_off, rows, tc):
    _, d = x2_all.shape
    boff = row_off // tc
    nsteps = rows // tc
    last = boff + nsteps - 1
    smem_blk = lambda fn: pl.BlockSpec((tc * TOP_K,), fn, memory_space=pltpu.SMEM)
    return pl.pallas_call(
        functools.partial(_combine_kernel, tc=tc, nsteps=nsteps),
        grid=(nsteps,),
        in_specs=[smem_blk(lambda i: (i + boff,)),
                  smem_blk(lambda i: (jnp.minimum(i + boff + 1, last),)),
                  pl.BlockSpec((tc, d), lambda i: (i + boff, 0)),
                  pl.BlockSpec((tc, TOP_K), lambda i: (i + boff, 0)),
                  pl.BlockSpec((1, d), lambda i: (0, 0)),
                  pl.BlockSpec(memory_space=pl.ANY)],
        out_specs=pl.BlockSpec((tc, d), lambda i: (i, 0)),
        out_shape=jax.ShapeDtypeStruct((rows, d), F32),
        scratch_shapes=[pltpu.VMEM((2, TOP_K, tc * SUBLANES, LANES), F32), pltpu.SemaphoreType.DMA((2,))],
        compiler_params=_cparams(("arbitrary",)),
        name="combine",
    )(dest, dest, x2_all, gt_all, gn, ys)


def _pick_tile(n, pref):
    t = min(pref, n)
    while n % t:
        t //= 2
    return t


def kernel(x_prompt, x_sample, cache_k, cache_v, state_conv, state_lru, norm_mix, w_in, conv_w, conv_b,
           lru_wa, lru_ba, lru_wx, lru_bx, lru_lambda, rel_bias, w_out, norm_ffn, router_w, router_b,
           w1, b1, w2, b2, norm_out):
    depth = w_in.shape[0]
    assert depth == 1, "single-layer step"
    bp, sp, d = x_prompt.shape
    bs, ss, _ = x_sample.shape
    n_heads = d // HEAD_DIM
    band = LEFT_CHUNKS * CHUNK
    n_keep = min(band, sp)
    n_cache = cache_k.shape[2]
    ne = router_w.shape[-1]
    dff = w2.shape[2]
    l = 0
    assert PAST_LEN % CHUNK == 0 and n_cache == band and ss == CHUNK
    assert d == SUBLANES * LANES, "token-tile layout holds one token per (8,128) tile"

    w_in_bf = w_in[l].astype(BF16)
    w_out_bf = w_out[l].astype(BF16)
    g_mix = norm_mix[l].reshape(1, d)
    g_ffn = norm_ffn[l].reshape(1, d)
    g_out = norm_out.reshape(1, d)
    wg = jnp.concatenate([lru_wa[l], lru_wx[l]], axis=-1).astype(BF16)
    ba = lru_ba[l].reshape(1, d)
    bx = lru_bx[l].reshape(1, d)
    lam = lru_lambda[l].reshape(1, d)
    cw = conv_w[l]
    cb = conv_b[l].reshape(1, d)
    rwt = router_w[l].T
    rb = router_b[l].reshape(ne, 1)
    b1g = b1[l][:, 0::2].reshape(ne, 1, dff)
    b1l = b1[l][:, 1::2].reshape(ne, 1, dff)
    b2r = b2[l].reshape(ne, 1, d)

    tp = bp * sp
    tsm = bs * ss
    t_all = tp + tsm

    gq = ATT_GROUP * CHUNK
    bias_p = _bias_table(rel_bias[l], gq, band + gq)
    bias_s = _bias_table(rel_bias[l], ss, n_cache + ss)

    xp2d = x_prompt.reshape(tp, d)
    conv0 = jnp.zeros((bp, CONV_W - 1, d), F32)
    lru0 = jnp.zeros((bp, 1, d), F32)
    qp, kp, vp, ga_p, mr_p, ktp, vtp, cnew_p, hlast_p = _mixer_prompt(
        xp2d, g_mix, w_in_bf, conv0, lru0, cw, cb, wg, ba, bx, lam,
        batch=bp, seq=sp, n_keep=n_keep, tm=_pick_tile(sp, ROW_TILE))
    mixed_p = _attn_prompt(qp, kp, vp, bias_p, ga_p, mr_p, batch=bp, seq=sp)

    xs2d = x_sample.reshape(tsm, d)
    xr, q, k, v, gr, ga_s, kts, vts = _inproj(xs2d, g_mix, w_in_bf, tm=_pick_tile(tsm, ROW_TILE))
    mr_s, cnew_s, hlast_s = _lru(xr, gr, state_conv[l], state_lru[l].reshape(bs, 1, d), cw, cb, wg, ba, bx, lam,
                                 batch=bs, seq=ss, ts=ss, starts_at_zero=False)
    ck = cache_k[l].reshape(bs * n_cache * n_heads, HEAD_DIM)
    cv = cache_v[l].reshape(bs * n_cache * n_heads, HEAD_DIM)
    mixed_s = _attn_sample(q, k, v, ck, cv, bias_s, ga_s, mr_s, batch=bs, seq=ss)

    tmo = _pick_tile(tsm, ROW_TILE)
    x2_all, hn_all, lgt_all = _outproj(mixed_p, xp2d, mixed_s, xs2d, w_out_bf, g_ffn, rwt, rb, tm=tmo)

    tr = _pick_tile(t_all, ROUTE_TILE)
    e_all, r_all, gates, counts = _route(lgt_all, tr=tr)
    blk = MOE_BLOCK
    nblocks = -(-(t_all * TOP_K) // blk) + ne
    meta, be, nx = _plan(counts, nblocks=nblocks, blk=blk)
    dest = _slots(e_all, r_all, counts, blk=blk, tr=tr).T.reshape(t_all * TOP_K)
    td = _pick_tile(t_all, 512)
    xs = _dispatch(meta, dest, hn_all, n_slots=nblocks * blk, td=td, blk=blk, ne=ne)
    ys = _experts(be, meta, nx, xs, w1[l], b1g, b1l, w2[l], b2r, blk=blk)
    gt_all = gates.T
    tc = _pick_tile(tsm, 256)
    y_p = _combine(dest, x2_all, gt_all, g_out, ys, row_off=0, rows=tp, tc=tc)
    y_s = _combine(dest, x2_all, gt_all, g_out, ys, row_off=tp, rows=tsm, tc=tc)

    return (y_p.reshape(bp, sp, d), y_s.reshape(bs, ss, d),
            ktp.reshape(1, bp, n_keep, n_heads, HEAD_DIM), vtp.reshape(1, bp, n_keep, n_heads, HEAD_DIM),
            cnew_p[None], hlast_p.reshape(1, bp, d),
            kts.reshape(1, bs, ss, n_heads, HEAD_DIM), vts.reshape(1, bs, ss, n_heads, HEAD_DIM),
            cnew_s[None], hlast_s.reshape(1, bs, d))
```

```python
import functools

import jax
import jax.numpy as jnp
from jax import lax
from jax.experimental import pallas as pl
from jax.experimental.pallas import tpu as pltpu

F32 = jnp.float32
BF16 = jnp.bfloat16
I32 = jnp.int32

CHUNK = 64
LEFT_CHUNKS = 8
HEAD_DIM = 128
MAX_REL = 256
NEG_INF = -1e30
LRU_BLOCKS = 8
CONV_W = 4
LRU_C = 8.0
TOP_K = 4
SWIGLU_ALPHA = 1.702
SWIGLU_LIMIT = 7.0
RMS_EPS = 1e-6
PAST_LEN = 1024
LOG2_E = 1.4426950408889634
QK_SCALE = HEAD_DIM ** -0.5 * LOG2_E

LANES = 128
SUBLANES = 8
MXU_COLS = 256
VMEM_LIMIT = 56 * 1024 * 1024
MOE_BLOCK = 512
ROUTE_TILE = 1024
ROW_TILE = 512
ATT_GROUP = 4


def _cparams(sem):
    return pltpu.CompilerParams(dimension_semantics=sem, vmem_limit_bytes=VMEM_LIMIT)


def _sigmoid(x):
    return 0.5 * jnp.tanh(0.5 * x) + 0.5


def _rms(x, g):
    ms = jnp.mean(x * x, axis=-1, keepdims=True)
    return x * lax.rsqrt(ms + RMS_EPS) * g


def _inproj_kernel(x_ref, g_ref, w_ref, xr_ref, q_ref, k_ref, v_ref, gr_ref, ga_ref,
                   kt_ref, vt_ref, *, d):
    u = _rms(x_ref[...], g_ref[...]).astype(BF16)

    def piece(j):
        return jnp.dot(u, w_ref[:, j * d:(j + 1) * d], preferred_element_type=F32)

    xr_ref[...] = piece(0)
    q_ref[...] = (piece(1) * QK_SCALE).astype(BF16)
    kf = piece(2)
    k_ref[...] = kf.astype(BF16)
    kt_ref[...] = kf
    vf = piece(3)
    v_ref[...] = vf.astype(BF16)
    vt_ref[...] = vf
    gr_ref[...] = piece(4).astype(BF16)
    ga_ref[...] = _sigmoid(piece(5)).astype(BF16)


def _inproj(x2d, g, w_bf, *, tm):
    n, d = x2d.shape
    row = pl.BlockSpec((tm, d), lambda i: (i, 0))
    bf = jax.ShapeDtypeStruct((n, d), BF16)
    f32 = jax.ShapeDtypeStruct((n, d), F32)
    return pl.pallas_call(
        functools.partial(_inproj_kernel, d=d),
        grid=(n // tm,),
        in_specs=[row,
                  pl.BlockSpec((1, d), lambda i: (0, 0)),
                  pl.BlockSpec((d, 6 * d), lambda i: (0, 0), pipeline_mode=pl.Buffered(1))],
        out_specs=[row] * 8,
        out_shape=[f32, bf, bf, bf, bf, bf, f32, f32],
        compiler_params=_cparams(("arbitrary",)),
        name="inproj",
    )(x2d, g, w_bf)


def _lru_kernel(xr_ref, gr_ref, cs_ref, h0_ref, cw_ref, cb_ref, wg_ref, ba_ref, bx_ref, lam_ref,
                mr_ref, cnew_ref, hlast_ref, xp_s, a_s, b_s, hs_s, h_s, *, ts, starts_at_zero):
    _lru_init(pl.program_id(1), cs_ref, h0_ref, xp_s, h_s)
    _lru_steps(xr_ref[...], gr_ref[...].astype(F32), pl.program_id(1),
               cw_ref, cb_ref, wg_ref, ba_ref, bx_ref, lam_ref,
               mr_ref, cnew_ref, hlast_ref, xp_s, a_s, b_s, hs_s, h_s,
               ts=ts, starts_at_zero=starts_at_zero)


def _lru_init(t, cs_ref, h0_ref, xp_s, h_s):
    head = SUBLANES
    hist = CONV_W - 1

    @pl.when(t == 0)
    def _():
        xp_s[0:head, :] = jnp.zeros((head, xp_s.shape[1]), F32)
        xp_s[head - hist:head, :] = cs_ref[0]
        h_s[...] = jnp.broadcast_to(h0_ref[0], h_s.shape)


def _lru_steps(xr, gr, t, cw_ref, cb_ref, wg_ref, ba_ref, bx_ref, lam_ref,
               mr_ref, cnew_ref, hlast_ref, xp_s, a_s, b_s, hs_s, h_s, *, ts, starts_at_zero):
    head = SUBLANES
    hist = CONV_W - 1
    xp_s[head:head + ts, :] = xr
    cw = cw_ref[...]
    xc = cb_ref[...] + xp_s[head - hist:head - hist + ts, :] * cw[0:1, :]
    for j in range(1, CONV_W):
        xc = xc + xp_s[head - hist + j:head - hist + j + ts, :] * cw[j:j + 1, :]
    tail = xp_s[ts + head - hist:ts + head, :]
    xp_s[head - hist:head, :] = tail
    cnew_ref[0] = tail

    lam = lam_ref[...]
    z = -lam
    softplus = jnp.maximum(z, 0.0) + jnp.log1p(jnp.exp(-jnp.abs(z)))
    c = (-LRU_C * LOG2_E) * softplus
    bw = xc.shape[1] // LRU_BLOCKS
    if starts_at_zero:
        first = (lax.broadcasted_iota(I32, (ts, bw), 0) + t * ts) == 0
    for g in range(LRU_BLOCKS):
        sl = slice(g * bw, (g + 1) * bw)
        xg = xc[:, sl]
        pre = jnp.dot(xg.astype(BF16), wg_ref[g], preferred_element_type=F32)
        r = _sigmoid(pre[:, :bw] + ba_ref[:, sl])
        ig = _sigmoid(pre[:, bw:] + bx_ref[:, sl])
        a = jnp.exp2(c[:, sl] * r)
        rem = 1.0 - a * a
        mult = jnp.where(rem > 0.0, rem * lax.rsqrt(rem), 0.0)
        if starts_at_zero:
            mult = jnp.where(first, 1.0, mult)
        a_s[:, sl] = a
        b_s[:, sl] = mult * (ig * xg)

    sub = lax.broadcasted_iota(I32, (SUBLANES, xc.shape[1]), 0)

    def scan_body(i, h):
        base = pl.multiple_of(i * SUBLANES, SUBLANES)
        a = a_s[pl.ds(base, SUBLANES), :]
        b = b_s[pl.ds(base, SUBLANES), :]
        for shift in (1, 2, 4):
            a_prev = jnp.where(sub < shift, 1.0, pltpu.roll(a, shift, 0))
            b_prev = jnp.where(sub < shift, 0.0, pltpu.roll(b, shift, 0))
            b = a * b_prev + b
            a = a * a_prev
        rows = a * h + b
        hs_s[pl.ds(base, SUBLANES), :] = rows
        return jnp.broadcast_to(rows[SUBLANES - 1:SUBLANES, :], rows.shape)

    h = lax.fori_loop(0, ts // SUBLANES, scan_body, h_s[...])
    h_s[...] = h
    hlast_ref[0] = h[0:1, :]
    mr_ref[...] = (_sigmoid(gr) * hs_s[...]).astype(BF16)


def _lru(xr, gr, conv_state, h0, conv_w, conv_b, wg, ba, bx, lam, *, batch, seq, ts, starts_at_zero):
    n, d = xr.shape
    ntb = seq // ts
    row = pl.BlockSpec((ts, d), lambda b, t: (b * ntb + t, 0))
    vec = pl.BlockSpec((1, d), lambda b, t: (0, 0))
    return pl.pallas_call(
        functools.partial(_lru_kernel, ts=ts, starts_at_zero=starts_at_zero),
        grid=(batch, ntb),
        in_specs=[row, row,
                  pl.BlockSpec((1, CONV_W - 1, d), lambda b, t: (b, 0, 0)),
                  pl.BlockSpec((1, 1, d), lambda b, t: (b, 0, 0)),
                  pl.BlockSpec((CONV_W, d), lambda b, t: (0, 0)),
                  vec,
                  pl.BlockSpec(wg.shape, lambda b, t: (0, 0, 0)),
                  vec, vec, vec],
        out_specs=[row,
                   pl.BlockSpec((1, CONV_W - 1, d), lambda b, t: (b, 0, 0)),
                   pl.BlockSpec((1, 1, d), lambda b, t: (b, 0, 0))],
        out_shape=[jax.ShapeDtypeStruct((n, d), BF16),
                   jax.ShapeDtypeStruct((batch, CONV_W - 1, d), F32),
                   jax.ShapeDtypeStruct((batch, 1, d), F32)],
        scratch_shapes=_lru_scratch(ts, d),
        compiler_params=_cparams(("arbitrary", "arbitrary")),
        name="lru",
    )(xr, gr, conv_state, h0, conv_w, conv_b, wg, ba, bx, lam)


def _lru_scratch(ts, d):
    return [pltpu.VMEM((ts + SUBLANES, d), F32),
            pltpu.VMEM((ts, d), F32),
            pltpu.VMEM((ts, d), F32),
            pltpu.VMEM((ts, d), F32),
            pltpu.VMEM((SUBLANES, d), F32)]


def _mixer_kernel(x_ref, g_ref, w_ref, cs_ref, h0_ref, cw_ref, cb_ref, wg_ref, ba_ref, bx_ref,
                  lam_ref, q_ref, k_ref, v_ref, ga_ref, mr_ref, kt_ref, vt_ref, cnew_ref, hlast_ref,
                  xp_s, a_s, b_s, hs_s, h_s, *, d, ntb):
    t = pl.program_id(0) % ntb
    _lru_init(t, cs_ref, h0_ref, xp_s, h_s)
    u = _rms(x_ref[...], g_ref[...]).astype(BF16)

    def piece(j):
        return jnp.dot(u, w_ref[:, j * d:(j + 1) * d], preferred_element_type=F32)

    q_ref[...] = (piece(1) * QK_SCALE).astype(BF16)
    kf = piece(2)
    k_ref[...] = kf.astype(BF16)
    kt_ref[...] = kf
    vf = piece(3)
    v_ref[...] = vf.astype(BF16)
    vt_ref[...] = vf
    ga_ref[...] = _sigmoid(piece(5)).astype(BF16)
    _lru_steps(piece(0), piece(4), t,
               cw_ref, cb_ref, wg_ref, ba_ref, bx_ref, lam_ref,
               mr_ref, cnew_ref, hlast_ref, xp_s, a_s, b_s, hs_s, h_s,
               ts=x_ref.shape[0], starts_at_zero=True)


def _mixer_prompt(x2d, g, w_bf, conv_state, h0, conv_w, conv_b, wg, ba, bx, lam, *, batch, seq, n_keep, tm):
    n, d = x2d.shape
    ntb = seq // tm
    nk = n_keep // tm
    tail_map = lambda i: ((i // ntb) * nk + jnp.maximum(i % ntb - (ntb - nk), 0), 0)
    row = pl.BlockSpec((tm, d), lambda i: (i, 0))
    vec = pl.BlockSpec((1, d), lambda i: (0, 0))
    per_batch = lambda rows: pl.BlockSpec((1, rows, d), lambda i: (i // ntb, 0, 0))
    bf = jax.ShapeDtypeStruct((n, d), BF16)
    tail = jax.ShapeDtypeStruct((batch * n_keep, d), F32)
    return pl.pallas_call(
        functools.partial(_mixer_kernel, d=d, ntb=ntb),
        grid=(n // tm,),
        in_specs=[row, vec,
                  pl.BlockSpec((d, 6 * d), lambda i: (0, 0), pipeline_mode=pl.Buffered(1)),
                  per_batch(CONV_W - 1), per_batch(1),
                  pl.BlockSpec((CONV_W, d), lambda i: (0, 0)), vec,
                  pl.BlockSpec(wg.shape, lambda i: (0, 0, 0)), vec, vec, vec],
        out_specs=[row, row, row, row, row,
                   pl.BlockSpec((tm, d), tail_map), pl.BlockSpec((tm, d), tail_map),
                   per_batch(CONV_W - 1), per_batch(1)],
        out_shape=[bf, bf, bf, bf, bf, tail, tail,
                   jax.ShapeDtypeStruct((batch, CONV_W - 1, d), F32),
                   jax.ShapeDtypeStruct((batch, 1, d), F32)],
        scratch_shapes=_lru_scratch(tm, d),
        compiler_params=_cparams(("arbitrary",)),
        name="mixer",
    )(x2d, g, w_bf, conv_state, h0, conv_w, conv_b, wg, ba, bx, lam)


def _bias_kernel(f_ref, o_ref, *, nq, nkeys, width):
    f = f_ref[0]
    x = jnp.broadcast_to(f, (nq, width))
    rolled = pltpu.roll(x, width - (nq - 1), 1, stride=1, stride_axis=0)
    t = rolled[:, :nkeys]
    qc = lax.broadcasted_iota(I32, (nq, nkeys), 0) // CHUNK
    kc = lax.broadcasted_iota(I32, (nq, nkeys), 1) // CHUNK
    ok = jnp.logical_and(kc >= qc, kc <= qc + LEFT_CHUNKS)
    o_ref[0] = jnp.where(ok, t * LOG2_E, NEG_INF)


def _bias_table(rel_bias, nq, nkeys):
    nh = rel_bias.shape[0]
    band = LEFT_CHUNKS * CHUNK
    width = -(-(nkeys + nq) // LANES) * LANES
    left = band + (nq - 1) - MAX_REL
    right = max(width - left - (2 * MAX_REL + 1), 0)
    flipped = rel_bias[:, ::-1]
    f = jnp.pad(flipped, ((0, 0), (left, right)), mode="edge")[:, :width].reshape(nh, 1, width)
    return pl.pallas_call(
        functools.partial(_bias_kernel, nq=nq, nkeys=nkeys, width=width),
        grid=(nh,),
        in_specs=[pl.BlockSpec((1, 1, width), lambda h: (h, 0, 0))],
        out_specs=pl.BlockSpec((1, nq, nkeys), lambda h: (h, 0, 0)),
        out_shape=jax.ShapeDtypeStruct((nh, nq, nkeys), F32),
        compiler_params=_cparams(("arbitrary",)),
        name="bias",
    )(f)


def _attn_kernel(*refs, part_rows, head_major, n_heads, mask_parts):
    np_ = len(part_rows)
    q_ref = refs[0]
    k_refs = refs[1:1 + np_]
    v_refs = refs[1 + np_:1 + 2 * np_]
    bias_ref, ga_ref, mr_ref, o_ref = refs[1 + 2 * np_:]
    i = pl.program_id(1)

    def head(ref, p, h):
        if head_major[p]:
            return ref[pl.ds(h, part_rows[p], stride=n_heads), :].astype(BF16)
        return ref[:, h * HEAD_DIM:(h + 1) * HEAD_DIM].astype(BF16)

    def one_head(h, masked):
        hs = slice(h * HEAD_DIM, (h + 1) * HEAD_DIM)
        qh = q_ref[:, hs]
        s_parts = []
        off = 0
        for p in range(np_):
            kp = head(k_refs[p], p, h)
            s = lax.dot_general(qh, kp, (((1,), (1,)), ((), ())), preferred_element_type=F32)
            s = s + bias_ref[h, :, off:off + part_rows[p]]
            if masked and p < np_ - 1:
                s = jnp.where(i < (np_ - 1 - p), NEG_INF, s)
            s_parts.append(s)
            off += part_rows[p]
        if len(set(part_rows)) == 1:
            m = functools.reduce(jnp.maximum, s_parts).max(axis=-1, keepdims=True)
        else:
            m = functools.reduce(jnp.maximum, [s.max(axis=-1, keepdims=True) for s in s_parts])
        o = None
        for p in range(np_):
            e = jnp.exp2(s_parts[p] - m).astype(BF16)
            vp = head(v_refs[p], p, h)
            v_aug = jnp.concatenate([vp, jnp.ones(vp.shape, BF16)], axis=1)
            op = jnp.dot(e, v_aug, preferred_element_type=F32)
            o = op if o is None else o + op
        att = o[:, :HEAD_DIM] / o[:, HEAD_DIM:]
        mixed = mr_ref[:, hs].astype(F32) + ga_ref[:, hs].astype(F32) * att
        o_ref[:, hs] = mixed.astype(BF16)

    for h in range(n_heads):
        one_head(h, mask_parts)


def _attn_prompt(q, k, v, bias, ga, mr, *, batch, seq):
    n, d = q.shape
    gq = ATT_GROUP * CHUNK
    nparts = LEFT_CHUNKS // ATT_GROUP + 1
    ng = seq // gq
    cur = pl.BlockSpec((gq, d), lambda b, i: (b * ng + i, 0))

    def back(p):
        return pl.BlockSpec((gq, d), lambda b, i: (b * ng + jnp.maximum(i - p, 0), 0))

    kv_specs = [back(nparts - 1 - p) for p in range(nparts)]
    n_heads = d // HEAD_DIM
    return pl.pallas_call(
        functools.partial(_attn_kernel, part_rows=(gq,) * nparts, head_major=(False,) * nparts,
                          n_heads=n_heads, mask_parts=True),
        grid=(batch, ng),
        in_specs=[cur] + kv_specs + kv_specs
                 + [pl.BlockSpec(bias.shape, lambda b, i: (0, 0, 0), pipeline_mode=pl.Buffered(1)), cur, cur],
        out_specs=cur,
        out_shape=jax.ShapeDtypeStruct((n, d), BF16),
        compiler_params=_cparams(("arbitrary", "arbitrary")),
        name="attn_prompt",
    )(q, *([k] * nparts), *([v] * nparts), bias, ga, mr)


def _attn_sample(q, k, v, ck, cv, bias, ga, mr, *, batch, seq):
    n, d = q.shape
    n_heads = d // HEAD_DIM
    nc = ck.shape[0] // (batch * n_heads)
    cur = pl.BlockSpec((seq, d), lambda b, i: (b, 0))
    cache = pl.BlockSpec((nc * n_heads, HEAD_DIM), lambda b, i: (b, 0))
    return pl.pallas_call(
        functools.partial(_attn_kernel, part_rows=(nc, seq), head_major=(True, False),
                          n_heads=n_heads, mask_parts=False),
        grid=(batch, 1),
        in_specs=[cur, cache, cur, cache, cur,
                  pl.BlockSpec(bias.shape, lambda b, i: (0, 0, 0), pipeline_mode=pl.Buffered(1)), cur, cur],
        out_specs=cur,
        out_shape=jax.ShapeDtypeStruct((n, d), BF16),
        compiler_params=_cparams(("arbitrary", "arbitrary")),
        name="attn_sample",
    )(q, ck, k, cv, v, bias, ga, mr)


def _store_token_tiles(ref, x):
    rows = x.shape[0]
    for c in range(x.shape[1] // LANES):
        ref[pl.ds(c, rows, stride=SUBLANES), :] = x[:, c * LANES:(c + 1) * LANES]


def _load_token_tiles(ref, rows):
    return jnp.concatenate([ref[pl.ds(c, rows, stride=SUBLANES), :] for c in range(SUBLANES)], axis=1)


def _split_bf16(x):
    hi = x.astype(BF16)
    lo = (x - hi.astype(F32)).astype(BF16)
    return hi, lo


def _outproj_kernel(mp_ref, xp_ref, ms_ref, xs_ref, w_ref, g_ref, rwt_ref, rb_ref,
                    x2_ref, hn_ref, lg_ref, *, ntp):
    def run(mixed_ref, x_ref):
        x2 = x_ref[...] + jnp.dot(mixed_ref[...], w_ref[...], preferred_element_type=F32)
        x2_ref[...] = x2
        hn = _rms(x2, g_ref[...])
        _store_token_tiles(hn_ref, hn)
        h_hi, h_lo = _split_bf16(hn)
        w_hi, w_lo = _split_bf16(rwt_ref[...])
        nt = (((1,), (1,)), ((), ()))
        ne = w_hi.shape[0]
        both = lax.dot_general(jnp.concatenate([w_hi, w_lo], axis=0), h_hi, nt, preferred_element_type=F32)
        lg = both[:ne] + both[ne:] + lax.dot_general(w_hi, h_lo, nt, preferred_element_type=F32)
        lg_ref[...] = lg + rb_ref[...]

    i = pl.program_id(0)

    @pl.when(i < ntp)
    def _():
        run(mp_ref, xp_ref)

    @pl.when(i >= ntp)
    def _():
        run(ms_ref, xs_ref)


def _outproj(mixed_p, xp2d, mixed_s, xs2d, w_bf, g, rwt, rb, *, tm):
    tp, d = xp2d.shape
    tsm = xs2d.shape[0]
    ne = rwt.shape[0]
    ntp = tp // tm
    nts = tsm // tm
    total_rows = tp + tsm
    prow = pl.BlockSpec((tm, d), lambda i: (jnp.minimum(i, ntp - 1), 0))
    srow = pl.BlockSpec((tm, d), lambda i: (jnp.maximum(i - ntp, 0), 0))
    full = lambda shape: pl.BlockSpec(shape, lambda i: (0,) * len(shape))
    return pl.pallas_call(
        functools.partial(_outproj_kernel, ntp=ntp),
        grid=(ntp + nts,),
        in_specs=[prow, prow, srow, srow, full((d, d)), full((1, d)), full((ne, d)), full((ne, 1))],
        out_specs=[pl.BlockSpec((tm, d), lambda i: (i, 0)),
                   pl.BlockSpec((tm * SUBLANES, LANES), lambda i: (i, 0)),
                   pl.BlockSpec((ne, tm), lambda i: (0, i))],
        out_shape=[jax.ShapeDtypeStruct((total_rows, d), F32),
                   jax.ShapeDtypeStruct((total_rows * SUBLANES, LANES), F32),
                   jax.ShapeDtypeStruct((ne, total_rows), F32)],
        compiler_params=_cparams(("arbitrary",)),
        name="outproj",
    )(mixed_p, xp2d, mixed_s, xs2d, w_bf, g, rwt, rb)


def _route_kernel(lg_ref, e_ref, r_ref, g_ref, cnt_ref, carry_s, *, tr):
    @pl.when(pl.program_id(0) == 0)
    def _():
        carry_s[...] = jnp.zeros(carry_s.shape, F32)

    work = lg_ref[...]
    ne = work.shape[0]
    eid = lax.broadcasted_iota(I32, (ne, tr), 0).astype(F32)
    sels, vals, idxs = [], [], []
    for _ in range(TOP_K):
        m = work.max(axis=0, keepdims=True)
        idx = jnp.where(work == m, eid, float(ne)).min(axis=0, keepdims=True)
        sel = eid == idx
        sels.append(sel)
        vals.append(m)
        idxs.append(idx)
        work = jnp.where(sel, -jnp.inf, work)
    ex = [jnp.exp(v - vals[0]) for v in vals]
    den = ex[0] + ex[1] + ex[2] + ex[3]
    chosen = jnp.zeros((ne, tr), F32)
    for sel in sels:
        chosen = chosen + sel.astype(F32)
    rr = lax.broadcasted_iota(I32, (tr, tr), 0)
    cc = lax.broadcasted_iota(I32, (tr, tr), 1)
    upper = jnp.where(rr < cc, 1.0, 0.0).astype(BF16)
    before = jnp.dot(chosen.astype(BF16), upper, preferred_element_type=F32)
    carry = carry_s[:, 0:1]
    rank_all = before + carry
    for k in range(TOP_K):
        e_ref[k:k + 1, :] = idxs[k].astype(I32)
        g_ref[k:k + 1, :] = ex[k] / den
        r_ref[k:k + 1, :] = jnp.where(sels[k], rank_all, 0.0).sum(axis=0, keepdims=True).astype(I32)
    new_carry = carry + chosen.sum(axis=1, keepdims=True)
    carry_s[...] = jnp.broadcast_to(new_carry, carry_s.shape)
    cnt_ref[...] = jnp.broadcast_to(new_carry, cnt_ref.shape).astype(I32)


def _route(lgt, *, tr):
    ne, t = lgt.shape
    blk = lambda rows: pl.BlockSpec((rows, tr), lambda i: (0, i))
    return pl.pallas_call(
        functools.partial(_route_kernel, tr=tr),
        grid=(t // tr,),
        in_specs=[blk(ne)],
        out_specs=[blk(TOP_K), blk(TOP_K), blk(TOP_K), pl.BlockSpec((ne, LANES), lambda i: (0, 0))],
        out_shape=[jax.ShapeDtypeStruct((TOP_K, t), I32),
                   jax.ShapeDtypeStruct((TOP_K, t), I32),
                   jax.ShapeDtypeStruct((TOP_K, t), F32),
                   jax.ShapeDtypeStruct((ne, LANES), I32)],
        scratch_shapes=[pltpu.VMEM((ne, LANES), F32)],
        compiler_params=_cparams(("arbitrary",)),
        name="route",
    )(lgt)


def _plan_kernel(cnt_ref, meta_ref, be_ref, nx_ref, *, ne, nblocks, blk):
    acc = jnp.int32(0)
    for e in range(ne):
        c = cnt_ref[e, 0]
        pc = ((c + (blk - 1)) // blk) * blk
        meta_ref[e] = acc
        b0 = acc // blk
        acc = acc + pc
        meta_ref[ne + e] = acc
        b1 = acc // blk

        def fill(b, carry, e=e):
            be_ref[b] = jnp.int32(e)
            return carry

        lax.fori_loop(b0, b1, fill, 0)
    used = acc // blk
    meta_ref[2 * ne] = used

    def fill_rest(b, carry):
        be_ref[b] = jnp.int32(ne - 1)
        return carry

    lax.fori_loop(used, nblocks, fill_rest, 0)
    for j in range(2 * ne + 1, meta_ref.shape[0]):
        meta_ref[j] = jnp.int32(0)

    following = jnp.int32(-1)
    for e in reversed(range(ne)):
        b0 = meta_ref[e] // blk
        b1 = meta_ref[ne + e] // blk

        def fill_next(b, carry, following=following):
            nx_ref[b] = following
            return carry

        lax.fori_loop(b0, b1, fill_next, 0)
        following = jnp.where(b1 > b0, jnp.int32(e), following)

    def fill_next_rest(b, carry):
        nx_ref[b] = jnp.int32(-1)
        return carry

    lax.fori_loop(used, nblocks, fill_next_rest, 0)


def _plan(counts, *, nblocks, blk):
    ne = counts.shape[0]
    smem = pl.BlockSpec(memory_space=pltpu.SMEM)
    return pl.pallas_call(
        functools.partial(_plan_kernel, ne=ne, nblocks=nblocks, blk=blk),
        in_specs=[smem],
        out_specs=[smem, smem, smem],
        out_shape=[jax.ShapeDtypeStruct((LANES,), I32), jax.ShapeDtypeStruct((nblocks,), I32),
                   jax.ShapeDtypeStruct((nblocks,), I32)],
        name="plan",
    )(counts)


def _slots_kernel(e_ref, r_ref, cnt_ref, d_ref, *, blk):
    ne = cnt_ref.shape[0]
    tr = e_ref.shape[1]
    padded = jnp.bitwise_and(cnt_ref[:, 0:1] + (blk - 1), -blk).astype(F32)
    eid1 = lax.broadcasted_iota(I32, (ne, 1), 0)
    start = jnp.zeros((ne, 1), F32)
    for e in range(ne - 1):
        start = start + jnp.where(eid1 > e, padded[e:e + 1, :], 0.0)
    eid = lax.broadcasted_iota(I32, (ne, tr), 0)
    for k in range(TOP_K):
        hit = eid == e_ref[k:k + 1, :]
        base = jnp.where(hit, start, 0.0).sum(axis=0, keepdims=True)
        d_ref[k:k + 1, :] = base.astype(I32) + r_ref[k:k + 1, :]


def _slots(e_all, r_all, counts, *, blk, tr):
    t = e_all.shape[1]
    blk_spec = pl.BlockSpec((TOP_K, tr), lambda i: (0, i))
    return pl.pallas_call(
        functools.partial(_slots_kernel, blk=blk),
        grid=(t // tr,),
        in_specs=[blk_spec, blk_spec, pl.BlockSpec(counts.shape, lambda i: (0, 0))],
        out_specs=blk_spec,
        out_shape=jax.ShapeDtypeStruct((TOP_K, t), I32),
        compiler_params=_cparams(("arbitrary",)),
        name="slots",
    )(e_all, r_all, counts)


DISPATCH_RING = 3


def _dispatch_kernel(meta_ref, d_ref, hn_ref, xs_ref, zero_s, ring, zsem, in_sem, out_sem,
                     *, td, ne, blk, nblocks, nsteps):
    i = pl.program_id(0)
    rows = td * SUBLANES
    slot = i % DISPATCH_RING
    nxt = (i + 1) % DISPATCH_RING

    def load(step, s):
        src = hn_ref.at[pl.ds(pl.multiple_of(step * rows, rows), rows), :]
        return pltpu.make_async_copy(src, ring.at[s], in_sem.at[s])

    def wait_scatters(s):
        for _ in range(TOP_K):
            pltpu.make_async_copy(ring.at[s], xs_ref.at[pl.ds(0, rows), :], out_sem.at[s]).wait()

    @pl.when(i == 0)
    def _():
        zero_s[...] = jnp.zeros(zero_s.shape, F32)

        def zcopy(e):
            row0 = pl.multiple_of(jnp.maximum(meta_ref[ne + e] - blk, 0) * SUBLANES, SUBLANES)
            return pltpu.make_async_copy(zero_s, xs_ref.at[pl.ds(row0, blk * SUBLANES), :], zsem)

        for e in range(ne):
            zcopy(e).start()
        for e in range(ne):
            zcopy(e).wait()

        def ztail(b, carry):
            row0 = pl.multiple_of(b * (blk * SUBLANES), blk * SUBLANES)
            cp = pltpu.make_async_copy(zero_s, xs_ref.at[pl.ds(row0, blk * SUBLANES), :], zsem)
            cp.start()
            cp.wait()
            return carry

        lax.fori_loop(meta_ref[2 * ne], nblocks, ztail, 0)
        load(0, 0).start()

    @pl.when(i >= DISPATCH_RING - 1)
    def _():
        wait_scatters(nxt)

    @pl.when(i + 1 < nsteps)
    def _():
        load(i + 1, nxt).start()

    load(i, slot).wait()

    def issue(r, carry):
        src = ring.at[slot, pl.ds(pl.multiple_of(r * SUBLANES, SUBLANES), SUBLANES), :]
        for k in range(TOP_K):
            dst = xs_ref.at[pl.ds(pl.multiple_of(d_ref[r * TOP_K + k] * SUBLANES, SUBLANES), SUBLANES), :]
            pltpu.make_async_copy(src, dst, out_sem.at[slot]).start(priority=k % 2)
        return carry

    lax.fori_loop(0, td, issue, 0, unroll=2)

    @pl.when(i == nsteps - 1)
    def _():
        wait_scatters(slot)

        @pl.when(i >= 1)
        def _():
            wait_scatters((i + DISPATCH_RING - 1) % DISPATCH_RING)


def _dispatch(meta, dest, hn_all, *, n_slots, td, blk, ne):
    t = hn_all.shape[0] // SUBLANES
    nsteps = t // td
    assert nsteps >= DISPATCH_RING - 1
    grid_spec = pltpu.PrefetchScalarGridSpec(
        num_scalar_prefetch=1,
        grid=(nsteps,),
        in_specs=[pl.BlockSpec((td * TOP_K,), lambda i, m: (i,), memory_space=pltpu.SMEM),
                  pl.BlockSpec(memory_space=pl.ANY)],
        out_specs=pl.BlockSpec(memory_space=pl.ANY),
        scratch_shapes=[pltpu.VMEM((blk * SUBLANES, LANES), F32),
                        pltpu.VMEM((DISPATCH_RING, td * SUBLANES, LANES), F32),
                        pltpu.SemaphoreType.DMA(()),
                        pltpu.SemaphoreType.DMA((DISPATCH_RING,)),
                        pltpu.SemaphoreType.DMA((DISPATCH_RING,))],
    )
    return pl.pallas_call(
        functools.partial(_dispatch_kernel, td=td, ne=ne, blk=blk, nblocks=n_slots // blk, nsteps=nsteps),
        grid_spec=grid_spec,
        out_shape=jax.ShapeDtypeStruct((n_slots * SUBLANES, LANES), F32),
        compiler_params=_cparams(("arbitrary",)),
        name="dispatch",
    )(meta, dest, hn_all)


def _experts_kernel(be_ref, meta_ref, nx_ref, x_ref, w1_ref, b1g_ref, b1l_ref, w2_ref, b2_ref,
                    y_ref, wf1_s, wf2_s, w1_s, w2_s, sem, group_s, *, ne, blk):
    b = pl.program_id(0)
    used = meta_ref[2 * ne]
    half = MXU_COLS // 2
    ngroups = w1_s.shape[1] // MXU_COLS
    fresh = jnp.logical_or(b == 0, be_ref[b] != be_ref[jnp.maximum(b - 1, 0)])

    def fetch(e, s):
        return (pltpu.make_async_copy(w1_ref.at[e], wf1_s.at[s], sem.at[0, s]),
                pltpu.make_async_copy(w2_ref.at[e], wf2_s.at[s], sem.at[1, s]))

    @pl.when(b == 0)
    def _():
        group_s[0] = 0
        for cp in fetch(be_ref[0], 0):
            cp.start()

    @pl.when(jnp.logical_and(fresh, b < used))
    def _():
        g = group_s[0]
        s = g % 2
        for cp in fetch(be_ref[b], s):
            cp.wait()

        @pl.when(nx_ref[b] >= 0)
        def _():
            for cp in fetch(nx_ref[b], 1 - s):
                cp.start()

        group_s[0] = g + 1
        src = lax.broadcasted_iota(I32, (MXU_COLS, MXU_COLS), 0)
        dst = lax.broadcasted_iota(I32, (MXU_COLS, MXU_COLS), 1)
        want = jnp.where(dst < half, 2 * dst, 2 * (dst - half) + 1)
        perm = jnp.where(src == want, 1.0, 0.0).astype(BF16)
        for t in range(ngroups):
            cols = slice(t * MXU_COLS, (t + 1) * MXU_COLS)
            w = wf1_s[s, :, cols].astype(BF16)
            w1_s[:, cols] = jnp.dot(w, perm, preferred_element_type=F32).astype(BF16)
        w2_s[...] = wf2_s[s].astype(BF16)

    @pl.when(b < used)
    def _():
        xb = _load_token_tiles(x_ref, blk).astype(BF16)
        acts = []
        for t in range(ngroups):
            h = jnp.dot(xb, w1_s[:, t * MXU_COLS:(t + 1) * MXU_COLS], preferred_element_type=F32)
            hg = h[:, :half] + b1g_ref[0, :, t * half:(t + 1) * half]
            hl = h[:, half:] + b1l_ref[0, :, t * half:(t + 1) * half]
            g = jnp.minimum(hg, SWIGLU_LIMIT)
            lin = jnp.clip(hl, -SWIGLU_LIMIT, SWIGLU_LIMIT)
            acts.append((g * _sigmoid(SWIGLU_ALPHA * g) * (lin + 1.0)).astype(BF16))
        act = jnp.concatenate(acts, axis=1)
        y = jnp.dot(act, w2_s[...], preferred_element_type=F32) + b2_ref[0]
        _store_token_tiles(y_ref, y)

    @pl.when(b >= used)
    def _():
        y_ref[...] = jnp.zeros(y_ref.shape, F32)


def _experts(be, meta, nx, xs, w1, b1g, b1l, w2, b2, *, blk):
    ne, d, dff2 = w1.shape
    dff = dff2 // 2
    n_slots = xs.shape[0] // SUBLANES
    nblocks = n_slots // blk

    def xmap(b, be_r, meta_r, nx_r):
        return (jnp.minimum(b, meta_r[2 * ne] - 1), 0)

    def wmap(b, be_r, meta_r, nx_r):
        return (be_r[b], 0, 0)

    grid_spec = pltpu.PrefetchScalarGridSpec(
        num_scalar_prefetch=3,
        grid=(nblocks,),
        in_specs=[pl.BlockSpec((blk * SUBLANES, LANES), xmap),
                  pl.BlockSpec(memory_space=pl.ANY),
                  pl.BlockSpec((1, 1, dff), wmap),
                  pl.BlockSpec((1, 1, dff), wmap),
                  pl.BlockSpec(memory_space=pl.ANY),
                  pl.BlockSpec((1, 1, d), wmap)],
        out_specs=pl.BlockSpec((blk * SUBLANES, LANES), lambda b, be_r, meta_r, nx_r: (b, 0)),
        scratch_shapes=[pltpu.VMEM((2, d, dff2), F32), pltpu.VMEM((2, dff, d), F32),
                        pltpu.VMEM((d, dff2), BF16), pltpu.VMEM((dff, d), BF16),
                        pltpu.SemaphoreType.DMA((2, 2)), pltpu.SMEM((1,), I32)],
    )
    return pl.pallas_call(
        functools.partial(_experts_kernel, ne=ne, blk=blk),
        grid_spec=grid_spec,
        out_shape=jax.ShapeDtypeStruct((n_slots * SUBLANES, LANES), F32),
        compiler_params=_cparams(("arbitrary",)),
        name="experts",
    )(be, meta, nx, xs, w1, b1g, b1l, w2, b2)


def _combine_kernel(d_ref, dn_ref, x2_ref, gt_ref, gn_ref, ys_ref, y_ref, buf, sem, *, tc, nsteps):
    i = pl.program_id(0)
    slot = i % 2
    rows = tc * SUBLANES

    def gather(idx_ref, s):
        def body(r, carry):
            for k in range(TOP_K):
                src = ys_ref.at[pl.ds(pl.multiple_of(idx_ref[r * TOP_K + k] * SUBLANES, SUBLANES), SUBLANES), :]
                dst = buf.at[s, k, pl.ds(pl.multiple_of(r * SUBLANES, SUBLANES), SUBLANES), :]
                pltpu.make_async_copy(src, dst, sem.at[s]).start(priority=k % 2)
            return carry

        lax.fori_loop(0, tc, body, 0, unroll=4)

    @pl.when(i == 0)
    def _():
        gather(d_ref, 0)

    @pl.when(i + 1 < nsteps)
    def _():
        gather(dn_ref, 1 - slot)

    for k in range(TOP_K):
        pltpu.make_async_copy(ys_ref.at[pl.ds(0, rows), :], buf.at[slot, k], sem.at[slot]).wait()
    acc = x2_ref[...]
    for k in range(TOP_K):
        acc = acc + gt_ref[:, k:k + 1] * _load_token_tiles(buf.at[slot, k], tc)
    y_ref[...] = _rms(acc, gn_ref[...])


def _combine(dest, x2_all, gt_all, gn, ys, *, row_off, rows, tc):
    _, d = x2_all.shape
    boff = row_off // tc
    nsteps = rows // tc
    last = boff + nsteps - 1
    smem_blk = lambda fn: pl.BlockSpec((tc * TOP_K,), fn, memory_space=pltpu.SMEM)
    return pl.pallas_call(
        functools.partial(_combine_kernel, tc=tc, nsteps=nsteps),
        grid=(nsteps,),
        in_specs=[smem_blk(lambda i: (i + boff,)),
                  smem_blk(lambda i: (jnp.minimum(i + boff + 1, last),)),
                  pl.BlockSpec((tc, d), lambda i: (i + boff, 0)),
                  pl.BlockSpec((tc, TOP_K), lambda i: (i + boff, 0)),
                  pl.BlockSpec((1, d), lambda i: (0, 0)),
                  pl.BlockSpec(memory_space=pl.ANY)],
        out_specs=pl.BlockSpec((tc, d), lambda i: (i, 0)),
        out_shape=jax.ShapeDtypeStruct((rows, d), F32),
        scratch_shapes=[pltpu.VMEM((2, TOP_K, tc * SUBLANES, LANES), F32), pltpu.SemaphoreType.DMA((2,))],
        compiler_params=_cparams(("arbitrary",)),
        name="combine",
    )(dest, dest, x2_all, gt_all, gn, ys)


def _pick_tile(n, pref):
    t = min(pref, n)
    while n % t:
        t //= 2
    return t


def kernel(x_prompt, x_sample, cache_k, cache_v, state_conv, state_lru, norm_mix, w_in, conv_w, conv_b,
           lru_wa, lru_ba, lru_wx, lru_bx, lru_lambda, rel_bias, w_out, norm_ffn, router_w, router_b,
           w1, b1, w2, b2, norm_out):
    depth = w_in.shape[0]
    assert depth == 1, "single-layer step"
    bp, sp, d = x_prompt.shape
    bs, ss, _ = x_sample.shape
    n_heads = d // HEAD_DIM
    band = LEFT_CHUNKS * CHUNK
    n_keep = min(band, sp)
    n_cache = cache_k.shape[2]
    ne = router_w.shape[-1]
    dff = w2.shape[2]
    l = 0
    assert PAST_LEN % CHUNK == 0 and n_cache == band and ss == CHUNK
    assert d == SUBLANES * LANES, "token-tile layout holds one token per (8,128) tile"

    w_in_bf = w_in[l].astype(BF16)
    w_out_bf = w_out[l].astype(BF16)
    g_mix = norm_mix[l].reshape(1, d)
    g_ffn = norm_ffn[l].reshape(1, d)
    g_out = norm_out.reshape(1, d)
    wg = jnp.concatenate([lru_wa[l], lru_wx[l]], axis=-1).astype(BF16)
    ba = lru_ba[l].reshape(1, d)
    bx = lru_bx[l].reshape(1, d)
    lam = lru_lambda[l].reshape(1, d)
    cw = conv_w[l]
    cb = conv_b[l].reshape(1, d)
    rwt = router_w[l].T
    rb = router_b[l].reshape(ne, 1)
    b1g = b1[l][:, 0::2].reshape(ne, 1, dff)
    b1l = b1[l][:, 1::2].reshape(ne, 1, dff)
    b2r = b2[l].reshape(ne, 1, d)

    tp = bp * sp
    tsm = bs * ss
    t_all = tp + tsm

    gq = ATT_GROUP * CHUNK
    bias_p = _bias_table(rel_bias[l], gq, band + gq)
    bias_s = _bias_table(rel_bias[l], ss, n_cache + ss)

    xp2d = x_prompt.reshape(tp, d)
    conv0 = jnp.zeros((bp, CONV_W - 1, d), F32)
    lru0 = jnp.zeros((bp, 1, d), F32)
    qp, kp, vp, ga_p, mr_p, ktp, vtp, cnew_p, hlast_p = _mixer_prompt(
        xp2d, g_mix, w_in_bf, conv0, lru0, cw, cb, wg, ba, bx, lam,
        batch=bp, seq=sp, n_keep=n_keep, tm=_pick_tile(sp, ROW_TILE))
    mixed_p = _attn_prompt(qp, kp, vp, bias_p, ga_p, mr_p, batch=bp, seq=sp)

    xs2d = x_sample.reshape(tsm, d)
    xr, q, k, v, gr, ga_s, kts, vts = _inproj(xs2d, g_mix, w_in_bf, tm=_pick_tile(tsm, ROW_TILE))
    mr_s, cnew_s, hlast_s = _lru(xr, gr, state_conv[l], state_lru[l].reshape(bs, 1, d), cw, cb, wg, ba, bx, lam,
                                 batch=bs, seq=ss, ts=ss, starts_at_zero=False)
    ck = cache_k[l].reshape(bs * n_cache * n_heads, HEAD_DIM)
    cv = cache_v[l].reshape(bs * n_cache * n_heads, HEAD_DIM)
    mixed_s = _attn_sample(q, k, v, ck, cv, bias_s, ga_s, mr_s, batch=bs, seq=ss)

    tmo = _pick_tile(tsm, ROW_TILE)
    x2_all, hn_all, lgt_all = _outproj(mixed_p, xp2d, mixed_s, xs2d, w_out_bf, g_ffn, rwt, rb, tm=tmo)

    tr = _pick_tile(t_all, ROUTE_TILE)
    e_all, r_all, gates, counts = _route(lgt_all, tr=tr)
    blk = MOE_BLOCK
    nblocks = -(-(t_all * TOP_K) // blk) + ne
    meta, be, nx = _plan(counts, nblocks=nblocks, blk=blk)
    dest = _slots(e_all, r_all, counts, blk=blk, tr=tr).T.reshape(t_all * TOP_K)
    td = _pick_tile(t_all, 512)
    xs = _dispatch(meta, dest, hn_all, n_slots=nblocks * blk, td=td, blk=blk, ne=ne)
    ys = _experts(be, meta, nx, xs, w1[l], b1g, b1l, w2[l], b2r, blk=blk)
    gt_all = gates.T
    tc = _pick_tile(tsm, 256)
    y_p = _combine(dest, x2_all, gt_all, g_out, ys, row_off=0, rows=tp, tc=tc)
    y_s = _combine(dest, x2_all, gt_all, g_out, ys, row_off=tp, rows=tsm, tc=tc)

    return (y_p.reshape(bp, sp, d), y_s.reshape(bs, ss, d),
            ktp.reshape(1, bp, n_keep, n_heads, HEAD_DIM), vtp.reshape(1, bp, n_keep, n_heads, HEAD_DIM),
            cnew_p[None], hlast_p.reshape(1, bp, d),
            kts.reshape(1, bs, ss, n_heads, HEAD_DIM), vts.reshape(1, bs, ss, n_heads, HEAD_DIM),
            cnew_s[None], hlast_s.reshape(1, bs, d))
```

```python
import functools

import jax
import jax.numpy as jnp
from jax import lax
from jax.experimental import pallas as pl
from jax.experimental.pallas import tpu as pltpu

F32 = jnp.float32
BF16 = jnp.bfloat16
I32 = jnp.int32

CHUNK = 64
LEFT_CHUNKS = 8
HEAD_DIM = 128
MAX_REL = 256
NEG_INF = -1e30
LRU_BLOCKS = 8
CONV_W = 4
LRU_C = 8.0
TOP_K = 4
SWIGLU_ALPHA = 1.702
SWIGLU_LIMIT = 7.0
RMS_EPS = 1e-6
PAST_LEN = 1024
LOG2_E = 1.4426950408889634
QK_SCALE = HEAD_DIM ** -0.5 * LOG2_E

LANES = 128
SUBLANES = 8
MXU_COLS = 256
VMEM_LIMIT = 56 * 1024 * 1024
MOE_BLOCK = 512
ROUTE_TILE = 1024
ROW_TILE = 512
ATT_GROUP = 4


def _cparams(sem):
    return pltpu.CompilerParams(dimension_semantics=sem, vmem_limit_bytes=VMEM_LIMIT)


def _sigmoid(x):
    return 0.5 * jnp.tanh(0.5 * x) + 0.5


def _rms(x, g):
    ms = jnp.mean(x * x, axis=-1, keepdims=True)
    return x * lax.rsqrt(ms + RMS_EPS) * g


def _inproj_kernel(x_ref, g_ref, w_ref, xr_ref, q_ref, k_ref, v_ref, gr_ref, ga_ref,
                   kt_ref, vt_ref, *, d):
    u = _rms(x_ref[...], g_ref[...]).astype(BF16)

    def piece(j):
        return jnp.dot(u, w_ref[:, j * d:(j + 1) * d], preferred_element_type=F32)

    xr_ref[...] = piece(0)
    q_ref[...] = (piece(1) * QK_SCALE).astype(BF16)
    kf = piece(2)
    k_ref[...] = kf.astype(BF16)
    kt_ref[...] = kf
    vf = piece(3)
    v_ref[...] = vf.astype(BF16)
    vt_ref[...] = vf
    gr_ref[...] = piece(4).astype(BF16)
    ga_ref[...] = _sigmoid(piece(5)).astype(BF16)


def _inproj(x2d, g, w_bf, *, tm):
    n, d = x2d.shape
    row = pl.BlockSpec((tm, d), lambda i: (i, 0))
    bf = jax.ShapeDtypeStruct((n, d), BF16)
    f32 = jax.ShapeDtypeStruct((n, d), F32)
    return pl.pallas_call(
        functools.partial(_inproj_kernel, d=d),
        grid=(n // tm,),
        in_specs=[row,
                  pl.BlockSpec((1, d), lambda i: (0, 0)),
                  pl.BlockSpec((d, 6 * d), lambda i: (0, 0), pipeline_mode=pl.Buffered(1))],
        out_specs=[row] * 8,
        out_shape=[f32, bf, bf, bf, bf, bf, f32, f32],
        compiler_params=_cparams(("arbitrary",)),
        name="inproj",
    )(x2d, g, w_bf)


def _lru_kernel(xr_ref, gr_ref, cs_ref, h0_ref, cw_ref, cb_ref, wg_ref, ba_ref, bx_ref, lam_ref,
                mr_ref, cnew_ref, hlast_ref, xp_s, a_s, b_s, hs_s, h_s, *, ts, starts_at_zero):
    _lru_init(pl.program_id(1), cs_ref, h0_ref, xp_s, h_s)
    _lru_steps(xr_ref[...], gr_ref[...].astype(F32), pl.program_id(1),
               cw_ref, cb_ref, wg_ref, ba_ref, bx_ref, lam_ref,
               mr_ref, cnew_ref, hlast_ref, xp_s, a_s, b_s, hs_s, h_s,
               ts=ts, starts_at_zero=starts_at_zero)


def _lru_init(t, cs_ref, h0_ref, xp_s, h_s):
    head = SUBLANES
    hist = CONV_W - 1

    @pl.when(t == 0)
    def _():
        xp_s[0:head, :] = jnp.zeros((head, xp_s.shape[1]), F32)
        xp_s[head - hist:head, :] = cs_ref[0]
        h_s[...] = jnp.broadcast_to(h0_ref[0], h_s.shape)


def _lru_steps(xr, gr, t, cw_ref, cb_ref, wg_ref, ba_ref, bx_ref, lam_ref,
               mr_ref, cnew_ref, hlast_ref, xp_s, a_s, b_s, hs_s, h_s, *, ts, starts_at_zero):
    head = SUBLANES
    hist = CONV_W - 1
    xp_s[head:head + ts, :] = xr
    cw = cw_ref[...]
    xc = cb_ref[...] + xp_s[head - hist:head - hist + ts, :] * cw[0:1, :]
    for j in range(1, CONV_W):
        xc = xc + xp_s[head - hist + j:head - hist + j + ts, :] * cw[j:j + 1, :]
    tail = xp_s[ts + head - hist:ts + head, :]
    xp_s[head - hist:head, :] = tail
    cnew_ref[0] = tail

    lam = lam_ref[...]
    z = -lam
    softplus = jnp.maximum(z, 0.0) + jnp.log1p(jnp.exp(-jnp.abs(z)))
    c = (-LRU_C * LOG2_E) * softplus
    bw = xc.shape[1] // LRU_BLOCKS
    if starts_at_zero:
        first = (lax.broadcasted_iota(I32, (ts, bw), 0) + t * ts) == 0
    for g in range(LRU_BLOCKS):
        sl = slice(g * bw, (g + 1) * bw)
        xg = xc[:, sl]
        pre = jnp.dot(xg.astype(BF16), wg_ref[g], preferred_element_type=F32)
        r = _sigmoid(pre[:, :bw] + ba_ref[:, sl])
        ig = _sigmoid(pre[:, bw:] + bx_ref[:, sl])
        a = jnp.exp2(c[:, sl] * r)
        rem = 1.0 - a * a
        mult = jnp.where(rem > 0.0, rem * lax.rsqrt(rem), 0.0)
        if starts_at_zero:
            mult = jnp.where(first, 1.0, mult)
        a_s[:, sl] = a
        b_s[:, sl] = mult * (ig * xg)

    sub = lax.broadcasted_iota(I32, (SUBLANES, xc.shape[1]), 0)

    def scan_body(i, h):
        base = pl.multiple_of(i * SUBLANES, SUBLANES)
        a = a_s[pl.ds(base, SUBLANES), :]
        b = b_s[pl.ds(base, SUBLANES), :]
        for shift in (1, 2, 4):
            a_prev = jnp.where(sub < shift, 1.0, pltpu.roll(a, shift, 0))
            b_prev = jnp.where(sub < shift, 0.0, pltpu.roll(b, shift, 0))
            b = a * b_prev + b
            a = a * a_prev
        rows = a * h + b
        hs_s[pl.ds(base, SUBLANES), :] = rows
        return jnp.broadcast_to(rows[SUBLANES - 1:SUBLANES, :], rows.shape)

    h = lax.fori_loop(0, ts // SUBLANES, scan_body, h_s[...])
    h_s[...] = h
    hlast_ref[0] = h[0:1, :]
    mr_ref[...] = (_sigmoid(gr) * hs_s[...]).astype(BF16)


def _lru(xr, gr, conv_state, h0, conv_w, conv_b, wg, ba, bx, lam, *, batch, seq, ts, starts_at_zero):
    n, d = xr.shape
    ntb = seq // ts
    row = pl.BlockSpec((ts, d), lambda b, t: (b * ntb + t, 0))
    vec = pl.BlockSpec((1, d), lambda b, t: (0, 0))
    return pl.pallas_call(
        functools.partial(_lru_kernel, ts=ts, starts_at_zero=starts_at_zero),
        grid=(batch, ntb),
        in_specs=[row, row,
                  pl.BlockSpec((1, CONV_W - 1, d), lambda b, t: (b, 0, 0)),
                  pl.BlockSpec((1, 1, d), lambda b, t: (b, 0, 0)),
                  pl.BlockSpec((CONV_W, d), lambda b, t: (0, 0)),
                  vec,
                  pl.BlockSpec(wg.shape, lambda b, t: (0, 0, 0)),
                  vec, vec, vec],
        out_specs=[row,
                   pl.BlockSpec((1, CONV_W - 1, d), lambda b, t: (b, 0, 0)),
                   pl.BlockSpec((1, 1, d), lambda b, t: (b, 0, 0))],
        out_shape=[jax.ShapeDtypeStruct((n, d), BF16),
                   jax.ShapeDtypeStruct((batch, CONV_W - 1, d), F32),
                   jax.ShapeDtypeStruct((batch, 1, d), F32)],
        scratch_shapes=_lru_scratch(ts, d),
        compiler_params=_cparams(("arbitrary", "arbitrary")),
        name="lru",
    )(xr, gr, conv_state, h0, conv_w, conv_b, wg, ba, bx, lam)


def _lru_scratch(ts, d):
    return [pltpu.VMEM((ts + SUBLANES, d), F32),
            pltpu.VMEM((ts, d), F32),
            pltpu.VMEM((ts, d), F32),
            pltpu.VMEM((ts, d), F32),
            pltpu.VMEM((SUBLANES, d), F32)]


def _mixer_kernel(x_ref, g_ref, w_ref, cs_ref, h0_ref, cw_ref, cb_ref, wg_ref, ba_ref, bx_ref,
                  lam_ref, q_ref, k_ref, v_ref, ga_ref, mr_ref, kt_ref, vt_ref, cnew_ref, hlast_ref,
                  xp_s, a_s, b_s, hs_s, h_s, *, d, ntb):
    t = pl.program_id(0) % ntb
    _lru_init(t, cs_ref, h0_ref, xp_s, h_s)
    u = _rms(x_ref[...], g_ref[...]).astype(BF16)

    def piece(j):
        return jnp.dot(u, w_ref[:, j * d:(j + 1) * d], preferred_element_type=F32)

    q_ref[...] = (piece(1) * QK_SCALE).astype(BF16)
    kf = piece(2)
    k_ref[...] = kf.astype(BF16)
    kt_ref[...] = kf
    vf = piece(3)
    v_ref[...] = vf.astype(BF16)
    vt_ref[...] = vf
    ga_ref[...] = _sigmoid(piece(5)).astype(BF16)
    _lru_steps(piece(0), piece(4), t,
               cw_ref, cb_ref, wg_ref, ba_ref, bx_ref, lam_ref,
               mr_ref, cnew_ref, hlast_ref, xp_s, a_s, b_s, hs_s, h_s,
               ts=x_ref.shape[0], starts_at_zero=True)


def _mixer_prompt(x2d, g, w_bf, conv_state, h0, conv_w, conv_b, wg, ba, bx, lam, *, batch, seq, n_keep, tm):
    n, d = x2d.shape
    ntb = seq // tm
    nk = n_keep // tm
    tail_map = lambda i: ((i // ntb) * nk + jnp.maximum(i % ntb - (ntb - nk), 0), 0)
    row = pl.BlockSpec((tm, d), lambda i: (i, 0))
    vec = pl.BlockSpec((1, d), lambda i: (0, 0))
    per_batch = lambda rows: pl.BlockSpec((1, rows, d), lambda i: (i // ntb, 0, 0))
    bf = jax.ShapeDtypeStruct((n, d), BF16)
    tail = jax.ShapeDtypeStruct((batch * n_keep, d), F32)
    return pl.pallas_call(
        functools.partial(_mixer_kernel, d=d, ntb=ntb),
        grid=(n // tm,),
        in_specs=[row, vec,
                  pl.BlockSpec((d, 6 * d), lambda i: (0, 0), pipeline_mode=pl.Buffered(1)),
                  per_batch(CONV_W - 1), per_batch(1),
                  pl.BlockSpec((CONV_W, d), lambda i: (0, 0)), vec,
                  pl.BlockSpec(wg.shape, lambda i: (0, 0, 0)), vec, vec, vec],
        out_specs=[row, row, row, row, row,
                   pl.BlockSpec((tm, d), tail_map), pl.BlockSpec((tm, d), tail_map),
                   per_batch(CONV_W - 1), per_batch(1)],
        out_shape=[bf, bf, bf, bf, bf, tail, tail,
                   jax.ShapeDtypeStruct((batch, CONV_W - 1, d), F32),
                   jax.ShapeDtypeStruct((batch, 1, d), F32)],
        scratch_shapes=_lru_scratch(tm, d),
        compiler_params=_cparams(("arbitrary",)),
        name="mixer",
    )(x2d, g, w_bf, conv_state, h0, conv_w, conv_b, wg, ba, bx, lam)


def _bias_kernel(f_ref, o_ref, *, nq, nkeys, width):
    f = f_ref[0]
    x = jnp.broadcast_to(f, (nq, width))
    rolled = pltpu.roll(x, width - (nq - 1), 1, stride=1, stride_axis=0)
    t = rolled[:, :nkeys]
    qc = lax.broadcasted_iota(I32, (nq, nkeys), 0) // CHUNK
    kc = lax.broadcasted_iota(I32, (nq, nkeys), 1) // CHUNK
    ok = jnp.logical_and(kc >= qc, kc <= qc + LEFT_CHUNKS)
    o_ref[0] = jnp.where(ok, t * LOG2_E, NEG_INF)


def _bias_table(rel_bias, nq, nkeys):
    nh = rel_bias.shape[0]
    band = LEFT_CHUNKS * CHUNK
    width = -(-(nkeys + nq) // LANES) * LANES
    left = band + (nq - 1) - MAX_REL
    right = max(width - left - (2 * MAX_REL + 1), 0)
    flipped = rel_bias[:, ::-1]
    f = jnp.pad(flipped, ((0, 0), (left, right)), mode="edge")[:, :width].reshape(nh, 1, width)
    return pl.pallas_call(
        functools.partial(_bias_kernel, nq=nq, nkeys=nkeys, width=width),
        grid=(nh,),
        in_specs=[pl.BlockSpec((1, 1, width), lambda h: (h, 0, 0))],
        out_specs=pl.BlockSpec((1, nq, nkeys), lambda h: (h, 0, 0)),
        out_shape=jax.ShapeDtypeStruct((nh, nq, nkeys), F32),
        compiler_params=_cparams(("arbitrary",)),
        name="bias",
    )(f)


def _attn_kernel(*refs, part_rows, head_major, n_heads, mask_parts):
    np_ = len(part_rows)
    q_ref = refs[0]
    k_refs = refs[1:1 + np_]
    v_refs = refs[1 + np_:1 + 2 * np_]
    bias_ref, ga_ref, mr_ref, o_ref = refs[1 + 2 * np_:]
    i = pl.program_id(1)

    def head(ref, p, h):
        if head_major[p]:
            return ref[pl.ds(h, part_rows[p], stride=n_heads), :].astype(BF16)
        return ref[:, h * HEAD_DIM:(h + 1) * HEAD_DIM].astype(BF16)

    def one_head(h, masked):
        hs = slice(h * HEAD_DIM, (h + 1) * HEAD_DIM)
        qh = q_ref[:, hs]
        s_parts = []
        off = 0
        for p in range(np_):
            kp = head(k_refs[p], p, h)
            s = lax.dot_general(qh, kp, (((1,), (1,)), ((), ())), preferred_element_type=F32)
            s = s + bias_ref[h, :, off:off + part_rows[p]]
            if masked and p < np_ - 1:
                s = jnp.where(i < (np_ - 1 - p), NEG_INF, s)
            s_parts.append(s)
            off += part_rows[p]
        if len(set(part_rows)) == 1:
            m = functools.reduce(jnp.maximum, s_parts).max(axis=-1, keepdims=True)
        else:
            m = functools.reduce(jnp.maximum, [s.max(axis=-1, keepdims=True) for s in s_parts])
        o = None
        for p in range(np_):
            e = jnp.exp2((s_parts[p] - m).astype(BF16))
            vp = head(v_refs[p], p, h)
            v_aug = jnp.concatenate([vp, jnp.ones(vp.shape, BF16)], axis=1)
            op = jnp.dot(e, v_aug, preferred_element_type=F32)
            o = op if o is None else o + op
        att = o[:, :HEAD_DIM] / o[:, HEAD_DIM:]
        mixed = mr_ref[:, hs].astype(F32) + ga_ref[:, hs].astype(F32) * att
        o_ref[:, hs] = mixed.astype(BF16)

    for h in range(n_heads):
        one_head(h, mask_parts)


def _attn_prompt(q, k, v, bias, ga, mr, *, batch, seq):
    n, d = q.shape
    gq = ATT_GROUP * CHUNK
    nparts = LEFT_CHUNKS // ATT_GROUP + 1
    ng = seq // gq
    cur = pl.BlockSpec((gq, d), lambda b, i: (b * ng + i, 0))

    def back(p):
        return pl.BlockSpec((gq, d), lambda b, i: (b * ng + jnp.maximum(i - p, 0), 0))

    kv_specs = [back(nparts - 1 - p) for p in range(nparts)]
    n_heads = d // HEAD_DIM
    return pl.pallas_call(
        functools.partial(_attn_kernel, part_rows=(gq,) * nparts, head_major=(False,) * nparts,
                          n_heads=n_heads, mask_parts=True),
        grid=(batch, ng),
        in_specs=[cur] + kv_specs + kv_specs
                 + [pl.BlockSpec(bias.shape, lambda b, i: (0, 0, 0), pipeline_mode=pl.Buffered(1)), cur, cur],
        out_specs=cur,
        out_shape=jax.ShapeDtypeStruct((n, d), BF16),
        compiler_params=_cparams(("arbitrary", "arbitrary")),
        name="attn_prompt",
    )(q, *([k] * nparts), *([v] * nparts), bias, ga, mr)


def _attn_sample(q, k, v, ck, cv, bias, ga, mr, *, batch, seq):
    n, d = q.shape
    n_heads = d // HEAD_DIM
    nc = ck.shape[0] // (batch * n_heads)
    cur = pl.BlockSpec((seq, d), lambda b, i: (b, 0))
    cache = pl.BlockSpec((nc * n_heads, HEAD_DIM), lambda b, i: (b, 0))
    return pl.pallas_call(
        functools.partial(_attn_kernel, part_rows=(nc, seq), head_major=(True, False),
                          n_heads=n_heads, mask_parts=False),
        grid=(batch, 1),
        in_specs=[cur, cache, cur, cache, cur,
                  pl.BlockSpec(bias.shape, lambda b, i: (0, 0, 0), pipeline_mode=pl.Buffered(1)), cur, cur],
        out_specs=cur,
        out_shape=jax.ShapeDtypeStruct((n, d), BF16),
        compiler_params=_cparams(("arbitrary", "arbitrary")),
        name="attn_sample",
    )(q, ck, k, cv, v, bias, ga, mr)


def _store_token_tiles(ref, x):
    rows = x.shape[0]
    for c in range(x.shape[1] // LANES):
        ref[pl.ds(c, rows, stride=SUBLANES), :] = x[:, c * LANES:(c + 1) * LANES]


def _load_token_tiles(ref, rows):
    return jnp.concatenate([ref[pl.ds(c, rows, stride=SUBLANES), :] for c in range(SUBLANES)], axis=1)


def _split_bf16(x):
    hi = x.astype(BF16)
    lo = (x - hi.astype(F32)).astype(BF16)
    return hi, lo


def _outproj_kernel(mp_ref, xp_ref, ms_ref, xs_ref, w_ref, g_ref, rwt_ref, rb_ref,
                    x2_ref, hn_ref, lg_ref, *, ntp):
    def run(mixed_ref, x_ref):
        x2 = x_ref[...] + jnp.dot(mixed_ref[...], w_ref[...], preferred_element_type=F32)
        x2_ref[...] = x2
        hn = _rms(x2, g_ref[...])
        _store_token_tiles(hn_ref, hn)
        h_hi, h_lo = _split_bf16(hn)
        w_hi, w_lo = _split_bf16(rwt_ref[...])
        nt = (((1,), (1,)), ((), ()))
        ne = w_hi.shape[0]
        both = lax.dot_general(jnp.concatenate([w_hi, w_lo], axis=0), h_hi, nt, preferred_element_type=F32)
        lg = both[:ne] + both[ne:] + lax.dot_general(w_hi, h_lo, nt, preferred_element_type=F32)
        lg_ref[...] = lg + rb_ref[...]

    i = pl.program_id(0)

    @pl.when(i < ntp)
    def _():
        run(mp_ref, xp_ref)

    @pl.when(i >= ntp)
    def _():
        run(ms_ref, xs_ref)


def _outproj(mixed_p, xp2d, mixed_s, xs2d, w_bf, g, rwt, rb, *, tm):
    tp, d = xp2d.shape
    tsm = xs2d.shape[0]
    ne = rwt.shape[0]
    ntp = tp // tm
    nts = tsm // tm
    total_rows = tp + tsm
    prow = pl.BlockSpec((tm, d), lambda i: (jnp.minimum(i, ntp - 1), 0))
    srow = pl.BlockSpec((tm, d), lambda i: (jnp.maximum(i - ntp, 0), 0))
    full = lambda shape: pl.BlockSpec(shape, lambda i: (0,) * len(shape))
    return pl.pallas_call(
        functools.partial(_outproj_kernel, ntp=ntp),
        grid=(ntp + nts,),
        in_specs=[prow, prow, srow, srow, full((d, d)), full((1, d)), full((ne, d)), full((ne, 1))],
        out_specs=[pl.BlockSpec((tm, d), lambda i: (i, 0)),
                   pl.BlockSpec((tm * SUBLANES, LANES), lambda i: (i, 0)),
                   pl.BlockSpec((ne, tm), lambda i: (0, i))],
        out_shape=[jax.ShapeDtypeStruct((total_rows, d), F32),
                   jax.ShapeDtypeStruct((total_rows * SUBLANES, LANES), F32),
                   jax.ShapeDtypeStruct((ne, total_rows), F32)],
        compiler_params=_cparams(("arbitrary",)),
        name="outproj",
    )(mixed_p, xp2d, mixed_s, xs2d, w_bf, g, rwt, rb)


def _route_kernel(lg_ref, e_ref, r_ref, g_ref, cnt_ref, carry_s, *, tr):
    @pl.when(pl.program_id(0) == 0)
    def _():
        carry_s[...] = jnp.zeros(carry_s.shape, F32)

    work = lg_ref[...]
    ne = work.shape[0]
    eid = lax.broadcasted_iota(I32, (ne, tr), 0).astype(F32)
    sels, vals, idxs = [], [], []
    for _ in range(TOP_K):
        m = work.max(axis=0, keepdims=True)
        idx = jnp.where(work == m, eid, float(ne)).min(axis=0, keepdims=True)
        sel = eid == idx
        sels.append(sel)
        vals.append(m)
        idxs.append(idx)
        work = jnp.where(sel, -jnp.inf, work)
    ex = [jnp.exp(v - vals[0]) for v in vals]
    den = ex[0] + ex[1] + ex[2] + ex[3]
    chosen = jnp.zeros((ne, tr), F32)
    for sel in sels:
        chosen = chosen + sel.astype(F32)
    rr = lax.broadcasted_iota(I32, (tr, tr), 0)
    cc = lax.broadcasted_iota(I32, (tr, tr), 1)
    upper = jnp.where(rr < cc, 1.0, 0.0).astype(BF16)
    before = jnp.dot(chosen.astype(BF16), upper, preferred_element_type=F32)
    carry = carry_s[:, 0:1]
    rank_all = before + carry
    for k in range(TOP_K):
        e_ref[k:k + 1, :] = idxs[k].astype(I32)
        g_ref[k:k + 1, :] = ex[k] / den
        r_ref[k:k + 1, :] = jnp.where(sels[k], rank_all, 0.0).sum(axis=0, keepdims=True).astype(I32)
    new_carry = carry + chosen.sum(axis=1, keepdims=True)
    carry_s[...] = jnp.broadcast_to(new_carry, carry_s.shape)
    cnt_ref[...] = jnp.broadcast_to(new_carry, cnt_ref.shape).astype(I32)


def _route(lgt, *, tr):
    ne, t = lgt.shape
    blk = lambda rows: pl.BlockSpec((rows, tr), lambda i: (0, i))
    return pl.pallas_call(
        functools.partial(_route_kernel, tr=tr),
        grid=(t // tr,),
        in_specs=[blk(ne)],
        out_specs=[blk(TOP_K), blk(TOP_K), blk(TOP_K), pl.BlockSpec((ne, LANES), lambda i: (0, 0))],
        out_shape=[jax.ShapeDtypeStruct((TOP_K, t), I32),
                   jax.ShapeDtypeStruct((TOP_K, t), I32),
                   jax.ShapeDtypeStruct((TOP_K, t), F32),
                   jax.ShapeDtypeStruct((ne, LANES), I32)],
        scratch_shapes=[pltpu.VMEM((ne, LANES), F32)],
        compiler_params=_cparams(("arbitrary",)),
        name="route",
    )(lgt)


def _plan_kernel(cnt_ref, meta_ref, be_ref, nx_ref, *, ne, nblocks, blk):
    acc = jnp.int32(0)
    for e in range(ne):
        c = cnt_ref[e, 0]
        pc = ((c + (blk - 1)) // blk) * blk
        meta_ref[e] = acc
        b0 = acc // blk
        acc = acc + pc
        meta_ref[ne + e] = acc
        b1 = acc // blk

        def fill(b, carry, e=e):
            be_ref[b] = jnp.int32(e)
            return carry

        lax.fori_loop(b0, b1, fill, 0)
    used = acc // blk
    meta_ref[2 * ne] = used

    def fill_rest(b, carry):
        be_ref[b] = jnp.int32(ne - 1)
        return carry

    lax.fori_loop(used, nblocks, fill_rest, 0)
    for j in range(2 * ne + 1, meta_ref.shape[0]):
        meta_ref[j] = jnp.int32(0)

    following = jnp.int32(-1)
    for e in reversed(range(ne)):
        b0 = meta_ref[e] // blk
        b1 = meta_ref[ne + e] // blk

        def fill_next(b, carry, following=following):
            nx_ref[b] = following
            return carry

        lax.fori_loop(b0, b1, fill_next, 0)
        following = jnp.where(b1 > b0, jnp.int32(e), following)

    def fill_next_rest(b, carry):
        nx_ref[b] = jnp.int32(-1)
        return carry

    lax.fori_loop(used, nblocks, fill_next_rest, 0)


def _plan(counts, *, nblocks, blk):
    ne = counts.shape[0]
    smem = pl.BlockSpec(memory_space=pltpu.SMEM)
    return pl.pallas_call(
        functools.partial(_plan_kernel, ne=ne, nblocks=nblocks, blk=blk),
        in_specs=[smem],
        out_specs=[smem, smem, smem],
        out_shape=[jax.ShapeDtypeStruct((LANES,), I32), jax.ShapeDtypeStruct((nblocks,), I32),
                   jax.ShapeDtypeStruct((nblocks,), I32)],
        name="plan",
    )(counts)


def _slots_kernel(e_ref, r_ref, cnt_ref, d_ref, *, blk):
    ne = cnt_ref.shape[0]
    tr = e_ref.shape[1]
    padded = jnp.bitwise_and(cnt_ref[:, 0:1] + (blk - 1), -blk).astype(F32)
    eid1 = lax.broadcasted_iota(I32, (ne, 1), 0)
    start = jnp.zeros((ne, 1), F32)
    for e in range(ne - 1):
        start = start + jnp.where(eid1 > e, padded[e:e + 1, :], 0.0)
    eid = lax.broadcasted_iota(I32, (ne, tr), 0)
    for k in range(TOP_K):
        hit = eid == e_ref[k:k + 1, :]
        base = jnp.where(hit, start, 0.0).sum(axis=0, keepdims=True)
        d_ref[k:k + 1, :] = base.astype(I32) + r_ref[k:k + 1, :]


def _slots(e_all, r_all, counts, *, blk, tr):
    t = e_all.shape[1]
    blk_spec = pl.BlockSpec((TOP_K, tr), lambda i: (0, i))
    return pl.pallas_call(
        functools.partial(_slots_kernel, blk=blk),
        grid=(t // tr,),
        in_specs=[blk_spec, blk_spec, pl.BlockSpec(counts.shape, lambda i: (0, 0))],
        out_specs=blk_spec,
        out_shape=jax.ShapeDtypeStruct((TOP_K, t), I32),
        compiler_params=_cparams(("arbitrary",)),
        name="slots",
    )(e_all, r_all, counts)


DISPATCH_RING = 3


def _dispatch_kernel(meta_ref, d_ref, hn_ref, xs_ref, zero_s, ring, zsem, in_sem, out_sem,
                     *, td, ne, blk, nblocks, nsteps):
    i = pl.program_id(0)
    rows = td * SUBLANES
    slot = i % DISPATCH_RING
    nxt = (i + 1) % DISPATCH_RING

    def load(step, s):
        src = hn_ref.at[pl.ds(pl.multiple_of(step * rows, rows), rows), :]
        return pltpu.make_async_copy(src, ring.at[s], in_sem.at[s])

    def wait_scatters(s):
        for _ in range(TOP_K):
            pltpu.make_async_copy(ring.at[s], xs_ref.at[pl.ds(0, rows), :], out_sem.at[s]).wait()

    @pl.when(i == 0)
    def _():
        zero_s[...] = jnp.zeros(zero_s.shape, F32)

        def zcopy(e):
            row0 = pl.multiple_of(jnp.maximum(meta_ref[ne + e] - blk, 0) * SUBLANES, SUBLANES)
            return pltpu.make_async_copy(zero_s, xs_ref.at[pl.ds(row0, blk * SUBLANES), :], zsem)

        for e in range(ne):
            zcopy(e).start()
        for e in range(ne):
            zcopy(e).wait()

        def ztail(b, carry):
            row0 = pl.multiple_of(b * (blk * SUBLANES), blk * SUBLANES)
            cp = pltpu.make_async_copy(zero_s, xs_ref.at[pl.ds(row0, blk * SUBLANES), :], zsem)
            cp.start()
            cp.wait()
            return carry

        lax.fori_loop(meta_ref[2 * ne], nblocks, ztail, 0)
        load(0, 0).start()

    @pl.when(i >= DISPATCH_RING - 1)
    def _():
        wait_scatters(nxt)

    @pl.when(i + 1 < nsteps)
    def _():
        load(i + 1, nxt).start()

    load(i, slot).wait()

    def issue(r, carry):
        src = ring.at[slot, pl.ds(pl.multiple_of(r * SUBLANES, SUBLANES), SUBLANES), :]
        for k in range(TOP_K):
            dst = xs_ref.at[pl.ds(pl.multiple_of(d_ref[r * TOP_K + k] * SUBLANES, SUBLANES), SUBLANES), :]
            pltpu.make_async_copy(src, dst, out_sem.at[slot]).start(priority=k % 2)
        return carry

    lax.fori_loop(0, td, issue, 0, unroll=2)

    @pl.when(i == nsteps - 1)
    def _():
        wait_scatters(slot)

        @pl.when(i >= 1)
        def _():
            wait_scatters((i + DISPATCH_RING - 1) % DISPATCH_RING)


def _dispatch(meta, dest, hn_all, *, n_slots, td, blk, ne):
    t = hn_all.shape[0] // SUBLANES
    nsteps = t // td
    assert nsteps >= DISPATCH_RING - 1
    grid_spec = pltpu.PrefetchScalarGridSpec(
        num_scalar_prefetch=1,
        grid=(nsteps,),
        in_specs=[pl.BlockSpec((td * TOP_K,), lambda i, m: (i,), memory_space=pltpu.SMEM),
                  pl.BlockSpec(memory_space=pl.ANY)],
        out_specs=pl.BlockSpec(memory_space=pl.ANY),
        scratch_shapes=[pltpu.VMEM((blk * SUBLANES, LANES), F32),
                        pltpu.VMEM((DISPATCH_RING, td * SUBLANES, LANES), F32),
                        pltpu.SemaphoreType.DMA(()),
                        pltpu.SemaphoreType.DMA((DISPATCH_RING,)),
                        pltpu.SemaphoreType.DMA((DISPATCH_RING,))],
    )
    return pl.pallas_call(
        functools.partial(_dispatch_kernel, td=td, ne=ne, blk=blk, nblocks=n_slots // blk, nsteps=nsteps),
        grid_spec=grid_spec,
        out_shape=jax.ShapeDtypeStruct((n_slots * SUBLANES, LANES), F32),
        compiler_params=_cparams(("arbitrary",)),
        name="dispatch",
    )(meta, dest, hn_all)


def _experts_kernel(be_ref, meta_ref, nx_ref, x_ref, w1_ref, b1g_ref, b1l_ref, w2_ref, b2_ref,
                    y_ref, wf1_s, wf2_s, w1_s, w2_s, sem, group_s, *, ne, blk):
    b = pl.program_id(0)
    used = meta_ref[2 * ne]
    half = MXU_COLS // 2
    ngroups = w1_s.shape[1] // MXU_COLS
    fresh = jnp.logical_or(b == 0, be_ref[b] != be_ref[jnp.maximum(b - 1, 0)])

    def fetch(e, s):
        return (pltpu.make_async_copy(w1_ref.at[e], wf1_s.at[s], sem.at[0, s]),
                pltpu.make_async_copy(w2_ref.at[e], wf2_s.at[s], sem.at[1, s]))

    @pl.when(b == 0)
    def _():
        group_s[0] = 0
        for cp in fetch(be_ref[0], 0):
            cp.start()

    @pl.when(jnp.logical_and(fresh, b < used))
    def _():
        g = group_s[0]
        s = g % 2
        for cp in fetch(be_ref[b], s):
            cp.wait()

        @pl.when(nx_ref[b] >= 0)
        def _():
            for cp in fetch(nx_ref[b], 1 - s):
                cp.start()

        group_s[0] = g + 1
        src = lax.broadcasted_iota(I32, (MXU_COLS, MXU_COLS), 0)
        dst = lax.broadcasted_iota(I32, (MXU_COLS, MXU_COLS), 1)
        want = jnp.where(dst < half, 2 * dst, 2 * (dst - half) + 1)
        perm = jnp.where(src == want, 1.0, 0.0).astype(BF16)
        for t in range(ngroups):
            cols = slice(t * MXU_COLS, (t + 1) * MXU_COLS)
            w = wf1_s[s, :, cols].astype(BF16)
            w1_s[:, cols] = jnp.dot(w, perm, preferred_element_type=F32).astype(BF16)
        w2_s[...] = wf2_s[s].astype(BF16)

    @pl.when(b < used)
    def _():
        xb = _load_token_tiles(x_ref, blk).astype(BF16)
        acts = []
        for t in range(ngroups):
            h = jnp.dot(xb, w1_s[:, t * MXU_COLS:(t + 1) * MXU_COLS], preferred_element_type=F32)
            hg = h[:, :half] + b1g_ref[0, :, t * half:(t + 1) * half]
            hl = h[:, half:] + b1l_ref[0, :, t * half:(t + 1) * half]
            g = jnp.minimum(hg, SWIGLU_LIMIT)
            lin = jnp.clip(hl, -SWIGLU_LIMIT, SWIGLU_LIMIT)
            acts.append((g * _sigmoid(SWIGLU_ALPHA * g) * (lin + 1.0)).astype(BF16))
        act = jnp.concatenate(acts, axis=1)
        y = jnp.dot(act, w2_s[...], preferred_element_type=F32) + b2_ref[0]
        _store_token_tiles(y_ref, y)

    @pl.when(b >= used)
    def _():
        y_ref[...] = jnp.zeros(y_ref.shape, F32)


def _experts(be, meta, nx, xs, w1, b1g, b1l, w2, b2, *, blk):
    ne, d, dff2 = w1.shape
    dff = dff2 // 2
    n_slots = xs.shape[0] // SUBLANES
    nblocks = n_slots // blk

    def xmap(b, be_r, meta_r, nx_r):
        return (jnp.minimum(b, meta_r[2 * ne] - 1), 0)

    def wmap(b, be_r, meta_r, nx_r):
        return (be_r[b], 0, 0)

    grid_spec = pltpu.PrefetchScalarGridSpec(
        num_scalar_prefetch=3,
        grid=(nblocks,),
        in_specs=[pl.BlockSpec((blk * SUBLANES, LANES), xmap),
                  pl.BlockSpec(memory_space=pl.ANY),
                  pl.BlockSpec((1, 1, dff), wmap),
                  pl.BlockSpec((1, 1, dff), wmap),
                  pl.BlockSpec(memory_space=pl.ANY),
                  pl.BlockSpec((1, 1, d), wmap)],
        out_specs=pl.BlockSpec((blk * SUBLANES, LANES), lambda b, be_r, meta_r, nx_r: (b, 0)),
        scratch_shapes=[pltpu.VMEM((2, d, dff2), F32), pltpu.VMEM((2, dff, d), F32),
                        pltpu.VMEM((d, dff2), BF16), pltpu.VMEM((dff, d), BF16),
                        pltpu.SemaphoreType.DMA((2, 2)), pltpu.SMEM((1,), I32)],
    )
    return pl.pallas_call(
        functools.partial(_experts_kernel, ne=ne, blk=blk),
        grid_spec=grid_spec,
        out_shape=jax.ShapeDtypeStruct((n_slots * SUBLANES, LANES), F32),
        compiler_params=_cparams(("arbitrary",)),
        name="experts",
    )(be, meta, nx, xs, w1, b1g, b1l, w2, b2)


def _combine_kernel(d_ref, dn_ref, x2_ref, gt_ref, gn_ref, ys_ref, y_ref, buf, sem, *, tc, nsteps):
    i = pl.program_id(0)
    slot = i % 2
    rows = tc * SUBLANES

    def gather(idx_ref, s):
        def body(r, carry):
            for k in range(TOP_K):
                src = ys_ref.at[pl.ds(pl.multiple_of(idx_ref[r * TOP_K + k] * SUBLANES, SUBLANES), SUBLANES), :]
                dst = buf.at[s, k, pl.ds(pl.multiple_of(r * SUBLANES, SUBLANES), SUBLANES), :]
                pltpu.make_async_copy(src, dst, sem.at[s]).start(priority=k % 2)
            return carry

        lax.fori_loop(0, tc, body, 0, unroll=4)

    @pl.when(i == 0)
    def _():
        gather(d_ref, 0)

    @pl.when(i + 1 < nsteps)
    def _():
        gather(dn_ref, 1 - slot)

    for k in range(TOP_K):
        pltpu.make_async_copy(ys_ref.at[pl.ds(0, rows), :], buf.at[slot, k], sem.at[slot]).wait()
    acc = x2_ref[...]
    for k in range(TOP_K):
        acc = acc + gt_ref[:, k:k + 1] * _load_token_tiles(buf.at[slot, k], tc)
    y_ref[...] = _rms(acc, gn_ref[...])


def _combine(dest, x2_all, gt_all, gn, ys, *, row_off, rows, tc):
    _, d = x2_all.shape
    boff = row_off // tc
    nsteps = rows // tc
    last = boff + nsteps - 1
    smem_blk = lambda fn: pl.BlockSpec((tc * TOP_K,), fn, memory_space=pltpu.SMEM)
    return pl.pallas_call(
        functools.partial(_combine_kernel, tc=tc, nsteps=nsteps),
        grid=(nsteps,),
        in_specs=[smem_blk(lambda i: (i + boff,)),
                  smem_blk(lambda i: (jnp.minimum(i + boff + 1, last),)),
                  pl.BlockSpec((tc, d), lambda i: (i + boff, 0)),
                  pl.BlockSpec((tc, TOP_K), lambda i: (i + boff, 0)),
                  pl.BlockSpec((1, d), lambda i: (0, 0)),
                  pl.BlockSpec(memory_space=pl.ANY)],
        out_specs=pl.BlockSpec((tc, d), lambda i: (i, 0)),
        out_shape=jax.ShapeDtypeStruct((rows, d), F32),
        scratch_shapes=[pltpu.VMEM((2, TOP_K, tc * SUBLANES, LANES), F32), pltpu.SemaphoreType.DMA((2,))],
        compiler_params=_cparams(("arbitrary",)),
        name="combine",
    )(dest, dest, x2_all, gt_all, gn, ys)


def _pick_tile(n, pref):
    t = min(pref, n)
    while n % t:
        t //= 2
    return t


def kernel(x_prompt, x_sample, cache_k, cache_v, state_conv, state_lru, norm_mix, w_in, conv_w, conv_b,
           lru_wa, lru_ba, lru_wx, lru_bx, lru_lambda, rel_bias, w_out, norm_ffn, router_w, router_b,
           w1, b1, w2, b2, norm_out):
    depth = w_in.shape[0]
    assert depth == 1, "single-layer step"
    bp, sp, d = x_prompt.shape
    bs, ss, _ = x_sample.shape
    n_heads = d // HEAD_DIM
    band = LEFT_CHUNKS * CHUNK
    n_keep = min(band, sp)
    n_cache = cache_k.shape[2]
    ne = router_w.shape[-1]
    dff = w2.shape[2]
    l = 0
    assert PAST_LEN % CHUNK == 0 and n_cache == band and ss == CHUNK
    assert d == SUBLANES * LANES, "token-tile layout holds one token per (8,128) tile"

    w_in_bf = w_in[l].astype(BF16)
    w_out_bf = w_out[l].astype(BF16)
    g_mix = norm_mix[l].reshape(1, d)
    g_ffn = norm_ffn[l].reshape(1, d)
    g_out = norm_out.reshape(1, d)
    wg = jnp.concatenate([lru_wa[l], lru_wx[l]], axis=-1).astype(BF16)
    ba = lru_ba[l].reshape(1, d)
    bx = lru_bx[l].reshape(1, d)
    lam = lru_lambda[l].reshape(1, d)
    cw = conv_w[l]
    cb = conv_b[l].reshape(1, d)
    rwt = router_w[l].T
    rb = router_b[l].reshape(ne, 1)
    b1g = b1[l][:, 0::2].reshape(ne, 1, dff)
    b1l = b1[l][:, 1::2].reshape(ne, 1, dff)
    b2r = b2[l].reshape(ne, 1, d)

    tp = bp * sp
    tsm = bs * ss
    t_all = tp + tsm

    gq = ATT_GROUP * CHUNK
    bias_p = _bias_table(rel_bias[l], gq, band + gq)
    bias_s = _bias_table(rel_bias[l], ss, n_cache + ss)

    xp2d = x_prompt.reshape(tp, d)
    conv0 = jnp.zeros((bp, CONV_W - 1, d), F32)
    lru0 = jnp.zeros((bp, 1, d), F32)
    qp, kp, vp, ga_p, mr_p, ktp, vtp, cnew_p, hlast_p = _mixer_prompt(
        xp2d, g_mix, w_in_bf, conv0, lru0, cw, cb, wg, ba, bx, lam,
        batch=bp, seq=sp, n_keep=n_keep, tm=_pick_tile(sp, ROW_TILE))
    mixed_p = _attn_prompt(qp, kp, vp, bias_p, ga_p, mr_p, batch=bp, seq=sp)

    xs2d = x_sample.reshape(tsm, d)
    xr, q, k, v, gr, ga_s, kts, vts = _inproj(xs2d, g_mix, w_in_bf, tm=_pick_tile(tsm, ROW_TILE))
    mr_s, cnew_s, hlast_s = _lru(xr, gr, state_conv[l], state_lru[l].reshape(bs, 1, d), cw, cb, wg, ba, bx, lam,
                                 batch=bs, seq=ss, ts=ss, starts_at_zero=False)
    ck = cache_k[l].reshape(bs * n_cache * n_heads, HEAD_DIM)
    cv = cache_v[l].reshape(bs * n_cache * n_heads, HEAD_DIM)
    mixed_s = _attn_sample(q, k, v, ck, cv, bias_s, ga_s, mr_s, batch=bs, seq=ss)

    tmo = _pick_tile(tsm, ROW_TILE)
    x2_all, hn_all, lgt_all = _outproj(mixed_p, xp2d, mixed_s, xs2d, w_out_bf, g_ffn, rwt, rb, tm=tmo)

    tr = _pick_tile(t_all, ROUTE_TILE)
    e_all, r_all, gates, counts = _route(lgt_all, tr=tr)
    blk = MOE_BLOCK
    nblocks = -(-(t_all * TOP_K) // blk) + ne
    meta, be, nx = _plan(counts, nblocks=nblocks, blk=blk)
    dest = _slots(e_all, r_all, counts, blk=blk, tr=tr).T.reshape(t_all * TOP_K)
    td = _pick_tile(t_all, 512)
    xs = _dispatch(meta, dest, hn_all, n_slots=nblocks * blk, td=td, blk=blk, ne=ne)
    ys = _experts(be, meta, nx, xs, w1[l], b1g, b1l, w2[l], b2r, blk=blk)
    gt_all = gates.T
    tc = _pick_tile(tsm, 256)
    y_p = _combine(dest, x2_all, gt_all, g_out, ys, row_off=0, rows=tp, tc=tc)
    y_s = _combine(dest, x2_all, gt_all, g_out, ys, row_off=tp, rows=tsm, tc=tc)

    return (y_p.reshape(bp, sp, d), y_s.reshape(bs, ss, d),
            ktp.reshape(1, bp, n_keep, n_heads, HEAD_DIM), vtp.reshape(1, bp, n_keep, n_heads, HEAD_DIM),
            cnew_p[None], hlast_p.reshape(1, bp, d),
            kts.reshape(1, bs, ss, n_heads, HEAD_DIM), vts.reshape(1, bs, ss, n_heads, HEAD_DIM),
            cnew_s[None], hlast_s.reshape(1, bs, d))
```

```python
import functools

import jax
import jax.numpy as jnp
from jax import lax
from jax.experimental import pallas as pl
from jax.experimental.pallas import tpu as pltpu

F32 = jnp.float32
BF16 = jnp.bfloat16
I32 = jnp.int32

CHUNK = 64
LEFT_CHUNKS = 8
HEAD_DIM = 128
MAX_REL = 256
NEG_INF = -1e30
LRU_BLOCKS = 8
CONV_W = 4
LRU_C = 8.0
TOP_K = 4
SWIGLU_ALPHA = 1.702
SWIGLU_LIMIT = 7.0
RMS_EPS = 1e-6
PAST_LEN = 1024
LOG2_E = 1.4426950408889634
QK_SCALE = HEAD_DIM ** -0.5 * LOG2_E

LANES = 128
SUBLANES = 8
MXU_COLS = 256
VMEM_LIMIT = 56 * 1024 * 1024
MOE_BLOCK = 512
ROUTE_TILE = 1024
ROW_TILE = 512
ATT_GROUP = 4


def _cparams(sem):
    return pltpu.CompilerParams(dimension_semantics=sem, vmem_limit_bytes=VMEM_LIMIT)


def _sigmoid(x):
    return 0.5 * jnp.tanh(0.5 * x) + 0.5


def _rms(x, g):
    ms = jnp.mean(x * x, axis=-1, keepdims=True)
    return x * lax.rsqrt(ms + RMS_EPS) * g


def _inproj_kernel(x_ref, g_ref, w_ref, xr_ref, q_ref, k_ref, v_ref, gr_ref, ga_ref,
                   kt_ref, vt_ref, *, d):
    u = _rms(x_ref[...], g_ref[...]).astype(BF16)

    def piece(j):
        return jnp.dot(u, w_ref[:, j * d:(j + 1) * d], preferred_element_type=F32)

    xr_ref[...] = piece(0)
    q_ref[...] = (piece(1) * QK_SCALE).astype(BF16)
    kf = piece(2)
    k_ref[...] = kf.astype(BF16)
    kt_ref[...] = kf
    vf = piece(3)
    v_ref[...] = vf.astype(BF16)
    vt_ref[...] = vf
    gr_ref[...] = piece(4).astype(BF16)
    ga_ref[...] = _sigmoid(piece(5)).astype(BF16)


def _inproj(x2d, g, w_bf, *, tm):
    n, d = x2d.shape
    row = pl.BlockSpec((tm, d), lambda i: (i, 0))
    bf = jax.ShapeDtypeStruct((n, d), BF16)
    f32 = jax.ShapeDtypeStruct((n, d), F32)
    return pl.pallas_call(
        functools.partial(_inproj_kernel, d=d),
        grid=(n // tm,),
        in_specs=[row,
                  pl.BlockSpec((1, d), lambda i: (0, 0)),
                  pl.BlockSpec((d, 6 * d), lambda i: (0, 0), pipeline_mode=pl.Buffered(1))],
        out_specs=[row] * 8,
        out_shape=[f32, bf, bf, bf, bf, bf, f32, f32],
        compiler_params=_cparams(("arbitrary",)),
        name="inproj",
    )(x2d, g, w_bf)


def _lru_kernel(xr_ref, gr_ref, cs_ref, h0_ref, cw_ref, cb_ref, wg_ref, ba_ref, bx_ref, lam_ref,
                mr_ref, cnew_ref, hlast_ref, xp_s, a_s, b_s, hs_s, h_s, *, ts, starts_at_zero):
    _lru_init(pl.program_id(1), cs_ref, h0_ref, xp_s, h_s)
    _lru_steps(xr_ref[...], gr_ref[...].astype(F32), pl.program_id(1),
               cw_ref, cb_ref, wg_ref, ba_ref, bx_ref, lam_ref,
               mr_ref, cnew_ref, hlast_ref, xp_s, a_s, b_s, hs_s, h_s,
               ts=ts, starts_at_zero=starts_at_zero)


def _lru_init(t, cs_ref, h0_ref, xp_s, h_s):
    head = SUBLANES
    hist = CONV_W - 1

    @pl.when(t == 0)
    def _():
        xp_s[0:head, :] = jnp.zeros((head, xp_s.shape[1]), F32)
        xp_s[head - hist:head, :] = cs_ref[0]
        h_s[...] = jnp.broadcast_to(h0_ref[0], h_s.shape)


def _lru_steps(xr, gr, t, cw_ref, cb_ref, wg_ref, ba_ref, bx_ref, lam_ref,
               mr_ref, cnew_ref, hlast_ref, xp_s, a_s, b_s, hs_s, h_s, *, ts, starts_at_zero):
    head = SUBLANES
    hist = CONV_W - 1
    xp_s[head:head + ts, :] = xr
    cw = cw_ref[...]
    xc = cb_ref[...] + xp_s[head - hist:head - hist + ts, :] * cw[0:1, :]
    for j in range(1, CONV_W):
        xc = xc + xp_s[head - hist + j:head - hist + j + ts, :] * cw[j:j + 1, :]
    tail = xp_s[ts + head - hist:ts + head, :]
    xp_s[head - hist:head, :] = tail
    cnew_ref[0] = tail

    lam = lam_ref[...]
    z = -lam
    softplus = jnp.maximum(z, 0.0) + jnp.log1p(jnp.exp(-jnp.abs(z)))
    c = (-LRU_C * LOG2_E) * softplus
    bw = xc.shape[1] // LRU_BLOCKS
    if starts_at_zero:
        first = (lax.broadcasted_iota(I32, (ts, bw), 0) + t * ts) == 0
    for g in range(LRU_BLOCKS):
        sl = slice(g * bw, (g + 1) * bw)
        xg = xc[:, sl]
        pre = jnp.dot(xg.astype(BF16), wg_ref[g], preferred_element_type=F32)
        r = _sigmoid(pre[:, :bw] + ba_ref[:, sl])
        ig = _sigmoid(pre[:, bw:] + bx_ref[:, sl])
        a = jnp.exp2(c[:, sl] * r)
        rem = 1.0 - a * a
        mult = jnp.where(rem > 0.0, rem * lax.rsqrt(rem), 0.0)
        if starts_at_zero:
            mult = jnp.where(first, 1.0, mult)
        a_s[:, sl] = a
        b_s[:, sl] = mult * (ig * xg)

    sub = lax.broadcasted_iota(I32, (SUBLANES, xc.shape[1]), 0)

    def scan_body(i, h):
        base = pl.multiple_of(i * SUBLANES, SUBLANES)
        a = a_s[pl.ds(base, SUBLANES), :]
        b = b_s[pl.ds(base, SUBLANES), :]
        for shift in (1, 2, 4):
            a_prev = jnp.where(sub < shift, 1.0, pltpu.roll(a, shift, 0))
            b_prev = jnp.where(sub < shift, 0.0, pltpu.roll(b, shift, 0))
            b = a * b_prev + b
            a = a * a_prev
        rows = a * h + b
        hs_s[pl.ds(base, SUBLANES), :] = rows
        return jnp.broadcast_to(rows[SUBLANES - 1:SUBLANES, :], rows.shape)

    h = lax.fori_loop(0, ts // SUBLANES, scan_body, h_s[...])
    h_s[...] = h
    hlast_ref[0] = h[0:1, :]
    mr_ref[...] = (_sigmoid(gr) * hs_s[...]).astype(BF16)


def _lru(xr, gr, conv_state, h0, conv_w, conv_b, wg, ba, bx, lam, *, batch, seq, ts, starts_at_zero):
    n, d = xr.shape
    ntb = seq // ts
    row = pl.BlockSpec((ts, d), lambda b, t: (b * ntb + t, 0))
    vec = pl.BlockSpec((1, d), lambda b, t: (0, 0))
    return pl.pallas_call(
        functools.partial(_lru_kernel, ts=ts, starts_at_zero=starts_at_zero),
        grid=(batch, ntb),
        in_specs=[row, row,
                  pl.BlockSpec((1, CONV_W - 1, d), lambda b, t: (b, 0, 0)),
                  pl.BlockSpec((1, 1, d), lambda b, t: (b, 0, 0)),
                  pl.BlockSpec((CONV_W, d), lambda b, t: (0, 0)),
                  vec,
                  pl.BlockSpec(wg.shape, lambda b, t: (0, 0, 0)),
                  vec, vec, vec],
        out_specs=[row,
                   pl.BlockSpec((1, CONV_W - 1, d), lambda b, t: (b, 0, 0)),
                   pl.BlockSpec((1, 1, d), lambda b, t: (b, 0, 0))],
        out_shape=[jax.ShapeDtypeStruct((n, d), BF16),
                   jax.ShapeDtypeStruct((batch, CONV_W - 1, d), F32),
                   jax.ShapeDtypeStruct((batch, 1, d), F32)],
        scratch_shapes=_lru_scratch(ts, d),
        compiler_params=_cparams(("arbitrary", "arbitrary")),
        name="lru",
    )(xr, gr, conv_state, h0, conv_w, conv_b, wg, ba, bx, lam)


def _lru_scratch(ts, d):
    return [pltpu.VMEM((ts + SUBLANES, d), F32),
            pltpu.VMEM((ts, d), F32),
            pltpu.VMEM((ts, d), F32),
            pltpu.VMEM((ts, d), F32),
            pltpu.VMEM((SUBLANES, d), F32)]


def _mixer_kernel(x_ref, g_ref, w_ref, cs_ref, h0_ref, cw_ref, cb_ref, wg_ref, ba_ref, bx_ref,
                  lam_ref, q_ref, k_ref, v_ref, ga_ref, mr_ref, kt_ref, vt_ref, cnew_ref, hlast_ref,
                  xp_s, a_s, b_s, hs_s, h_s, *, d, ntb):
    t = pl.program_id(0) % ntb
    _lru_init(t, cs_ref, h0_ref, xp_s, h_s)
    u = _rms(x_ref[...], g_ref[...]).astype(BF16)

    def piece(j):
        return jnp.dot(u, w_ref[:, j * d:(j + 1) * d], preferred_element_type=F32)

    q_ref[...] = (piece(1) * QK_SCALE).astype(BF16)
    kf = piece(2)
    k_ref[...] = kf.astype(BF16)
    kt_ref[...] = kf
    vf = piece(3)
    v_ref[...] = vf.astype(BF16)
    vt_ref[...] = vf
    ga_ref[...] = _sigmoid(piece(5)).astype(BF16)
    _lru_steps(piece(0), piece(4), t,
               cw_ref, cb_ref, wg_ref, ba_ref, bx_ref, lam_ref,
               mr_ref, cnew_ref, hlast_ref, xp_s, a_s, b_s, hs_s, h_s,
               ts=x_ref.shape[0], starts_at_zero=True)


def _mixer_prompt(x2d, g, w_bf, conv_state, h0, conv_w, conv_b, wg, ba, bx, lam, *, batch, seq, n_keep, tm):
    n, d = x2d.shape
    ntb = seq // tm
    nk = n_keep // tm
    tail_map = lambda i: ((i // ntb) * nk + jnp.maximum(i % ntb - (ntb - nk), 0), 0)
    row = pl.BlockSpec((tm, d), lambda i: (i, 0))
    vec = pl.BlockSpec((1, d), lambda i: (0, 0))
    per_batch = lambda rows: pl.BlockSpec((1, rows, d), lambda i: (i // ntb, 0, 0))
    bf = jax.ShapeDtypeStruct((n, d), BF16)
    tail = jax.ShapeDtypeStruct((batch * n_keep, d), F32)
    return pl.pallas_call(
        functools.partial(_mixer_kernel, d=d, ntb=ntb),
        grid=(n // tm,),
        in_specs=[row, vec,
                  pl.BlockSpec((d, 6 * d), lambda i: (0, 0), pipeline_mode=pl.Buffered(1)),
                  per_batch(CONV_W - 1), per_batch(1),
                  pl.BlockSpec((CONV_W, d), lambda i: (0, 0)), vec,
                  pl.BlockSpec(wg.shape, lambda i: (0, 0, 0)), vec, vec, vec],
        out_specs=[row, row, row, row, row,
                   pl.BlockSpec((tm, d), tail_map), pl.BlockSpec((tm, d), tail_map),
                   per_batch(CONV_W - 1), per_batch(1)],
        out_shape=[bf, bf, bf, bf, bf, tail, tail,
                   jax.ShapeDtypeStruct((batch, CONV_W - 1, d), F32),
                   jax.ShapeDtypeStruct((batch, 1, d), F32)],
        scratch_shapes=_lru_scratch(tm, d),
        compiler_params=_cparams(("arbitrary",)),
        name="mixer",
    )(x2d, g, w_bf, conv_state, h0, conv_w, conv_b, wg, ba, bx, lam)


def _bias_kernel(f_ref, o_ref, *, nq, nkeys, width):
    f = f_ref[0]
    x = jnp.broadcast_to(f, (nq, width))
    rolled = pltpu.roll(x, width - (nq - 1), 1, stride=1, stride_axis=0)
    t = rolled[:, :nkeys]
    qc = lax.broadcasted_iota(I32, (nq, nkeys), 0) // CHUNK
    kc = lax.broadcasted_iota(I32, (nq, nkeys), 1) // CHUNK
    ok = jnp.logical_and(kc >= qc, kc <= qc + LEFT_CHUNKS)
    o_ref[0] = jnp.where(ok, t * LOG2_E, NEG_INF)


def _bias_table(rel_bias, nq, nkeys):
    nh = rel_bias.shape[0]
    band = LEFT_CHUNKS * CHUNK
    width = -(-(nkeys + nq) // LANES) * LANES
    left = band + (nq - 1) - MAX_REL
    right = max(width - left - (2 * MAX_REL + 1), 0)
    flipped = rel_bias[:, ::-1]
    f = jnp.pad(flipped, ((0, 0), (left, right)), mode="edge")[:, :width].reshape(nh, 1, width)
    return pl.pallas_call(
        functools.partial(_bias_kernel, nq=nq, nkeys=nkeys, width=width),
        grid=(nh,),
        in_specs=[pl.BlockSpec((1, 1, width), lambda h: (h, 0, 0))],
        out_specs=pl.BlockSpec((1, nq, nkeys), lambda h: (h, 0, 0)),
        out_shape=jax.ShapeDtypeStruct((nh, nq, nkeys), F32),
        compiler_params=_cparams(("arbitrary",)),
        name="bias",
    )(f)


def _attn_kernel(*refs, part_rows, head_major, n_heads, mask_parts):
    np_ = len(part_rows)
    q_ref = refs[0]
    k_refs = refs[1:1 + np_]
    v_refs = refs[1 + np_:1 + 2 * np_]
    bias_ref, ga_ref, mr_ref, o_ref = refs[1 + 2 * np_:]
    i = pl.program_id(1)

    def head(ref, p, h):
        if head_major[p]:
            return ref[pl.ds(h, part_rows[p], stride=n_heads), :].astype(BF16)
        return ref[:, h * HEAD_DIM:(h + 1) * HEAD_DIM].astype(BF16)

    def one_head(h, masked):
        hs = slice(h * HEAD_DIM, (h + 1) * HEAD_DIM)
        qh = q_ref[:, hs]
        s_parts = []
        off = 0
        for p in range(np_):
            kp = head(k_refs[p], p, h)
            s = lax.dot_general(qh, kp, (((1,), (1,)), ((), ())), preferred_element_type=F32)
            s = s + bias_ref[h, :, off:off + part_rows[p]]
            if masked and p < np_ - 1:
                s = jnp.where(i < (np_ - 1 - p), NEG_INF, s)
            s_parts.append(s)
            off += part_rows[p]
        if len(set(part_rows)) == 1:
            m = functools.reduce(jnp.maximum, s_parts).max(axis=-1, keepdims=True)
        else:
            m = functools.reduce(jnp.maximum, [s.max(axis=-1, keepdims=True) for s in s_parts])
        o = None
        for p in range(np_):
            e = jnp.exp2((s_parts[p] - m).astype(BF16))
            vp = head(v_refs[p], p, h)
            v_aug = jnp.concatenate([vp, jnp.ones(vp.shape, BF16)], axis=1)
            op = jnp.dot(e, v_aug, preferred_element_type=F32)
            o = op if o is None else o + op
        att = o[:, :HEAD_DIM] / o[:, HEAD_DIM:]
        mixed = mr_ref[:, hs].astype(F32) + ga_ref[:, hs].astype(F32) * att
        o_ref[:, hs] = mixed.astype(BF16)

    for h in range(n_heads):
        one_head(h, mask_parts)


def _attn_prompt(q, k, v, bias, ga, mr, *, batch, seq):
    n, d = q.shape
    gq = ATT_GROUP * CHUNK
    nparts = LEFT_CHUNKS // ATT_GROUP + 1
    ng = seq // gq
    cur = pl.BlockSpec((gq, d), lambda b, i: (b * ng + i, 0))

    def back(p):
        return pl.BlockSpec((gq, d), lambda b, i: (b * ng + jnp.maximum(i - p, 0), 0))

    kv_specs = [back(nparts - 1 - p) for p in range(nparts)]
    n_heads = d // HEAD_DIM
    return pl.pallas_call(
        functools.partial(_attn_kernel, part_rows=(gq,) * nparts, head_major=(False,) * nparts,
                          n_heads=n_heads, mask_parts=True),
        grid=(batch, ng),
        in_specs=[cur] + kv_specs + kv_specs
                 + [pl.BlockSpec(bias.shape, lambda b, i: (0, 0, 0), pipeline_mode=pl.Buffered(1)), cur, cur],
        out_specs=cur,
        out_shape=jax.ShapeDtypeStruct((n, d), BF16),
        compiler_params=_cparams(("arbitrary", "arbitrary")),
        name="attn_prompt",
    )(q, *([k] * nparts), *([v] * nparts), bias, ga, mr)


def _attn_sample(q, k, v, ck, cv, bias, ga, mr, *, batch, seq):
    n, d = q.shape
    n_heads = d // HEAD_DIM
    nc = ck.shape[0] // (batch * n_heads)
    cur = pl.BlockSpec((seq, d), lambda b, i: (b, 0))
    cache = pl.BlockSpec((nc * n_heads, HEAD_DIM), lambda b, i: (b, 0))
    return pl.pallas_call(
        functools.partial(_attn_kernel, part_rows=(nc, seq), head_major=(True, False),
                          n_heads=n_heads, mask_parts=False),
        grid=(batch, 1),
        in_specs=[cur, cache, cur, cache, cur,
                  pl.BlockSpec(bias.shape, lambda b, i: (0, 0, 0), pipeline_mode=pl.Buffered(1)), cur, cur],
        out_specs=cur,
        out_shape=jax.ShapeDtypeStruct((n, d), BF16),
        compiler_params=_cparams(("arbitrary", "arbitrary")),
        name="attn_sample",
    )(q, ck, k, cv, v, bias, ga, mr)


def _store_token_tiles(ref, x):
    rows = x.shape[0]
    for c in range(x.shape[1] // LANES):
        ref[pl.ds(c, rows, stride=SUBLANES), :] = x[:, c * LANES:(c + 1) * LANES]


def _load_token_tiles(ref, rows):
    return jnp.concatenate([ref[pl.ds(c, rows, stride=SUBLANES), :] for c in range(SUBLANES)], axis=1)


def _split_bf16(x):
    hi = x.astype(BF16)
    lo = (x - hi.astype(F32)).astype(BF16)
    return hi, lo


def _outproj_kernel(mp_ref, xp_ref, ms_ref, xs_ref, w_ref, g_ref, rwt_ref, rb_ref,
                    x2_ref, hn_ref, lg_ref, *, ntp):
    def run(mixed_ref, x_ref):
        x2 = x_ref[...] + jnp.dot(mixed_ref[...], w_ref[...], preferred_element_type=F32)
        x2_ref[...] = x2
        hn = _rms(x2, g_ref[...])
        _store_token_tiles(hn_ref, hn)
        h_hi, h_lo = _split_bf16(hn)
        w_hi, w_lo = _split_bf16(rwt_ref[...])
        nt = (((1,), (1,)), ((), ()))
        ne = w_hi.shape[0]
        both = lax.dot_general(jnp.concatenate([w_hi, w_lo], axis=0), h_hi, nt, preferred_element_type=F32)
        lg = both[:ne] + both[ne:] + lax.dot_general(w_hi, h_lo, nt, preferred_element_type=F32)
        lg_ref[...] = lg + rb_ref[...]

    i = pl.program_id(0)

    @pl.when(i < ntp)
    def _():
        run(mp_ref, xp_ref)

    @pl.when(i >= ntp)
    def _():
        run(ms_ref, xs_ref)


def _outproj(mixed_p, xp2d, mixed_s, xs2d, w_bf, g, rwt, rb, *, tm):
    tp, d = xp2d.shape
    tsm = xs2d.shape[0]
    ne = rwt.shape[0]
    ntp = tp // tm
    nts = tsm // tm
    total_rows = tp + tsm
    prow = pl.BlockSpec((tm, d), lambda i: (jnp.minimum(i, ntp - 1), 0))
    srow = pl.BlockSpec((tm, d), lambda i: (jnp.maximum(i - ntp, 0), 0))
    full = lambda shape: pl.BlockSpec(shape, lambda i: (0,) * len(shape))
    return pl.pallas_call(
        functools.partial(_outproj_kernel, ntp=ntp),
        grid=(ntp + nts,),
        in_specs=[prow, prow, srow, srow, full((d, d)), full((1, d)), full((ne, d)), full((ne, 1))],
        out_specs=[pl.BlockSpec((tm, d), lambda i: (i, 0)),
                   pl.BlockSpec((tm * SUBLANES, LANES), lambda i: (i, 0)),
                   pl.BlockSpec((ne, tm), lambda i: (0, i))],
        out_shape=[jax.ShapeDtypeStruct((total_rows, d), F32),
                   jax.ShapeDtypeStruct((total_rows * SUBLANES, LANES), F32),
                   jax.ShapeDtypeStruct((ne, total_rows), F32)],
        compiler_params=_cparams(("arbitrary",)),
        name="outproj",
    )(mixed_p, xp2d, mixed_s, xs2d, w_bf, g, rwt, rb)


def _route_kernel(lg_ref, e_ref, r_ref, g_ref, cnt_ref, carry_s, upper_s, *, tr):
    @pl.when(pl.program_id(0) == 0)
    def _():
        carry_s[...] = jnp.zeros(carry_s.shape, F32)
        rr = lax.broadcasted_iota(I32, (tr, tr), 0)
        cc = lax.broadcasted_iota(I32, (tr, tr), 1)
        upper_s[...] = jnp.where(rr < cc, 1.0, 0.0).astype(BF16)

    work = lg_ref[...]
    ne = work.shape[0]
    eid = lax.broadcasted_iota(I32, (ne, tr), 0).astype(F32)
    sels, vals, idxs = [], [], []
    for _ in range(TOP_K):
        m = work.max(axis=0, keepdims=True)
        idx = jnp.where(work == m, eid, float(ne)).min(axis=0, keepdims=True)
        sel = eid == idx
        sels.append(sel)
        vals.append(m)
        idxs.append(idx)
        work = jnp.where(sel, -jnp.inf, work)
    ex = [jnp.exp(v - vals[0]) for v in vals]
    den = ex[0] + ex[1] + ex[2] + ex[3]
    chosen = jnp.zeros((ne, tr), F32)
    for sel in sels:
        chosen = chosen + sel.astype(F32)
    before = jnp.dot(chosen.astype(BF16), upper_s[...], preferred_element_type=F32)
    carry = carry_s[:, 0:1]
    rank_all = before + carry
    for k in range(TOP_K):
        e_ref[k:k + 1, :] = idxs[k].astype(I32)
        g_ref[k:k + 1, :] = ex[k] / den
        r_ref[k:k + 1, :] = jnp.where(sels[k], rank_all, 0.0).sum(axis=0, keepdims=True).astype(I32)
    new_carry = carry + chosen.sum(axis=1, keepdims=True)
    carry_s[...] = jnp.broadcast_to(new_carry, carry_s.shape)
    cnt_ref[...] = jnp.broadcast_to(new_carry, cnt_ref.shape).astype(I32)


def _route(lgt, *, tr):
    ne, t = lgt.shape
    blk = lambda rows: pl.BlockSpec((rows, tr), lambda i: (0, i))
    return pl.pallas_call(
        functools.partial(_route_kernel, tr=tr),
        grid=(t // tr,),
        in_specs=[blk(ne)],
        out_specs=[blk(TOP_K), blk(TOP_K), blk(TOP_K), pl.BlockSpec((ne, LANES), lambda i: (0, 0))],
        out_shape=[jax.ShapeDtypeStruct((TOP_K, t), I32),
                   jax.ShapeDtypeStruct((TOP_K, t), I32),
                   jax.ShapeDtypeStruct((TOP_K, t), F32),
                   jax.ShapeDtypeStruct((ne, LANES), I32)],
        scratch_shapes=[pltpu.VMEM((ne, LANES), F32), pltpu.VMEM((tr, tr), BF16)],
        compiler_params=_cparams(("arbitrary",)),
        name="route",
    )(lgt)


def _plan_kernel(cnt_ref, meta_ref, be_ref, nx_ref, *, ne, nblocks, blk):
    acc = jnp.int32(0)
    for e in range(ne):
        c = cnt_ref[e, 0]
        pc = ((c + (blk - 1)) // blk) * blk
        meta_ref[e] = acc
        b0 = acc // blk
        acc = acc + pc
        meta_ref[ne + e] = acc
        b1 = acc // blk

        def fill(b, carry, e=e):
            be_ref[b] = jnp.int32(e)
            return carry

        lax.fori_loop(b0, b1, fill, 0)
    used = acc // blk
    meta_ref[2 * ne] = used

    def fill_rest(b, carry):
        be_ref[b] = jnp.int32(ne - 1)
        return carry

    lax.fori_loop(used, nblocks, fill_rest, 0)
    for j in range(2 * ne + 1, meta_ref.shape[0]):
        meta_ref[j] = jnp.int32(0)

    following = jnp.int32(-1)
    for e in reversed(range(ne)):
        b0 = meta_ref[e] // blk
        b1 = meta_ref[ne + e] // blk

        def fill_next(b, carry, following=following):
            nx_ref[b] = following
            return carry

        lax.fori_loop(b0, b1, fill_next, 0)
        following = jnp.where(b1 > b0, jnp.int32(e), following)

    def fill_next_rest(b, carry):
        nx_ref[b] = jnp.int32(-1)
        return carry

    lax.fori_loop(used, nblocks, fill_next_rest, 0)


def _plan(counts, *, nblocks, blk):
    ne = counts.shape[0]
    smem = pl.BlockSpec(memory_space=pltpu.SMEM)
    return pl.pallas_call(
        functools.partial(_plan_kernel, ne=ne, nblocks=nblocks, blk=blk),
        in_specs=[smem],
        out_specs=[smem, smem, smem],
        out_shape=[jax.ShapeDtypeStruct((LANES,), I32), jax.ShapeDtypeStruct((nblocks,), I32),
                   jax.ShapeDtypeStruct((nblocks,), I32)],
        name="plan",
    )(counts)


def _slots_kernel(e_ref, r_ref, cnt_ref, d_ref, start_s, *, blk):
    ne = cnt_ref.shape[0]
    tr = e_ref.shape[1]

    @pl.when(pl.program_id(0) == 0)
    def _():
        padded = jnp.bitwise_and(cnt_ref[...] + (blk - 1), -blk).astype(F32)
        eid1 = lax.broadcasted_iota(I32, padded.shape, 0)
        start = jnp.zeros(padded.shape, F32)
        for e in range(ne - 1):
            start = start + jnp.where(eid1 > e, padded[e:e + 1, :], 0.0)
        start_s[...] = start

    start = start_s[:, 0:1]
    eid = lax.broadcasted_iota(I32, (ne, tr), 0)
    for k in range(TOP_K):
        hit = eid == e_ref[k:k + 1, :]
        base = jnp.where(hit, start, 0.0).sum(axis=0, keepdims=True)
        d_ref[k:k + 1, :] = base.astype(I32) + r_ref[k:k + 1, :]


def _slots(e_all, r_all, counts, *, blk, tr):
    t = e_all.shape[1]
    blk_spec = pl.BlockSpec((TOP_K, tr), lambda i: (0, i))
    return pl.pallas_call(
        functools.partial(_slots_kernel, blk=blk),
        grid=(t // tr,),
        in_specs=[blk_spec, blk_spec, pl.BlockSpec(counts.shape, lambda i: (0, 0))],
        out_specs=blk_spec,
        out_shape=jax.ShapeDtypeStruct((TOP_K, t), I32),
        scratch_shapes=[pltpu.VMEM(counts.shape, F32)],
        compiler_params=_cparams(("arbitrary",)),
        name="slots",
    )(e_all, r_all, counts)


DISPATCH_RING = 3


def _dispatch_kernel(meta_ref, d_ref, hn_ref, xs_ref, zero_s, ring, zsem, in_sem, out_sem,
                     *, td, ne, blk, nblocks, nsteps):
    i = pl.program_id(0)
    rows = td * SUBLANES
    slot = i % DISPATCH_RING
    nxt = (i + 1) % DISPATCH_RING

    def load(step, s):
        src = hn_ref.at[pl.ds(pl.multiple_of(step * rows, rows), rows), :]
        return pltpu.make_async_copy(src, ring.at[s], in_sem.at[s])

    def wait_scatters(s):
        for _ in range(TOP_K):
            pltpu.make_async_copy(ring.at[s], xs_ref.at[pl.ds(0, rows), :], out_sem.at[s]).wait()

    @pl.when(i == 0)
    def _():
        zero_s[...] = jnp.zeros(zero_s.shape, F32)

        def zcopy(e):
            row0 = pl.multiple_of(jnp.maximum(meta_ref[ne + e] - blk, 0) * SUBLANES, SUBLANES)
            return pltpu.make_async_copy(zero_s, xs_ref.at[pl.ds(row0, blk * SUBLANES), :], zsem)

        for e in range(ne):
            zcopy(e).start()
        for e in range(ne):
            zcopy(e).wait()

        def ztail(b, carry):
            row0 = pl.multiple_of(b * (blk * SUBLANES), blk * SUBLANES)
            cp = pltpu.make_async_copy(zero_s, xs_ref.at[pl.ds(row0, blk * SUBLANES), :], zsem)
            cp.start()
            cp.wait()
            return carry

        lax.fori_loop(meta_ref[2 * ne], nblocks, ztail, 0)
        load(0, 0).start()

    @pl.when(i >= DISPATCH_RING - 1)
    def _():
        wait_scatters(nxt)

    @pl.when(i + 1 < nsteps)
    def _():
        load(i + 1, nxt).start()

    load(i, slot).wait()

    def issue(r, carry):
        src = ring.at[slot, pl.ds(pl.multiple_of(r * SUBLANES, SUBLANES), SUBLANES), :]
        for k in range(TOP_K):
            dst = xs_ref.at[pl.ds(pl.multiple_of(d_ref[r * TOP_K + k] * SUBLANES, SUBLANES), SUBLANES), :]
            pltpu.make_async_copy(src, dst, out_sem.at[slot]).start(priority=k % 2)
        return carry

    lax.fori_loop(0, td, issue, 0, unroll=2)

    @pl.when(i == nsteps - 1)
    def _():
        wait_scatters(slot)

        @pl.when(i >= 1)
        def _():
            wait_scatters((i + DISPATCH_RING - 1) % DISPATCH_RING)


def _dispatch(meta, dest, hn_all, *, n_slots, td, blk, ne):
    t = hn_all.shape[0] // SUBLANES
    nsteps = t // td
    assert nsteps >= DISPATCH_RING - 1
    grid_spec = pltpu.PrefetchScalarGridSpec(
        num_scalar_prefetch=1,
        grid=(nsteps,),
        in_specs=[pl.BlockSpec((td * TOP_K,), lambda i, m: (i,), memory_space=pltpu.SMEM),
                  pl.BlockSpec(memory_space=pl.ANY)],
        out_specs=pl.BlockSpec(memory_space=pl.ANY),
        scratch_shapes=[pltpu.VMEM((blk * SUBLANES, LANES), F32),
                        pltpu.VMEM((DISPATCH_RING, td * SUBLANES, LANES), F32),
                        pltpu.SemaphoreType.DMA(()),
                        pltpu.SemaphoreType.DMA((DISPATCH_RING,)),
                        pltpu.SemaphoreType.DMA((DISPATCH_RING,))],
    )
    return pl.pallas_call(
        functools.partial(_dispatch_kernel, td=td, ne=ne, blk=blk, nblocks=n_slots // blk, nsteps=nsteps),
        grid_spec=grid_spec,
        out_shape=jax.ShapeDtypeStruct((n_slots * SUBLANES, LANES), F32),
        compiler_params=_cparams(("arbitrary",)),
        name="dispatch",
    )(meta, dest, hn_all)


def _experts_kernel(be_ref, meta_ref, nx_ref, x_ref, w1_ref, b1g_ref, b1l_ref, w2_ref, b2_ref,
                    y_ref, wf1_s, wf2_s, w1_s, w2_s, sem, group_s, *, ne, blk):
    b = pl.program_id(0)
    used = meta_ref[2 * ne]
    half = MXU_COLS // 2
    ngroups = w1_s.shape[1] // MXU_COLS
    fresh = jnp.logical_or(b == 0, be_ref[b] != be_ref[jnp.maximum(b - 1, 0)])

    def fetch(e, s):
        return (pltpu.make_async_copy(w1_ref.at[e], wf1_s.at[s], sem.at[0, s]),
                pltpu.make_async_copy(w2_ref.at[e], wf2_s.at[s], sem.at[1, s]))

    @pl.when(b == 0)
    def _():
        group_s[0] = 0
        for cp in fetch(be_ref[0], 0):
            cp.start()

    @pl.when(jnp.logical_and(fresh, b < used))
    def _():
        g = group_s[0]
        s = g % 2
        for cp in fetch(be_ref[b], s):
            cp.wait()

        @pl.when(nx_ref[b] >= 0)
        def _():
            for cp in fetch(nx_ref[b], 1 - s):
                cp.start()

        group_s[0] = g + 1
        src = lax.broadcasted_iota(I32, (MXU_COLS, MXU_COLS), 0)
        dst = lax.broadcasted_iota(I32, (MXU_COLS, MXU_COLS), 1)
        want = jnp.where(dst < half, 2 * dst, 2 * (dst - half) + 1)
        perm = jnp.where(src == want, 1.0, 0.0).astype(BF16)
        for t in range(ngroups):
            cols = slice(t * MXU_COLS, (t + 1) * MXU_COLS)
            w = wf1_s[s, :, cols].astype(BF16)
            w1_s[:, cols] = jnp.dot(w, perm, preferred_element_type=F32).astype(BF16)
        w2_s[...] = wf2_s[s].astype(BF16)

    @pl.when(b < used)
    def _():
        xb = _load_token_tiles(x_ref, blk).astype(BF16)
        acts = []
        for t in range(ngroups):
            h = jnp.dot(xb, w1_s[:, t * MXU_COLS:(t + 1) * MXU_COLS], preferred_element_type=F32)
            hg = h[:, :half] + b1g_ref[0, :, t * half:(t + 1) * half]
            hl = h[:, half:] + b1l_ref[0, :, t * half:(t + 1) * half]
            g = jnp.minimum(hg, SWIGLU_LIMIT)
            lin = jnp.clip(hl, -SWIGLU_LIMIT, SWIGLU_LIMIT)
            acts.append((g * _sigmoid(SWIGLU_ALPHA * g) * (lin + 1.0)).astype(BF16))
        act = jnp.concatenate(acts, axis=1)
        y = jnp.dot(act, w2_s[...], preferred_element_type=F32) + b2_ref[0]
        _store_token_tiles(y_ref, y)

    @pl.when(b >= used)
    def _():
        y_ref[...] = jnp.zeros(y_ref.shape, F32)


def _experts(be, meta, nx, xs, w1, b1g, b1l, w2, b2, *, blk):
    ne, d, dff2 = w1.shape
    dff = dff2 // 2
    n_slots = xs.shape[0] // SUBLANES
    nblocks = n_slots // blk

    def xmap(b, be_r, meta_r, nx_r):
        return (jnp.minimum(b, meta_r[2 * ne] - 1), 0)

    def wmap(b, be_r, meta_r, nx_r):
        return (be_r[b], 0, 0)

    grid_spec = pltpu.PrefetchScalarGridSpec(
        num_scalar_prefetch=3,
        grid=(nblocks,),
        in_specs=[pl.BlockSpec((blk * SUBLANES, LANES), xmap),
                  pl.BlockSpec(memory_space=pl.ANY),
                  pl.BlockSpec((1, 1, dff), wmap),
                  pl.BlockSpec((1, 1, dff), wmap),
                  pl.BlockSpec(memory_space=pl.ANY),
                  pl.BlockSpec((1, 1, d), wmap)],
        out_specs=pl.BlockSpec((blk * SUBLANES, LANES), lambda b, be_r, meta_r, nx_r: (b, 0)),
        scratch_shapes=[pltpu.VMEM((2, d, dff2), F32), pltpu.VMEM((2, dff, d), F32),
                        pltpu.VMEM((d, dff2), BF16), pltpu.VMEM((dff, d), BF16),
                        pltpu.SemaphoreType.DMA((2, 2)), pltpu.SMEM((1,), I32)],
    )
    return pl.pallas_call(
        functools.partial(_experts_kernel, ne=ne, blk=blk),
        grid_spec=grid_spec,
        out_shape=jax.ShapeDtypeStruct((n_slots * SUBLANES, LANES), F32),
        compiler_params=_cparams(("arbitrary",)),
        name="experts",
    )(be, meta, nx, xs, w1, b1g, b1l, w2, b2)


def _combine_kernel(d_ref, dn_ref, x2_ref, gt_ref, gn_ref, ys_ref, y_ref, buf, sem, *, tc, nsteps):
    i = pl.program_id(0)
    slot = i % 2
    rows = tc * SUBLANES

    def gather(idx_ref, s):
        def body(r, carry):
            for k in range(TOP_K):
                src = ys_ref.at[pl.ds(pl.multiple_of(idx_ref[r * TOP_K + k] * SUBLANES, SUBLANES), SUBLANES), :]
                dst = buf.at[s, k, pl.ds(pl.multiple_of(r * SUBLANES, SUBLANES), SUBLANES), :]
                pltpu.make_async_copy(src, dst, sem.at[s]).start(priority=k % 2)
            return carry

        lax.fori_loop(0, tc, body, 0, unroll=4)

    @pl.when(i == 0)
    def _():
        gather(d_ref, 0)

    @pl.when(i + 1 < nsteps)
    def _():
        gather(dn_ref, 1 - slot)

    for k in range(TOP_K):
        pltpu.make_async_copy(ys_ref.at[pl.ds(0, rows), :], buf.at[slot, k], sem.at[slot]).wait()
    acc = x2_ref[...]
    for k in range(TOP_K):
        acc = acc + gt_ref[:, k:k + 1] * _load_token_tiles(buf.at[slot, k], tc)
    y_ref[...] = _rms(acc, gn_ref[...])


def _combine(dest, x2_all, gt_all, gn, ys, *, row_off, rows, tc):
    _, d = x2_all.shape
    boff = row_off // tc
    nsteps = rows // tc
    last = boff + nsteps - 1
    smem_blk = lambda fn: pl.BlockSpec((tc * TOP_K,), fn, memory_space=pltpu.SMEM)
    return pl.pallas_call(
        functools.partial(_combine_kernel, tc=tc, nsteps=nsteps),
        grid=(nsteps,),
        in_specs=[smem_blk(lambda i: (i + boff,)),
                  smem_blk(lambda i: (jnp.minimum(i + boff + 1, last),)),
                  pl.BlockSpec((tc, d), lambda i: (i + boff, 0)),
                  pl.BlockSpec((tc, TOP_K), lambda i: (i + boff, 0)),
                  pl.BlockSpec((1, d), lambda i: (0, 0)),
                  pl.BlockSpec(memory_space=pl.ANY)],
        out_specs=pl.BlockSpec((tc, d), lambda i: (i, 0)),
        out_shape=jax.ShapeDtypeStruct((rows, d), F32),
        scratch_shapes=[pltpu.VMEM((2, TOP_K, tc * SUBLANES, LANES), F32), pltpu.SemaphoreType.DMA((2,))],
        compiler_params=_cparams(("arbitrary",)),
        name="combine",
    )(dest, dest, x2_all, gt_all, gn, ys)


def _pick_tile(n, pref):
    t = min(pref, n)
    while n % t:
        t //= 2
    return t


def kernel(x_prompt, x_sample, cache_k, cache_v, state_conv, state_lru, norm_mix, w_in, conv_w, conv_b,
           lru_wa, lru_ba, lru_wx, lru_bx, lru_lambda, rel_bias, w_out, norm_ffn, router_w, router_b,
           w1, b1, w2, b2, norm_out):
    depth = w_in.shape[0]
    assert depth == 1, "single-layer step"
    bp, sp, d = x_prompt.shape
    bs, ss, _ = x_sample.shape
    n_heads = d // HEAD_DIM
    band = LEFT_CHUNKS * CHUNK
    n_keep = min(band, sp)
    n_cache = cache_k.shape[2]
    ne = router_w.shape[-1]
    dff = w2.shape[2]
    l = 0
    assert PAST_LEN % CHUNK == 0 and n_cache == band and ss == CHUNK
    assert d == SUBLANES * LANES, "token-tile layout holds one token per (8,128) tile"

    w_in_bf = w_in[l].astype(BF16)
    w_out_bf = w_out[l].astype(BF16)
    g_mix = norm_mix[l].reshape(1, d)
    g_ffn = norm_ffn[l].reshape(1, d)
    g_out = norm_out.reshape(1, d)
    wg = jnp.concatenate([lru_wa[l], lru_wx[l]], axis=-1).astype(BF16)
    ba = lru_ba[l].reshape(1, d)
    bx = lru_bx[l].reshape(1, d)
    lam = lru_lambda[l].reshape(1, d)
    cw = conv_w[l]
    cb = conv_b[l].reshape(1, d)
    rwt = router_w[l].T
    rb = router_b[l].reshape(ne, 1)
    b1g = b1[l][:, 0::2].reshape(ne, 1, dff)
    b1l = b1[l][:, 1::2].reshape(ne, 1, dff)
    b2r = b2[l].reshape(ne, 1, d)

    tp = bp * sp
    tsm = bs * ss
    t_all = tp + tsm

    gq = ATT_GROUP * CHUNK
    bias_p = _bias_table(rel_bias[l], gq, band + gq)
    bias_s = _bias_table(rel_bias[l], ss, n_cache + ss)

    xp2d = x_prompt.reshape(tp, d)
    conv0 = jnp.zeros((bp, CONV_W - 1, d), F32)
    lru0 = jnp.zeros((bp, 1, d), F32)
    qp, kp, vp, ga_p, mr_p, ktp, vtp, cnew_p, hlast_p = _mixer_prompt(
        xp2d, g_mix, w_in_bf, conv0, lru0, cw, cb, wg, ba, bx, lam,
        batch=bp, seq=sp, n_keep=n_keep, tm=_pick_tile(sp, ROW_TILE))
    mixed_p = _attn_prompt(qp, kp, vp, bias_p, ga_p, mr_p, batch=bp, seq=sp)

    xs2d = x_sample.reshape(tsm, d)
    xr, q, k, v, gr, ga_s, kts, vts = _inproj(xs2d, g_mix, w_in_bf, tm=_pick_tile(tsm, ROW_TILE))
    mr_s, cnew_s, hlast_s = _lru(xr, gr, state_conv[l], state_lru[l].reshape(bs, 1, d), cw, cb, wg, ba, bx, lam,
                                 batch=bs, seq=ss, ts=ss, starts_at_zero=False)
    ck = cache_k[l].reshape(bs * n_cache * n_heads, HEAD_DIM)
    cv = cache_v[l].reshape(bs * n_cache * n_heads, HEAD_DIM)
    mixed_s = _attn_sample(q, k, v, ck, cv, bias_s, ga_s, mr_s, batch=bs, seq=ss)

    tmo = _pick_tile(tsm, ROW_TILE)
    x2_all, hn_all, lgt_all = _outproj(mixed_p, xp2d, mixed_s, xs2d, w_out_bf, g_ffn, rwt, rb, tm=tmo)

    tr = _pick_tile(t_all, ROUTE_TILE)
    e_all, r_all, gates, counts = _route(lgt_all, tr=tr)
    blk = MOE_BLOCK
    nblocks = -(-(t_all * TOP_K) // blk) + ne
    meta, be, nx = _plan(counts, nblocks=nblocks, blk=blk)
    dest = _slots(e_all, r_all, counts, blk=blk, tr=tr).T.reshape(t_all * TOP_K)
    td = _pick_tile(t_all, 512)
    xs = _dispatch(meta, dest, hn_all, n_slots=nblocks * blk, td=td, blk=blk, ne=ne)
    ys = _experts(be, meta, nx, xs, w1[l], b1g, b1l, w2[l], b2r, blk=blk)
    gt_all = gates.T
    tc = _pick_tile(tsm, 256)
    y_p = _combine(dest, x2_all, gt_all, g_out, ys, row_off=0, rows=tp, tc=tc)
    y_s = _combine(dest, x2_all, gt_all, g_out, ys, row_off=tp, rows=tsm, tc=tc)

    return (y_p.reshape(bp, sp, d), y_s.reshape(bs, ss, d),
            ktp.reshape(1, bp, n_keep, n_heads, HEAD_DIM), vtp.reshape(1, bp, n_keep, n_heads, HEAD_DIM),
            cnew_p[None], hlast_p.reshape(1, bp, d),
            kts.reshape(1, bs, ss, n_heads, HEAD_DIM), vts.reshape(1, bs, ss, n_heads, HEAD_DIM),
            cnew_s[None], hlast_s.reshape(1, bs, d))
```

```python
import functools

import jax
import jax.numpy as jnp
from jax import lax
from jax.experimental import pallas as pl
from jax.experimental.pallas import tpu as pltpu

F32 = jnp.float32
BF16 = jnp.bfloat16
I32 = jnp.int32

CHUNK = 64
LEFT_CHUNKS = 8
HEAD_DIM = 128
MAX_REL = 256
NEG_INF = -1e30
LRU_BLOCKS = 8
CONV_W = 4
LRU_C = 8.0
TOP_K = 4
SWIGLU_ALPHA = 1.702
SWIGLU_LIMIT = 7.0
RMS_EPS = 1e-6
PAST_LEN = 1024
LOG2_E = 1.4426950408889634
QK_SCALE = HEAD_DIM ** -0.5 * LOG2_E

LANES = 128
SUBLANES = 8
MXU_COLS = 256
VMEM_LIMIT = 56 * 1024 * 1024
MOE_BLOCK = 512
ROUTE_TILE = 1024
ROW_TILE = 512
ATT_GROUP = 4


def _cparams(sem):
    return pltpu.CompilerParams(dimension_semantics=sem, vmem_limit_bytes=VMEM_LIMIT)


def _sigmoid(x):
    return 0.5 * jnp.tanh(0.5 * x) + 0.5


def _rms(x, g):
    ms = jnp.mean(x * x, axis=-1, keepdims=True)
    return x * lax.rsqrt(ms + RMS_EPS) * g


def _inproj_kernel(x_ref, g_ref, w_ref, xr_ref, q_ref, k_ref, v_ref, gr_ref, ga_ref,
                   kt_ref, vt_ref, *, d):
    u = _rms(x_ref[...], g_ref[...]).astype(BF16)

    def piece(j):
        return jnp.dot(u, w_ref[:, j * d:(j + 1) * d], preferred_element_type=F32)

    xr_ref[...] = piece(0)
    q_ref[...] = (piece(1) * QK_SCALE).astype(BF16)
    kf = piece(2)
    k_ref[...] = kf.astype(BF16)
    kt_ref[...] = kf
    vf = piece(3)
    v_ref[...] = vf.astype(BF16)
    vt_ref[...] = vf
    gr_ref[...] = piece(4).astype(BF16)
    ga_ref[...] = _sigmoid(piece(5)).astype(BF16)


def _inproj(x2d, g, w_bf, *, tm):
    n, d = x2d.shape
    row = pl.BlockSpec((tm, d), lambda i: (i, 0))
    bf = jax.ShapeDtypeStruct((n, d), BF16)
    f32 = jax.ShapeDtypeStruct((n, d), F32)
    return pl.pallas_call(
        functools.partial(_inproj_kernel, d=d),
        grid=(n // tm,),
        in_specs=[row,
                  pl.BlockSpec((1, d), lambda i: (0, 0)),
                  pl.BlockSpec((d, 6 * d), lambda i: (0, 0), pipeline_mode=pl.Buffered(1))],
        out_specs=[row] * 8,
        out_shape=[f32, bf, bf, bf, bf, bf, f32, f32],
        compiler_params=_cparams(("arbitrary",)),
        name="inproj",
    )(x2d, g, w_bf)


def _lru_kernel(xr_ref, gr_ref, cs_ref, h0_ref, cw_ref, cb_ref, wg_ref, ba_ref, bx_ref, lam_ref,
                mr_ref, cnew_ref, hlast_ref, xp_s, a_s, b_s, hs_s, h_s, *, ts, starts_at_zero):
    _lru_init(pl.program_id(1), cs_ref, h0_ref, xp_s, h_s)
    _lru_steps(xr_ref[...], gr_ref[...].astype(F32), pl.program_id(1),
               cw_ref, cb_ref, wg_ref, ba_ref, bx_ref, lam_ref,
               mr_ref, cnew_ref, hlast_ref, xp_s, a_s, b_s, hs_s, h_s,
               ts=ts, starts_at_zero=starts_at_zero)


def _lru_init(t, cs_ref, h0_ref, xp_s, h_s):
    head = SUBLANES
    hist = CONV_W - 1

    @pl.when(t == 0)
    def _():
        xp_s[0:head, :] = jnp.zeros((head, xp_s.shape[1]), F32)
        xp_s[head - hist:head, :] = cs_ref[0]
        h_s[...] = jnp.broadcast_to(h0_ref[0], h_s.shape)


def _lru_steps(xr, gr, t, cw_ref, cb_ref, wg_ref, ba_ref, bx_ref, lam_ref,
               mr_ref, cnew_ref, hlast_ref, xp_s, a_s, b_s, hs_s, h_s, *, ts, starts_at_zero):
    head = SUBLANES
    hist = CONV_W - 1
    xp_s[head:head + ts, :] = xr
    cw = cw_ref[...]
    xc = cb_ref[...] + xp_s[head - hist:head - hist + ts, :] * cw[0:1, :]
    for j in range(1, CONV_W):
        xc = xc + xp_s[head - hist + j:head - hist + j + ts, :] * cw[j:j + 1, :]
    tail = xp_s[ts + head - hist:ts + head, :]
    xp_s[head - hist:head, :] = tail
    cnew_ref[0] = tail

    lam = lam_ref[...]
    z = -lam
    softplus = jnp.maximum(z, 0.0) + jnp.log1p(jnp.exp(-jnp.abs(z)))
    c = (-LRU_C * LOG2_E) * softplus
    bw = xc.shape[1] // LRU_BLOCKS
    if starts_at_zero:
        first = (lax.broadcasted_iota(I32, (ts, bw), 0) + t * ts) == 0
    for g in range(LRU_BLOCKS):
        sl = slice(g * bw, (g + 1) * bw)
        xg = xc[:, sl]
        pre = jnp.dot(xg.astype(BF16), wg_ref[g], preferred_element_type=F32)
        r = _sigmoid(pre[:, :bw] + ba_ref[:, sl])
        ig = _sigmoid(pre[:, bw:] + bx_ref[:, sl])
        a = jnp.exp2(c[:, sl] * r)
        rem = 1.0 - a * a
        mult = jnp.where(rem > 0.0, rem * lax.rsqrt(rem), 0.0)
        if starts_at_zero:
            mult = jnp.where(first, 1.0, mult)
        a_s[:, sl] = a
        b_s[:, sl] = mult * (ig * xg)

    sub = lax.broadcasted_iota(I32, (SUBLANES, xc.shape[1]), 0)

    def scan_body(i, h):
        base = pl.multiple_of(i * SUBLANES, SUBLANES)
        a = a_s[pl.ds(base, SUBLANES), :]
        b = b_s[pl.ds(base, SUBLANES), :]
        for shift in (1, 2, 4):
            a_prev = jnp.where(sub < shift, 1.0, pltpu.roll(a, shift, 0))
            b_prev = jnp.where(sub < shift, 0.0, pltpu.roll(b, shift, 0))
            b = a * b_prev + b
            a = a * a_prev
        rows = a * h + b
        hs_s[pl.ds(base, SUBLANES), :] = rows
        return jnp.broadcast_to(rows[SUBLANES - 1:SUBLANES, :], rows.shape)

    h = lax.fori_loop(0, ts // SUBLANES, scan_body, h_s[...])
    h_s[...] = h
    hlast_ref[0] = h[0:1, :]
    mr_ref[...] = (_sigmoid(gr) * hs_s[...]).astype(BF16)


def _lru(xr, gr, conv_state, h0, conv_w, conv_b, wg, ba, bx, lam, *, batch, seq, ts, starts_at_zero):
    n, d = xr.shape
    ntb = seq // ts
    row = pl.BlockSpec((ts, d), lambda b, t: (b * ntb + t, 0))
    vec = pl.BlockSpec((1, d), lambda b, t: (0, 0))
    return pl.pallas_call(
        functools.partial(_lru_kernel, ts=ts, starts_at_zero=starts_at_zero),
        grid=(batch, ntb),
        in_specs=[row, row,
                  pl.BlockSpec((1, CONV_W - 1, d), lambda b, t: (b, 0, 0)),
                  pl.BlockSpec((1, 1, d), lambda b, t: (b, 0, 0)),
                  pl.BlockSpec((CONV_W, d), lambda b, t: (0, 0)),
                  vec,
                  pl.BlockSpec(wg.shape, lambda b, t: (0, 0, 0)),
                  vec, vec, vec],
        out_specs=[row,
                   pl.BlockSpec((1, CONV_W - 1, d), lambda b, t: (b, 0, 0)),
                   pl.BlockSpec((1, 1, d), lambda b, t: (b, 0, 0))],
        out_shape=[jax.ShapeDtypeStruct((n, d), BF16),
                   jax.ShapeDtypeStruct((batch, CONV_W - 1, d), F32),
                   jax.ShapeDtypeStruct((batch, 1, d), F32)],
        scratch_shapes=_lru_scratch(ts, d),
        compiler_params=_cparams(("arbitrary", "arbitrary")),
        name="lru",
    )(xr, gr, conv_state, h0, conv_w, conv_b, wg, ba, bx, lam)


def _lru_scratch(ts, d):
    return [pltpu.VMEM((ts + SUBLANES, d), F32),
            pltpu.VMEM((ts, d), F32),
            pltpu.VMEM((ts, d), F32),
            pltpu.VMEM((ts, d), F32),
            pltpu.VMEM((SUBLANES, d), F32)]


def _mixer_kernel(x_ref, g_ref, w_ref, cs_ref, h0_ref, cw_ref, cb_ref, wg_ref, ba_ref, bx_ref,
                  lam_ref, q_ref, k_ref, v_ref, ga_ref, mr_ref, kt_ref, vt_ref, cnew_ref, hlast_ref,
                  xp_s, a_s, b_s, hs_s, h_s, *, d, ntb):
    t = pl.program_id(0) % ntb
    _lru_init(t, cs_ref, h0_ref, xp_s, h_s)
    u = _rms(x_ref[...], g_ref[...]).astype(BF16)

    def piece(j):
        return jnp.dot(u, w_ref[:, j * d:(j + 1) * d], preferred_element_type=F32)

    q_ref[...] = (piece(1) * QK_SCALE).astype(BF16)
    kf = piece(2)
    k_ref[...] = kf.astype(BF16)
    kt_ref[...] = kf
    vf = piece(3)
    v_ref[...] = vf.astype(BF16)
    vt_ref[...] = vf
    ga_ref[...] = _sigmoid(piece(5)).astype(BF16)
    _lru_steps(piece(0), piece(4), t,
               cw_ref, cb_ref, wg_ref, ba_ref, bx_ref, lam_ref,
               mr_ref, cnew_ref, hlast_ref, xp_s, a_s, b_s, hs_s, h_s,
               ts=x_ref.shape[0], starts_at_zero=True)


def _mixer_prompt(x2d, g, w_bf, conv_state, h0, conv_w, conv_b, wg, ba, bx, lam, *, batch, seq, n_keep, tm):
    n, d = x2d.shape
    ntb = seq // tm
    nk = n_keep // tm
    tail_map = lambda i: ((i // ntb) * nk + jnp.maximum(i % ntb - (ntb - nk), 0), 0)
    row = pl.BlockSpec((tm, d), lambda i: (i, 0))
    vec = pl.BlockSpec((1, d), lambda i: (0, 0))
    per_batch = lambda rows: pl.BlockSpec((1, rows, d), lambda i: (i // ntb, 0, 0))
    bf = jax.ShapeDtypeStruct((n, d), BF16)
    tail = jax.ShapeDtypeStruct((batch * n_keep, d), F32)
    return pl.pallas_call(
        functools.partial(_mixer_kernel, d=d, ntb=ntb),
        grid=(n // tm,),
        in_specs=[row, vec,
                  pl.BlockSpec((d, 6 * d), lambda i: (0, 0), pipeline_mode=pl.Buffered(1)),
                  per_batch(CONV_W - 1), per_batch(1),
                  pl.BlockSpec((CONV_W, d), lambda i: (0, 0)), vec,
                  pl.BlockSpec(wg.shape, lambda i: (0, 0, 0)), vec, vec, vec],
        out_specs=[row, row, row, row, row,
                   pl.BlockSpec((tm, d), tail_map), pl.BlockSpec((tm, d), tail_map),
                   per_batch(CONV_W - 1), per_batch(1)],
        out_shape=[bf, bf, bf, bf, bf, tail, tail,
                   jax.ShapeDtypeStruct((batch, CONV_W - 1, d), F32),
                   jax.ShapeDtypeStruct((batch, 1, d), F32)],
        scratch_shapes=_lru_scratch(tm, d),
        compiler_params=_cparams(("arbitrary",)),
        name="mixer",
    )(x2d, g, w_bf, conv_state, h0, conv_w, conv_b, wg, ba, bx, lam)


def _bias_kernel(f_ref, o_ref, *, nq, nkeys, width):
    f = f_ref[0]
    x = jnp.broadcast_to(f, (nq, width))
    rolled = pltpu.roll(x, width - (nq - 1), 1, stride=1, stride_axis=0)
    t = rolled[:, :nkeys]
    qc = lax.broadcasted_iota(I32, (nq, nkeys), 0) // CHUNK
    kc = lax.broadcasted_iota(I32, (nq, nkeys), 1) // CHUNK
    ok = jnp.logical_and(kc >= qc, kc <= qc + LEFT_CHUNKS)
    o_ref[0] = jnp.where(ok, t * LOG2_E, NEG_INF)


def _bias_table(rel_bias, nq, nkeys):
    nh = rel_bias.shape[0]
    band = LEFT_CHUNKS * CHUNK
    width = -(-(nkeys + nq) // LANES) * LANES
    left = band + (nq - 1) - MAX_REL
    right = max(width - left - (2 * MAX_REL + 1), 0)
    flipped = rel_bias[:, ::-1]
    f = jnp.pad(flipped, ((0, 0), (left, right)), mode="edge")[:, :width].reshape(nh, 1, width)
    return pl.pallas_call(
        functools.partial(_bias_kernel, nq=nq, nkeys=nkeys, width=width),
        grid=(nh,),
        in_specs=[pl.BlockSpec((1, 1, width), lambda h: (h, 0, 0))],
        out_specs=pl.BlockSpec((1, nq, nkeys), lambda h: (h, 0, 0)),
        out_shape=jax.ShapeDtypeStruct((nh, nq, nkeys), F32),
        compiler_params=_cparams(("arbitrary",)),
        name="bias",
    )(f)


def _attn_kernel(*refs, part_rows, head_major, n_heads, mask_parts):
    np_ = len(part_rows)
    q_ref = refs[0]
    k_refs = refs[1:1 + np_]
    v_refs = refs[1 + np_:1 + 2 * np_]
    bias_ref, ga_ref, mr_ref, o_ref = refs[1 + 2 * np_:]
    i = pl.program_id(1)

    def head(ref, p, h):
        if head_major[p]:
            return ref[pl.ds(h, part_rows[p], stride=n_heads), :].astype(BF16)
        return ref[:, h * HEAD_DIM:(h + 1) * HEAD_DIM].astype(BF16)

    def one_head(h, masked):
        hs = slice(h * HEAD_DIM, (h + 1) * HEAD_DIM)
        qh = q_ref[:, hs]
        s_parts = []
        off = 0
        for p in range(np_):
            kp = head(k_refs[p], p, h)
            s = lax.dot_general(qh, kp, (((1,), (1,)), ((), ())), preferred_element_type=F32)
            s = s + bias_ref[h, :, off:off + part_rows[p]]
            if masked and p < np_ - 1:
                s = jnp.where(i < (np_ - 1 - p), NEG_INF, s)
            s_parts.append(s)
            off += part_rows[p]
        if len(set(part_rows)) == 1:
            m = functools.reduce(jnp.maximum, s_parts).max(axis=-1, keepdims=True)
        else:
            m = functools.reduce(jnp.maximum, [s.max(axis=-1, keepdims=True) for s in s_parts])
        o = None
        for p in range(np_):
            e = jnp.exp2((s_parts[p] - m).astype(BF16))
            vp = head(v_refs[p], p, h)
            v_aug = jnp.concatenate([vp, jnp.ones(vp.shape, BF16)], axis=1)
            op = jnp.dot(e, v_aug, preferred_element_type=F32)
            o = op if o is None else o + op
        att = o[:, :HEAD_DIM] / o[:, HEAD_DIM:]
        mixed = mr_ref[:, hs].astype(F32) + ga_ref[:, hs].astype(F32) * att
        o_ref[:, hs] = mixed.astype(BF16)

    for h in range(n_heads):
        one_head(h, mask_parts)


def _attn_prompt(q, k, v, bias, ga, mr, *, batch, seq):
    n, d = q.shape
    gq = ATT_GROUP * CHUNK
    nparts = LEFT_CHUNKS // ATT_GROUP + 1
    ng = seq // gq
    cur = pl.BlockSpec((gq, d), lambda b, i: (b * ng + i, 0))

    def back(p):
        return pl.BlockSpec((gq, d), lambda b, i: (b * ng + jnp.maximum(i - p, 0), 0))

    kv_specs = [back(nparts - 1 - p) for p in range(nparts)]
    n_heads = d // HEAD_DIM
    return pl.pallas_call(
        functools.partial(_attn_kernel, part_rows=(gq,) * nparts, head_major=(False,) * nparts,
                          n_heads=n_heads, mask_parts=True),
        grid=(batch, ng),
        in_specs=[cur] + kv_specs + kv_specs
                 + [pl.BlockSpec(bias.shape, lambda b, i: (0, 0, 0), pipeline_mode=pl.Buffered(1)), cur, cur],
        out_specs=cur,
        out_shape=jax.ShapeDtypeStruct((n, d), BF16),
        compiler_params=_cparams(("arbitrary", "arbitrary")),
        name="attn_prompt",
    )(q, *([k] * nparts), *([v] * nparts), bias, ga, mr)


def _attn_sample(q, k, v, ck, cv, bias, ga, mr, *, batch, seq):
    n, d = q.shape
    n_heads = d // HEAD_DIM
    nc = ck.shape[0] // (batch * n_heads)
    cur = pl.BlockSpec((seq, d), lambda b, i: (b, 0))
    cache = pl.BlockSpec((nc * n_heads, HEAD_DIM), lambda b, i: (b, 0))
    return pl.pallas_call(
        functools.partial(_attn_kernel, part_rows=(nc, seq), head_major=(True, False),
                          n_heads=n_heads, mask_parts=False),
        grid=(batch, 1),
        in_specs=[cur, cache, cur, cache, cur,
                  pl.BlockSpec(bias.shape, lambda b, i: (0, 0, 0), pipeline_mode=pl.Buffered(1)), cur, cur],
        out_specs=cur,
        out_shape=jax.ShapeDtypeStruct((n, d), BF16),
        compiler_params=_cparams(("arbitrary", "arbitrary")),
        name="attn_sample",
    )(q, ck, k, cv, v, bias, ga, mr)


def _store_token_tiles(ref, x):
    rows = x.shape[0]
    for c in range(x.shape[1] // LANES):
        ref[pl.ds(c, rows, stride=SUBLANES), :] = x[:, c * LANES:(c + 1) * LANES]


def _load_token_tiles(ref, rows):
    return jnp.concatenate([ref[pl.ds(c, rows, stride=SUBLANES), :] for c in range(SUBLANES)], axis=1)


def _split_bf16(x):
    hi = x.astype(BF16)
    lo = (x - hi.astype(F32)).astype(BF16)
    return hi, lo


def _outproj_kernel(mp_ref, xp_ref, ms_ref, xs_ref, w_ref, g_ref, rwt_ref, rb_ref,
                    x2_ref, hn_ref, lg_ref, *, ntp):
    def run(mixed_ref, x_ref):
        x2 = x_ref[...] + jnp.dot(mixed_ref[...], w_ref[...], preferred_element_type=F32)
        x2_ref[...] = x2
        hn = _rms(x2, g_ref[...])
        _store_token_tiles(hn_ref, hn)
        h_hi, h_lo = _split_bf16(hn)
        w_hi, w_lo = _split_bf16(rwt_ref[...])
        nt = (((1,), (1,)), ((), ()))
        ne = w_hi.shape[0]
        both = lax.dot_general(jnp.concatenate([w_hi, w_lo], axis=0), h_hi, nt, preferred_element_type=F32)
        lg = both[:ne] + both[ne:] + lax.dot_general(w_hi, h_lo, nt, preferred_element_type=F32)
        lg_ref[...] = lg + rb_ref[...]

    i = pl.program_id(0)

    @pl.when(i < ntp)
    def _():
        run(mp_ref, xp_ref)

    @pl.when(i >= ntp)
    def _():
        run(ms_ref, xs_ref)


def _outproj(mixed_p, xp2d, mixed_s, xs2d, w_bf, g, rwt, rb, *, tm):
    tp, d = xp2d.shape
    tsm = xs2d.shape[0]
    ne = rwt.shape[0]
    ntp = tp // tm
    nts = tsm // tm
    total_rows = tp + tsm
    prow = pl.BlockSpec((tm, d), lambda i: (jnp.minimum(i, ntp - 1), 0))
    srow = pl.BlockSpec((tm, d), lambda i: (jnp.maximum(i - ntp, 0), 0))
    full = lambda shape: pl.BlockSpec(shape, lambda i: (0,) * len(shape))
    return pl.pallas_call(
        functools.partial(_outproj_kernel, ntp=ntp),
        grid=(ntp + nts,),
        in_specs=[prow, prow, srow, srow, full((d, d)), full((1, d)), full((ne, d)), full((ne, 1))],
        out_specs=[pl.BlockSpec((tm, d), lambda i: (i, 0)),
                   pl.BlockSpec((tm * SUBLANES, LANES), lambda i: (i, 0)),
                   pl.BlockSpec((ne, tm), lambda i: (0, i))],
        out_shape=[jax.ShapeDtypeStruct((total_rows, d), F32),
                   jax.ShapeDtypeStruct((total_rows * SUBLANES, LANES), F32),
                   jax.ShapeDtypeStruct((ne, total_rows), F32)],
        compiler_params=_cparams(("arbitrary",)),
        name="outproj",
    )(mixed_p, xp2d, mixed_s, xs2d, w_bf, g, rwt, rb)


def _route_kernel(lg_ref, e_ref, r_ref, g_ref, cnt_ref, carry_s, *, tr):
    @pl.when(pl.program_id(0) == 0)
    def _():
        carry_s[...] = jnp.zeros(carry_s.shape, F32)

    work = lg_ref[...]
    ne = work.shape[0]
    eid = lax.broadcasted_iota(I32, (ne, tr), 0).astype(F32)
    sels, vals, idxs = [], [], []
    for _ in range(TOP_K):
        m = work.max(axis=0, keepdims=True)
        idx = jnp.where(work == m, eid, float(ne)).min(axis=0, keepdims=True)
        sel = eid == idx
        sels.append(sel)
        vals.append(m)
        idxs.append(idx)
        work = jnp.where(sel, -jnp.inf, work)
    ex = [jnp.exp(v - vals[0]) for v in vals]
    den = ex[0] + ex[1] + ex[2] + ex[3]
    chosen = jnp.zeros((ne, tr), F32)
    for sel in sels:
        chosen = chosen + sel.astype(F32)
    rr = lax.broadcasted_iota(I32, (tr, tr), 0)
    cc = lax.broadcasted_iota(I32, (tr, tr), 1)
    upper = jnp.where(rr < cc, 1.0, 0.0).astype(BF16)
    before = jnp.dot(chosen.astype(BF16), upper, preferred_element_type=F32)
    carry = carry_s[:, 0:1]
    rank_all = before + carry
    for k in range(TOP_K):
        e_ref[k:k + 1, :] = idxs[k].astype(I32)
        g_ref[k:k + 1, :] = ex[k] / den
        r_ref[k:k + 1, :] = jnp.where(sels[k], rank_all, 0.0).sum(axis=0, keepdims=True).astype(I32)
    new_carry = carry + chosen.sum(axis=1, keepdims=True)
    carry_s[...] = jnp.broadcast_to(new_carry, carry_s.shape)
    cnt_ref[...] = jnp.broadcast_to(new_carry, cnt_ref.shape).astype(I32)


def _route(lgt, *, tr):
    ne, t = lgt.shape
    blk = lambda rows: pl.BlockSpec((rows, tr), lambda i: (0, i))
    return pl.pallas_call(
        functools.partial(_route_kernel, tr=tr),
        grid=(t // tr,),
        in_specs=[blk(ne)],
        out_specs=[blk(TOP_K), blk(TOP_K), blk(TOP_K), pl.BlockSpec((ne, LANES), lambda i: (0, 0))],
        out_shape=[jax.ShapeDtypeStruct((TOP_K, t), I32),
                   jax.ShapeDtypeStruct((TOP_K, t), I32),
                   jax.ShapeDtypeStruct((TOP_K, t), F32),
                   jax.ShapeDtypeStruct((ne, LANES), I32)],
        scratch_shapes=[pltpu.VMEM((ne, LANES), F32)],
        compiler_params=_cparams(("arbitrary",)),
        name="route",
    )(lgt)


def _plan_kernel(cnt_ref, meta_ref, be_ref, nx_ref, *, ne, nblocks, blk):
    acc = jnp.int32(0)
    for e in range(ne):
        c = cnt_ref[e, 0]
        pc = ((c + (blk - 1)) // blk) * blk
        meta_ref[e] = acc
        b0 = acc // blk
        acc = acc + pc
        meta_ref[ne + e] = acc
        b1 = acc // blk

        def fill(b, carry, e=e):
            be_ref[b] = jnp.int32(e)
            return carry

        lax.fori_loop(b0, b1, fill, 0)
    used = acc // blk
    meta_ref[2 * ne] = used

    def fill_rest(b, carry):
        be_ref[b] = jnp.int32(ne - 1)
        return carry

    lax.fori_loop(used, nblocks, fill_rest, 0)
    for j in range(2 * ne + 1, meta_ref.shape[0]):
        meta_ref[j] = jnp.int32(0)

    following = jnp.int32(-1)
    for e in reversed(range(ne)):
        b0 = meta_ref[e] // blk
        b1 = meta_ref[ne + e] // blk

        def fill_next(b, carry, following=following):
            nx_ref[b] = following
            return carry

        lax.fori_loop(b0, b1, fill_next, 0)
        following = jnp.where(b1 > b0, jnp.int32(e), following)

    def fill_next_rest(b, carry):
        nx_ref[b] = jnp.int32(-1)
        return carry

    lax.fori_loop(used, nblocks, fill_next_rest, 0)


def _plan(counts, *, nblocks, blk):
    ne = counts.shape[0]
    smem = pl.BlockSpec(memory_space=pltpu.SMEM)
    return pl.pallas_call(
        functools.partial(_plan_kernel, ne=ne, nblocks=nblocks, blk=blk),
        in_specs=[smem],
        out_specs=[smem, smem, smem],
        out_shape=[jax.ShapeDtypeStruct((LANES,), I32), jax.ShapeDtypeStruct((nblocks,), I32),
                   jax.ShapeDtypeStruct((nblocks,), I32)],
        name="plan",
    )(counts)


def _slots_kernel(e_ref, r_ref, cnt_ref, d_ref, *, blk):
    ne = cnt_ref.shape[0]
    tr = e_ref.shape[1]
    padded = jnp.bitwise_and(cnt_ref[:, 0:1] + (blk - 1), -blk).astype(F32)
    eid1 = lax.broadcasted_iota(I32, (ne, 1), 0)
    start = jnp.zeros((ne, 1), F32)
    for e in range(ne - 1):
        start = start + jnp.where(eid1 > e, padded[e:e + 1, :], 0.0)
    eid = lax.broadcasted_iota(I32, (ne, tr), 0)
    for k in range(TOP_K):
        hit = eid == e_ref[k:k + 1, :]
        base = jnp.where(hit, start, 0.0).sum(axis=0, keepdims=True)
        d_ref[k:k + 1, :] = base.astype(I32) + r_ref[k:k + 1, :]


def _slots(e_all, r_all, counts, *, blk, tr):
    t = e_all.shape[1]
    blk_spec = pl.BlockSpec((TOP_K, tr), lambda i: (0, i))
    return pl.pallas_call(
        functools.partial(_slots_kernel, blk=blk),
        grid=(t // tr,),
        in_specs=[blk_spec, blk_spec, pl.BlockSpec(counts.shape, lambda i: (0, 0))],
        out_specs=blk_spec,
        out_shape=jax.ShapeDtypeStruct((TOP_K, t), I32),
        compiler_params=_cparams(("arbitrary",)),
        name="slots",
    )(e_all, r_all, counts)


DISPATCH_RING = 3


def _dispatch_kernel(meta_ref, d_ref, hn_ref, xs_ref, zero_s, ring, zsem, in_sem, out_sem,
                     *, td, ne, blk, nblocks, nsteps):
    i = pl.program_id(0)
    rows = td * SUBLANES
    slot = i % DISPATCH_RING
    nxt = (i + 1) % DISPATCH_RING

    def load(step, s):
        src = hn_ref.at[pl.ds(pl.multiple_of(step * rows, rows), rows), :]
        return pltpu.make_async_copy(src, ring.at[s], in_sem.at[s])

    def wait_scatters(s):
        for _ in range(TOP_K):
            pltpu.make_async_copy(ring.at[s], xs_ref.at[pl.ds(0, rows), :], out_sem.at[s]).wait()

    @pl.when(i == 0)
    def _():
        zero_s[...] = jnp.zeros(zero_s.shape, F32)

        def zcopy(e):
            row0 = pl.multiple_of(jnp.maximum(meta_ref[ne + e] - blk, 0) * SUBLANES, SUBLANES)
            return pltpu.make_async_copy(zero_s, xs_ref.at[pl.ds(row0, blk * SUBLANES), :], zsem)

        for e in range(ne):
            zcopy(e).start()
        for e in range(ne):
            zcopy(e).wait()

        def ztail(b, carry):
            row0 = pl.multiple_of(b * (blk * SUBLANES), blk * SUBLANES)
            cp = pltpu.make_async_copy(zero_s, xs_ref.at[pl.ds(row0, blk * SUBLANES), :], zsem)
            cp.start()
            cp.wait()
            return carry

        lax.fori_loop(meta_ref[2 * ne], nblocks, ztail, 0)
        load(0, 0).start()

    @pl.when(i >= DISPATCH_RING - 1)
    def _():
        wait_scatters(nxt)

    @pl.when(i + 1 < nsteps)
    def _():
        load(i + 1, nxt).start()

    load(i, slot).wait()

    def issue(r, carry):
        src = ring.at[slot, pl.ds(pl.multiple_of(r * SUBLANES, SUBLANES), SUBLANES), :]
        for k in range(TOP_K):
            dst = xs_ref.at[pl.ds(pl.multiple_of(d_ref[r * TOP_K + k] * SUBLANES, SUBLANES), SUBLANES), :]
            pltpu.make_async_copy(src, dst, out_sem.at[slot]).start(priority=k % 2)
        return carry

    lax.fori_loop(0, td, issue, 0, unroll=2)

    @pl.when(i == nsteps - 1)
    def _():
        wait_scatters(slot)

        @pl.when(i >= 1)
        def _():
            wait_scatters((i + DISPATCH_RING - 1) % DISPATCH_RING)


def _dispatch(meta, dest, hn_all, *, n_slots, td, blk, ne):
    t = hn_all.shape[0] // SUBLANES
    nsteps = t // td
    assert nsteps >= DISPATCH_RING - 1
    grid_spec = pltpu.PrefetchScalarGridSpec(
        num_scalar_prefetch=1,
        grid=(nsteps,),
        in_specs=[pl.BlockSpec((td * TOP_K,), lambda i, m: (i,), memory_space=pltpu.SMEM),
                  pl.BlockSpec(memory_space=pl.ANY)],
        out_specs=pl.BlockSpec(memory_space=pl.ANY),
        scratch_shapes=[pltpu.VMEM((blk * SUBLANES, LANES), F32),
                        pltpu.VMEM((DISPATCH_RING, td * SUBLANES, LANES), F32),
                        pltpu.SemaphoreType.DMA(()),
                        pltpu.SemaphoreType.DMA((DISPATCH_RING,)),
                        pltpu.SemaphoreType.DMA((DISPATCH_RING,))],
    )
    return pl.pallas_call(
        functools.partial(_dispatch_kernel, td=td, ne=ne, blk=blk, nblocks=n_slots // blk, nsteps=nsteps),
        grid_spec=grid_spec,
        out_shape=jax.ShapeDtypeStruct((n_slots * SUBLANES, LANES), F32),
        compiler_params=_cparams(("arbitrary",)),
        name="dispatch",
    )(meta, dest, hn_all)


def _experts_kernel(be_ref, meta_ref, nx_ref, x_ref, w1_ref, b1g_ref, b1l_ref, w2_ref, b2_ref,
                    y_ref, wf1_s, wf2_s, w1_s, w2_s, sem, group_s, *, ne, blk):
    b = pl.program_id(0)
    used = meta_ref[2 * ne]
    half = MXU_COLS // 2
    ngroups = w1_s.shape[1] // MXU_COLS
    fresh = jnp.logical_or(b == 0, be_ref[b] != be_ref[jnp.maximum(b - 1, 0)])

    def fetch(e, s):
        return (pltpu.make_async_copy(w1_ref.at[e], wf1_s.at[s], sem.at[0, s]),
                pltpu.make_async_copy(w2_ref.at[e], wf2_s.at[s], sem.at[1, s]))

    @pl.when(b == 0)
    def _():
        group_s[0] = 0
        for cp in fetch(be_ref[0], 0):
            cp.start()

    @pl.when(jnp.logical_and(fresh, b < used))
    def _():
        g = group_s[0]
        s = g % 2
        for cp in fetch(be_ref[b], s):
            cp.wait()

        @pl.when(nx_ref[b] >= 0)
        def _():
            for cp in fetch(nx_ref[b], 1 - s):
                cp.start()

        group_s[0] = g + 1
        src = lax.broadcasted_iota(I32, (MXU_COLS, MXU_COLS), 0)
        dst = lax.broadcasted_iota(I32, (MXU_COLS, MXU_COLS), 1)
        want = jnp.where(dst < half, 2 * dst, 2 * (dst - half) + 1)
        perm = jnp.where(src == want, 1.0, 0.0).astype(BF16)
        for t in range(ngroups):
            cols = slice(t * MXU_COLS, (t + 1) * MXU_COLS)
            w = wf1_s[s, :, cols].astype(BF16)
            w1_s[:, cols] = jnp.dot(w, perm, preferred_element_type=F32).astype(BF16)
        w2_s[...] = wf2_s[s].astype(BF16)

    @pl.when(b < used)
    def _():
        xb = _load_token_tiles(x_ref, blk).astype(BF16)
        acts = []
        for t in range(ngroups):
            h = jnp.dot(xb, w1_s[:, t * MXU_COLS:(t + 1) * MXU_COLS], preferred_element_type=F32)
            hg = h[:, :half] + b1g_ref[0, :, t * half:(t + 1) * half]
            hl = h[:, half:] + b1l_ref[0, :, t * half:(t + 1) * half]
            g = jnp.minimum(hg, SWIGLU_LIMIT)
            lin = jnp.clip(hl, -SWIGLU_LIMIT, SWIGLU_LIMIT)
            acts.append((g * _sigmoid(SWIGLU_ALPHA * g) * (lin + 1.0)).astype(BF16))
        act = jnp.concatenate(acts, axis=1)
        y = jnp.dot(act, w2_s[...], preferred_element_type=F32) + b2_ref[0]
        _store_token_tiles(y_ref, y)

    @pl.when(b >= used)
    def _():
        y_ref[...] = jnp.zeros(y_ref.shape, F32)


def _experts(be, meta, nx, xs, w1, b1g, b1l, w2, b2, *, blk):
    ne, d, dff2 = w1.shape
    dff = dff2 // 2
    n_slots = xs.shape[0] // SUBLANES
    nblocks = n_slots // blk

    def xmap(b, be_r, meta_r, nx_r):
        return (jnp.minimum(b, meta_r[2 * ne] - 1), 0)

    def wmap(b, be_r, meta_r, nx_r):
        return (be_r[b], 0, 0)

    grid_spec = pltpu.PrefetchScalarGridSpec(
        num_scalar_prefetch=3,
        grid=(nblocks,),
        in_specs=[pl.BlockSpec((blk * SUBLANES, LANES), xmap),
                  pl.BlockSpec(memory_space=pl.ANY),
                  pl.BlockSpec((1, 1, dff), wmap),
                  pl.BlockSpec((1, 1, dff), wmap),
                  pl.BlockSpec(memory_space=pl.ANY),
                  pl.BlockSpec((1, 1, d), wmap)],
        out_specs=pl.BlockSpec((blk * SUBLANES, LANES), lambda b, be_r, meta_r, nx_r: (b, 0)),
        scratch_shapes=[pltpu.VMEM((2, d, dff2), F32), pltpu.VMEM((2, dff, d), F32),
                        pltpu.VMEM((d, dff2), BF16), pltpu.VMEM((dff, d), BF16),
                        pltpu.SemaphoreType.DMA((2, 2)), pltpu.SMEM((1,), I32)],
    )
    return pl.pallas_call(
        functools.partial(_experts_kernel, ne=ne, blk=blk),
        grid_spec=grid_spec,
        out_shape=jax.ShapeDtypeStruct((n_slots * SUBLANES, LANES), F32),
        compiler_params=_cparams(("arbitrary",)),
        name="experts",
    )(be, meta, nx, xs, w1, b1g, b1l, w2, b2)


def _combine_kernel(d_ref, dn_ref, x2_ref, gt_ref, gn_ref, ys_ref, y_ref, buf, sem, *, tc, nsteps):
    i = pl.program_id(0)
    slot = i % 2
    rows = tc * SUBLANES

    def gather(idx_ref, s):
        def body(r, carry):
            for k in range(TOP_K):
                src = ys_ref.at[pl.ds(pl.multiple_of(idx_ref[r * TOP_K + k] * SUBLANES, SUBLANES), SUBLANES), :]
                dst = buf.at[s, k, pl.ds(pl.multiple_of(r * SUBLANES, SUBLANES), SUBLANES), :]
                pltpu.make_async_copy(src, dst, sem.at[s]).start(priority=k % 2)
            return carry

        lax.fori_loop(0, tc, body, 0, unroll=4)

    @pl.when(i == 0)
    def _():
        gather(d_ref, 0)

    @pl.when(i + 1 < nsteps)
    def _():
        gather(dn_ref, 1 - slot)

    for k in range(TOP_K):
        pltpu.make_async_copy(ys_ref.at[pl.ds(0, rows), :], buf.at[slot, k], sem.at[slot]).wait()
    acc = x2_ref[...]
    for k in range(TOP_K):
        acc = acc + gt_ref[:, k:k + 1] * _load_token_tiles(buf.at[slot, k], tc)
    y_ref[...] = _rms(acc, gn_ref[...])


def _combine(dest, x2_all, gt_all, gn, ys, *, row_off, rows, tc):
    _, d = x2_all.shape
    boff = row_off // tc
    nsteps = rows // tc
    last = boff + nsteps - 1
    smem_blk = lambda fn: pl.BlockSpec((tc * TOP_K,), fn, memory_space=pltpu.SMEM)
    return pl.pallas_call(
        functools.partial(_combine_kernel, tc=tc, nsteps=nsteps),
        grid=(nsteps,),
        in_specs=[smem_blk(lambda i: (i + boff,)),
                  smem_blk(lambda i: (jnp.minimum(i + boff + 1, last),)),
                  pl.BlockSpec((tc, d), lambda i: (i + boff, 0)),
                  pl.BlockSpec((tc, TOP_K), lambda i: (i + boff, 0)),
                  pl.BlockSpec((1, d), lambda i: (0, 0)),
                  pl.BlockSpec(memory_space=pl.ANY)],
        out_specs=pl.BlockSpec((tc, d), lambda i: (i, 0)),
        out_shape=jax.ShapeDtypeStruct((rows, d), F32),
        scratch_shapes=[pltpu.VMEM((2, TOP_K, tc * SUBLANES, LANES), F32), pltpu.SemaphoreType.DMA((2,))],
        compiler_params=_cparams(("arbitrary",)),
        name="combine",
    )(dest, dest, x2_all, gt_all, gn, ys)


def _pick_tile(n, pref):
    t = min(pref, n)
    while n % t:
        t //= 2
    return t


def kernel(x_prompt, x_sample, cache_k, cache_v, state_conv, state_lru, norm_mix, w_in, conv_w, conv_b,
           lru_wa, lru_ba, lru_wx, lru_bx, lru_lambda, rel_bias, w_out, norm_ffn, router_w, router_b,
           w1, b1, w2, b2, norm_out):
    depth = w_in.shape[0]
    assert depth == 1, "single-layer step"
    bp, sp, d = x_prompt.shape
    bs, ss, _ = x_sample.shape
    n_heads = d // HEAD_DIM
    band = LEFT_CHUNKS * CHUNK
    n_keep = min(band, sp)
    n_cache = cache_k.shape[2]
    ne = router_w.shape[-1]
    dff = w2.shape[2]
    l = 0
    assert PAST_LEN % CHUNK == 0 and n_cache == band and ss == CHUNK
    assert d == SUBLANES * LANES, "token-tile layout holds one token per (8,128) tile"

    w_in_bf = w_in[l].astype(BF16)
    w_out_bf = w_out[l].astype(BF16)
    g_mix = norm_mix[l].reshape(1, d)
    g_ffn = norm_ffn[l].reshape(1, d)
    g_out = norm_out.reshape(1, d)
    wg = jnp.concatenate([lru_wa[l], lru_wx[l]], axis=-1).astype(BF16)
    ba = lru_ba[l].reshape(1, d)
    bx = lru_bx[l].reshape(1, d)
    lam = lru_lambda[l].reshape(1, d)
    cw = conv_w[l]
    cb = conv_b[l].reshape(1, d)
    rwt = router_w[l].T
    rb = router_b[l].reshape(ne, 1)
    b1g = b1[l][:, 0::2].reshape(ne, 1, dff)
    b1l = b1[l][:, 1::2].reshape(ne, 1, dff)
    b2r = b2[l].reshape(ne, 1, d)

    tp = bp * sp
    tsm = bs * ss
    t_all = tp + tsm

    gq = ATT_GROUP * CHUNK
    bias_p = _bias_table(rel_bias[l], gq, band + gq)
    bias_s = _bias_table(rel_bias[l], ss, n_cache + ss)

    xp2d = x_prompt.reshape(tp, d)
    conv0 = jnp.zeros((bp, CONV_W - 1, d), F32)
    lru0 = jnp.zeros((bp, 1, d), F32)
    qp, kp, vp, ga_p, mr_p, ktp, vtp, cnew_p, hlast_p = _mixer_prompt(
        xp2d, g_mix, w_in_bf, conv0, lru0, cw, cb, wg, ba, bx, lam,
        batch=bp, seq=sp, n_keep=n_keep, tm=_pick_tile(sp, ROW_TILE))
    mixed_p = _attn_prompt(qp, kp, vp, bias_p, ga_p, mr_p, batch=bp, seq=sp)

    xs2d = x_sample.reshape(tsm, d)
    xr, q, k, v, gr, ga_s, kts, vts = _inproj(xs2d, g_mix, w_in_bf, tm=_pick_tile(tsm, ROW_TILE))
    mr_s, cnew_s, hlast_s = _lru(xr, gr, state_conv[l], state_lru[l].reshape(bs, 1, d), cw, cb, wg, ba, bx, lam,
                                 batch=bs, seq=ss, ts=ss, starts_at_zero=False)
    ck = cache_k[l].reshape(bs * n_cache * n_heads, HEAD_DIM)
    cv = cache_v[l].reshape(bs * n_cache * n_heads, HEAD_DIM)
    mixed_s = _attn_sample(q, k, v, ck, cv, bias_s, ga_s, mr_s, batch=bs, seq=ss)

    tmo = _pick_tile(tsm, 2 * ROW_TILE)
    x2_all, hn_all, lgt_all = _outproj(mixed_p, xp2d, mixed_s, xs2d, w_out_bf, g_ffn, rwt, rb, tm=tmo)

    tr = _pick_tile(t_all, ROUTE_TILE)
    e_all, r_all, gates, counts = _route(lgt_all, tr=tr)
    blk = MOE_BLOCK
    nblocks = -(-(t_all * TOP_K) // blk) + ne
    meta, be, nx = _plan(counts, nblocks=nblocks, blk=blk)
    dest = _slots(e_all, r_all, counts, blk=blk, tr=tr).T.reshape(t_all * TOP_K)
    td = _pick_tile(t_all, 1024)
    xs = _dispatch(meta, dest, hn_all, n_slots=nblocks * blk, td=td, blk=blk, ne=ne)
    ys = _experts(be, meta, nx, xs, w1[l], b1g, b1l, w2[l], b2r, blk=blk)
    gt_all = gates.T
    tc = _pick_tile(tsm, 512)
    y_p = _combine(dest, x2_all, gt_all, g_out, ys, row_off=0, rows=tp, tc=tc)
    y_s = _combine(dest, x2_all, gt_all, g_out, ys, row_off=tp, rows=tsm, tc=tc)

    return (y_p.reshape(bp, sp, d), y_s.reshape(bs, ss, d),
            ktp.reshape(1, bp, n_keep, n_heads, HEAD_DIM), vtp.reshape(1, bp, n_keep, n_heads, HEAD_DIM),
            cnew_p[None], hlast_p.reshape(1, bp, d),
            kts.reshape(1, bs, ss, n_heads, HEAD_DIM), vts.reshape(1, bs, ss, n_heads, HEAD_DIM),
            cnew_s[None], hlast_s.reshape(1, bs, d))
```
